```python
import math
import jax, jax.numpy as jnp
from jax import lax
import numpy as np

D_MODEL = 1024
BATCH = 8
SEQ = 4096
DEPTH = 1

CHUNK = 64
D_MIX = D_MODEL
SSD_HEADS = 8
SSD_HEAD_DIM = 64
SSD_GROUPS = 2
SSD_STATE = 128
SSD_CONV = 4
D_SSD = SSD_HEADS * SSD_HEAD_DIM
D_XBC = D_SSD + 2 * SSD_GROUPS * SSD_STATE
FOX_HEADS = 8
FOX_HEAD_DIM = 64
D_FOX = FOX_HEADS * FOX_HEAD_DIM
Q_BLOCK = 128
PROJ_SPLITS = (D_SSD, D_SSD + D_XBC, D_SSD + D_XBC + SSD_HEADS,
               D_SSD + D_XBC + SSD_HEADS + D_FOX,
               D_SSD + D_XBC + SSD_HEADS + 2 * D_FOX,
               D_SSD + D_XBC + SSD_HEADS + 3 * D_FOX)
D_IN_PROJ = D_SSD + D_XBC + SSD_HEADS + 3 * D_FOX + FOX_HEADS
N_EXPERT_GROUPS = 4
EXPERTS_PER_GROUP = 4
TOP_K_IN_GROUP = 2
D_EXPERT = 512
D_PLE = 256
EPS = 1e-6

kernel_name = 'hymba_ssd_fox_hiermoe_ple'


def rms_norm(x, g):
    xf = x.astype(jnp.float32)
    y = xf * lax.rsqrt(jnp.mean(xf * xf, axis=-1, keepdims=True) + EPS)
    return (y * g.astype(jnp.float32)).astype(x.dtype)


def ssd_scan(xs, dt, a, bm, cm):
    b, s, G, R, P = xs.shape
    N = bm.shape[-1]
    nc = s // CHUNK
    xdt = (xs * dt[..., None]).reshape(b, nc, CHUNK, G, R, P)
    adt = (dt * a).reshape(b, nc, CHUNK, G, R).transpose(0, 3, 4, 1, 2)
    a_cs = jnp.cumsum(adt, axis=-1)
    bc = bm.reshape(b, nc, CHUNK, G, N).astype(xdt.dtype)
    cc = cm.reshape(b, nc, CHUNK, G, N).astype(xdt.dtype)
    causal = jnp.tril(jnp.ones((CHUNK, CHUNK), dtype=bool))
    seg = a_cs[..., :, None] - a_cs[..., None, :]
    decay_in = jnp.exp(jnp.where(causal, seg, -jnp.inf))
    cb = jnp.einsum('bclgn,bcsgn->bgcls', cc, bc)
    y_diag = jnp.einsum('bgcls,bgrcls,bcsgrp->bclgrp', cb, decay_in, xdt)
    decay_st = jnp.exp(a_cs[..., -1:] - a_cs)
    states = jnp.einsum('bclgn,bgrcl,bclgrp->cbgrpn', bc, decay_st, xdt)
    chunk_decay = jnp.exp(a_cs[..., -1]).transpose(3, 0, 1, 2)

    def step(carry, inp):
        st, dec = inp
        return carry * dec[..., None, None] + st, carry

    init = jnp.zeros((b, G, R, P, N), dtype=states.dtype)
    _, prev = lax.scan(step, init, (states, chunk_decay))
    y_off = jnp.einsum('bclgn,cbgrpn,bgrcl->bclgrp', cc, prev, jnp.exp(a_cs))
    return (y_diag + y_off).reshape(b, s, G, R, P)


def ssd_mixer(z, xbc, dt_raw, conv_w, conv_b, dt_bias, a_log, d_skip, g_norm):
    b, s, _ = xbc.shape
    R = SSD_HEADS // SSD_GROUPS
    xbc = lax.conv_general_dilated(xbc, conv_w, window_strides=(1,),
                                   padding=[(SSD_CONV - 1, 0)],
                                   dimension_numbers=('NWC', 'WIO', 'NWC'),
                                   feature_group_count=D_XBC) + conv_b
    xbc = jax.nn.silu(xbc)
    xs, bm, cm = jnp.split(xbc, [D_SSD, D_SSD + SSD_GROUPS * SSD_STATE], axis=-1)
    xs = xs.reshape(b, s, SSD_GROUPS, R, SSD_HEAD_DIM)
    bm = bm.reshape(b, s, SSD_GROUPS, SSD_STATE)
    cm = cm.reshape(b, s, SSD_GROUPS, SSD_STATE)
    dt = jax.nn.softplus(dt_raw.astype(jnp.float32) + dt_bias.astype(jnp.float32))
    dt = dt.reshape(b, s, SSD_GROUPS, R)
    a = -jnp.exp(a_log.astype(jnp.float32)).reshape(SSD_GROUPS, R)
    y = ssd_scan(xs, dt, a, bm, cm)
    y = y + d_skip.reshape(SSD_GROUPS, R)[:, :, None] * xs
    y = y.reshape(b, s, D_SSD)
    return rms_norm(y * jax.nn.silu(z.astype(y.dtype)), g_norm)


def fox_mixer(q, k, v, f_logit, f_bias):
    b, s, _ = q.shape
    q = q.reshape(b, s, FOX_HEADS, FOX_HEAD_DIM).transpose(0, 2, 1, 3)
    k = k.reshape(b, s, FOX_HEADS, FOX_HEAD_DIM).transpose(0, 2, 1, 3)
    v = v.reshape(b, s, FOX_HEADS, FOX_HEAD_DIM).transpose(0, 2, 1, 3)
    log_f = jax.nn.log_sigmoid((f_logit + f_bias).astype(jnp.float32))
    f_cum = jnp.cumsum(log_f, axis=1).transpose(0, 2, 1)
    scale = FOX_HEAD_DIM ** -0.5
    outs = []
    for blk in range(s // Q_BLOCK):
        q0 = blk * Q_BLOCK
        q1 = q0 + Q_BLOCK
        logits = jnp.einsum('bhqd,bhkd->bhqk', q[:, :, q0:q1], k[:, :, :q1]).astype(jnp.float32) * scale
        logits = logits + f_cum[:, :, q0:q1, None] - f_cum[:, :, None, :q1]
        mask = (q0 + jnp.arange(Q_BLOCK))[:, None] >= jnp.arange(q1)[None, :]
        probs = jax.nn.softmax(jnp.where(mask, logits, -jnp.inf), axis=-1)
        outs.append(jnp.einsum('bhqk,bhkd->bhqd', probs.astype(v.dtype), v[:, :, :q1]))
    o = jnp.concatenate(outs, axis=2)
    return o.transpose(0, 2, 1, 3).reshape(b, s, D_FOX)


def hier_moe(h, w_rg, b_rg, w_re, b_re, w_gate, w_up, w_down):
    b, s, _ = h.shape
    grp_prob = jax.nn.softmax((h @ w_rg + b_rg).astype(jnp.float32), axis=-1)
    p_grp, g_idx = lax.top_k(grp_prob, 1)
    grp_onehot = jax.nn.one_hot(g_idx[..., 0], N_EXPERT_GROUPS, dtype=jnp.float32)
    exp_logits = (h @ w_re + b_re).astype(jnp.float32).reshape(b, s, N_EXPERT_GROUPS, EXPERTS_PER_GROUP)
    in_grp = jnp.einsum('bsg,bsge->bse', grp_onehot, exp_logits)
    top_val, top_idx = lax.top_k(in_grp, TOP_K_IN_GROUP)
    w_top = jax.nn.softmax(top_val, axis=-1) * p_grp
    w_exp = jnp.sum(jax.nn.one_hot(top_idx, EXPERTS_PER_GROUP, dtype=jnp.float32) * w_top[..., None], axis=-2)
    combine = (grp_onehot[..., :, None] * w_exp[..., None, :]).astype(h.dtype)
    y = jnp.zeros_like(h)
    for g in range(N_EXPERT_GROUPS):
        act = jax.nn.silu(jnp.einsum('bsd,edf->bsef', h, w_gate[g]))
        hid = act * jnp.einsum('bsd,edf->bsef', h, w_up[g]) * combine[:, :, g, :, None]
        y = y + jnp.einsum('bsef,efd->bsd', hid, w_down[g])
    return y


def setup_inputs(seed: int = 0) -> dict:
    key = jax.random.key(seed)
    ks = jax.random.split(key, 24)
    nrm = jax.random.normal
    L = DEPTH
    x = nrm(ks[0], (BATCH, SEQ, D_MODEL), jnp.float32)
    p = nrm(ks[1], (L, BATCH, SEQ, D_PLE), jnp.float32)
    w_in = nrm(ks[2], (L, D_MODEL, D_IN_PROJ), jnp.float32) * D_MODEL ** -0.5
    conv_w = nrm(ks[3], (L, SSD_CONV, 1, D_XBC), jnp.float32) * SSD_CONV ** -0.5
    conv_b = 0.01 * nrm(ks[4], (L, D_XBC), jnp.float32)
    u = jax.random.uniform(ks[5], (L, SSD_HEADS), jnp.float32)
    dt0 = jnp.exp(u * (math.log(0.1) - math.log(0.001)) + math.log(0.001))
    dt_bias = dt0 + jnp.log(-jnp.expm1(-dt0))
    a_log = jnp.log(jax.random.uniform(ks[6], (L, SSD_HEADS), jnp.float32, 1.0, 16.0))
    d_skip = 1.0 + 0.1 * nrm(ks[7], (L, SSD_HEADS), jnp.float32)
    g_ssd = 1.0 + 0.05 * nrm(ks[8], (L, D_SSD), jnp.float32)
    fox_fbias = 2.0 + 0.5 * nrm(ks[9], (L, FOX_HEADS), jnp.float32)
    w_out = nrm(ks[10], (L, D_MIX, D_MODEL), jnp.float32) * D_MIX ** -0.5
    g_mix = 1.0 + 0.05 * nrm(ks[11], (L, D_MODEL), jnp.float32)
    g_ffn = 1.0 + 0.05 * nrm(ks[12], (L, D_MODEL), jnp.float32)
    w_route_group = nrm(ks[13], (L, D_MODEL, N_EXPERT_GROUPS), jnp.float32) * D_MODEL ** -0.5
    b_route_group = 0.01 * nrm(ks[14], (L, N_EXPERT_GROUPS), jnp.float32)
    w_route_expert = nrm(ks[15], (L, D_MODEL, N_EXPERT_GROUPS * EXPERTS_PER_GROUP), jnp.float32) * D_MODEL ** -0.5
    b_route_expert = 0.01 * nrm(ks[16], (L, N_EXPERT_GROUPS * EXPERTS_PER_GROUP), jnp.float32)
    eshape = (L, N_EXPERT_GROUPS, EXPERTS_PER_GROUP, D_MODEL, D_EXPERT)
    w_exp_gate = nrm(ks[17], eshape, jnp.float32) * D_MODEL ** -0.5
    w_exp_up = nrm(ks[18], eshape, jnp.float32) * D_MODEL ** -0.5
    w_exp_down = nrm(ks[19], (L, N_EXPERT_GROUPS, EXPERTS_PER_GROUP, D_EXPERT, D_MODEL), jnp.float32) * D_EXPERT ** -0.5
    g_ple = 1.0 + 0.05 * nrm(ks[20], (L, D_MODEL), jnp.float32)
    w_ple_proj = nrm(ks[21], (L, D_PLE, D_MODEL), jnp.float32) * D_PLE ** -0.5
    w_ple_gate = nrm(ks[22], (L, D_MODEL, D_MODEL), jnp.float32) * D_MODEL ** -0.5
    g_final = 1.0 + 0.05 * nrm(ks[23], (D_MODEL,), jnp.float32)
    return {'x': x, 'p': p, 'w_in': w_in, 'conv_w': conv_w, 'conv_b': conv_b,
            'dt_bias': dt_bias, 'a_log': a_log, 'd_skip': d_skip, 'g_ssd': g_ssd,
            'fox_fbias': fox_fbias, 'w_out': w_out, 'g_mix': g_mix, 'g_ffn': g_ffn,
            'w_route_group': w_route_group, 'b_route_group': b_route_group,
            'w_route_expert': w_route_expert, 'b_route_expert': b_route_expert,
            'w_exp_gate': w_exp_gate, 'w_exp_up': w_exp_up, 'w_exp_down': w_exp_down,
            'g_ple': g_ple, 'w_ple_proj': w_ple_proj, 'w_ple_gate': w_ple_gate,
            'g_final': g_final}


def reference(x, p, w_in, conv_w, conv_b, dt_bias, a_log, d_skip, g_ssd, fox_fbias,
              w_out, g_mix, g_ffn, w_route_group, b_route_group, w_route_expert,
              b_route_expert, w_exp_gate, w_exp_up, w_exp_down, g_ple, w_ple_proj,
              w_ple_gate, g_final):
    h = x
    for i in range(DEPTH):
        hn = rms_norm(h, g_mix[i])
        proj = hn @ w_in[i]
        z, xbc, dt_raw, q, k, v, f_logit = jnp.split(proj, PROJ_SPLITS, axis=-1)
        y_ssd = ssd_mixer(z, xbc, dt_raw, conv_w[i], conv_b[i], dt_bias[i],
                          a_log[i], d_skip[i], g_ssd[i])
        y_fox = fox_mixer(q, k, v, f_logit, fox_fbias[i])
        mixed = jnp.concatenate([y_ssd.astype(h.dtype), y_fox.astype(h.dtype)], axis=-1)
        h = h + mixed @ w_out[i]
        h = h + hier_moe(rms_norm(h, g_ffn[i]), w_route_group[i], b_route_group[i],
                         w_route_expert[i], b_route_expert[i], w_exp_gate[i],
                         w_exp_up[i], w_exp_down[i])
        gate = jax.nn.sigmoid((rms_norm(h, g_ple[i]) @ w_ple_gate[i]).astype(jnp.float32))
        h = h + ((p[i] @ w_ple_proj[i]).astype(jnp.float32) * gate).astype(h.dtype)
    return rms_norm(h, g_final)
```

```python
import functools

import jax
import jax.numpy as jnp
from jax import lax
from jax.experimental import pallas as pl
from jax.experimental.pallas import tpu as pltpu

F32 = jnp.float32
BF16 = jnp.bfloat16
I32 = jnp.int32

D_MODEL = 1024
SSD_HEADS = 8
SSD_HEAD_DIM = 64
SSD_GROUPS = 2
SSD_STATE = 128
SSD_CONV = 4
D_SSD = SSD_HEADS * SSD_HEAD_DIM
D_XBC = D_SSD + 2 * SSD_GROUPS * SSD_STATE
FOX_HEADS = 8
FOX_HEAD_DIM = 64
D_FOX = FOX_HEADS * FOX_HEAD_DIM
N_GROUPS = 4
EXPERTS_PER_GROUP = 4
N_EXPERTS = N_GROUPS * EXPERTS_PER_GROUP
N_PAIRS = 6
N_BUCKETS = N_GROUPS * N_PAIRS
D_EXPERT = 512
D_PLE = 256
EPS = 1e-6
LANES = 128
ROW_WORDS = D_MODEL + LANES
VMEM_LIMIT = 52 * 1024 * 1024

TM_PROJ = 512
L_SSD = 256
TQ_FOX = 256
TM_MOE = 256
TD_DISP = 1024
TM_COMB = 256

_NT = (((1,), (1,)), ((), ()))
_TN = (((0,), (0,)), ((), ()))


def _rms(x, g):
    ms = jnp.mean(x * x, axis=-1, keepdims=True)
    return x * lax.rsqrt(ms + EPS) * g


def _sigmoid(x):
    return 1.0 / (1.0 + jnp.exp(-x))


def _softplus(x):
    return jnp.maximum(x, 0.0) + jnp.log(1.0 + jnp.exp(-jnp.abs(x)))


def _split_bf16(x, parts):
    out = []
    r = x
    for _ in range(parts):
        h = r.astype(BF16)
        out.append(h)
        r = r - h.astype(F32)
    return out


def _dot(a, b):
    return jnp.dot(a, b, preferred_element_type=F32)


def _inproj_kernel(x_ref, g_ref, wbig_ref, wsm_ref, wsmt_ref,
                   z_ref, xbc_ref, q_ref, k_ref, v_ref, sm_ref, smt_ref):
    hn = _rms(x_ref[...], g_ref[...]).astype(BF16)
    z_ref[...] = _dot(hn, wbig_ref[:, 0:512]).astype(BF16)
    xbc_ref[...] = _dot(hn, wbig_ref[:, 512:1536]).astype(BF16)
    q_ref[...] = (_dot(hn, wbig_ref[:, 1536:2048]) * (FOX_HEAD_DIM ** -0.5)).astype(BF16)
    k_ref[...] = _dot(hn, wbig_ref[:, 2048:2560]).astype(BF16)
    v_ref[...] = _dot(hn, wbig_ref[:, 2560:3072]).astype(BF16)
    sm_ref[...] = _dot(hn, wsm_ref[...])
    smt_ref[...] = lax.dot_general(wsmt_ref[...], hn, _NT, preferred_element_type=F32)


def _in_proj(x2, g_mix, wbig, wsm, wsmt):
    T = x2.shape[0]
    tm = TM_PROJ
    row = lambda i: (i, 0)
    const = lambda i: (0, 0)
    return pl.pallas_call(
        _inproj_kernel,
        grid=(T // tm,),
        in_specs=[pl.BlockSpec((tm, D_MODEL), row),
                  pl.BlockSpec((1, D_MODEL), const),
                  pl.BlockSpec(wbig.shape, const),
                  pl.BlockSpec(wsm.shape, const),
                  pl.BlockSpec(wsmt.shape, const)],
        out_specs=[pl.BlockSpec((tm, D_SSD), row),
                   pl.BlockSpec((tm, D_XBC), row),
                   pl.BlockSpec((tm, D_FOX), row),
                   pl.BlockSpec((tm, D_FOX), row),
                   pl.BlockSpec((tm, D_FOX), row),
                   pl.BlockSpec((tm, LANES), row),
                   pl.BlockSpec((16, tm), lambda i: (0, i))],
        out_shape=[jax.ShapeDtypeStruct((T, D_SSD), BF16),
                   jax.ShapeDtypeStruct((T, D_XBC), BF16),
                   jax.ShapeDtypeStruct((T, D_FOX), BF16),
                   jax.ShapeDtypeStruct((T, D_FOX), BF16),
                   jax.ShapeDtypeStruct((T, D_FOX), BF16),
                   jax.ShapeDtypeStruct((T, LANES), F32),
                   jax.ShapeDtypeStruct((16, T), F32)],
        compiler_params=pltpu.CompilerParams(dimension_semantics=("arbitrary",),
                                             vmem_limit_bytes=VMEM_LIMIT),
        name="in_proj",
    )(x2, g_mix, wbig, wsm, wsmt)


def _ssd_kernel(xbc_ref, z_ref, sm_ref, smt_ref, convw_ref, convb_ref, colb_ref, cola_ref,
                rowb_ref, rowa_ref, dexp_ref, gssd_ref, expand_ref,
                y_ref, f_ref, ext_ref, st_ref, fcar_ref):
    L = xbc_ref.shape[0]
    R = SSD_HEADS // SSD_GROUPS
    c = pl.program_id(1)

    @pl.when(c == 0)
    def _():
        ext_ref[0:8, :] = jnp.zeros((8, D_XBC), F32)
        st_ref[...] = jnp.zeros(st_ref.shape, F32)
        fcar_ref[...] = jnp.zeros(fcar_ref.shape, F32)

    ext_ref[8:8 + L, :] = xbc_ref[...].astype(F32)
    conv = convb_ref[...] + convw_ref[0:1, :] * ext_ref[5:5 + L, :]
    for kk in range(1, SSD_CONV):
        conv = conv + convw_ref[kk:kk + 1, :] * ext_ref[5 + kk:5 + kk + L, :]
    ext_ref[0:8, :] = ext_ref[L:L + 8, :]
    xc = conv * _sigmoid(conv)
    xs = xc[:, 0:D_SSD]
    xs_bf = xs.astype(BF16)
    bm = xc[:, D_SSD:D_SSD + SSD_GROUPS * SSD_STATE].astype(BF16)
    cm = xc[:, D_SSD + SSD_GROUPS * SSD_STATE:].astype(BF16)

    ri = lax.broadcasted_iota(I32, (L, L), 0)
    ci = lax.broadcasted_iota(I32, (L, L), 1)
    causal = ri >= ci
    tril = causal.astype(BF16)
    triu = (ri <= ci).astype(BF16)

    raw_r = smt_ref[...] + rowb_ref[:, 0:1]
    sp_r = _softplus(raw_r)
    a_r = -jnp.exp(rowa_ref[:, 0:1])
    rsel = lax.broadcasted_iota(I32, raw_r.shape, 0) < SSD_HEADS
    stack = jnp.where(rsel, sp_r * a_r, -_softplus(-raw_r))
    cs16 = _dot(_split_bf16(stack, 3)[0], triu)
    for part in _split_bf16(stack, 3)[1:]:
        cs16 = cs16 + _dot(part, triu)
    f_out = cs16[8:16, :] + fcar_ref[:, 0:1]
    f_ref[...] = f_out
    fcar_ref[...] = jnp.broadcast_to(f_out[:, L - 1:L], fcar_ref.shape)
    dtr = sp_r[0:8, :]
    csr = cs16[0:8, :]

    dtc = _softplus(sm_ref[...] + colb_ref[...])
    adt_c = dtc * (-jnp.exp(cola_ref[...]))
    parts = _split_bf16(adt_c, 3)
    cs_c = _dot(tril, parts[0]) + _dot(tril, parts[1]) + _dot(tril, parts[2])
    cs_last = cs_c[L - 1:L, :]
    e1 = jnp.exp(cs_c)
    wst = dtc * jnp.exp(cs_last - cs_c)
    ex = expand_ref[...]
    p1 = _split_bf16(e1, 2)
    e1x = _dot(p1[0], ex) + _dot(p1[1], ex)
    p2 = _split_bf16(wst, 2)
    wstx = _dot(p2[0], ex) + _dot(p2[1], ex)

    xw = (xs * wstx).astype(BF16)
    dec8 = jnp.broadcast_to(jnp.exp(csr[:, L - 1:L]), (SSD_HEADS, SSD_STATE))

    ydiag = []
    yoff = []
    for g in range(SSD_GROUPS):
        bg = bm[:, g * SSD_STATE:(g + 1) * SSD_STATE]
        cg = cm[:, g * SSD_STATE:(g + 1) * SSD_STATE]
        gmat = lax.dot_general(cg, bg, _NT, preferred_element_type=F32)
        s_old = st_ref[g]
        yoff.append(lax.dot_general(cg, s_old.astype(BF16), _NT, preferred_element_type=F32))
        for j in range(R):
            h = g * R + j
            seg = cs_c[:, h:h + 1] - csr[h:h + 1, :]
            lm = jnp.exp(jnp.where(causal, seg, -jnp.inf))
            m = (gmat * lm * dtr[h:h + 1, :]).astype(BF16)
            ydiag.append(_dot(m, xs_bf[:, h * SSD_HEAD_DIM:(h + 1) * SSD_HEAD_DIM]))
        upd = lax.dot_general(xw[:, g * R * SSD_HEAD_DIM:(g + 1) * R * SSD_HEAD_DIM], bg, _TN,
                              preferred_element_type=F32)
        dec = jnp.concatenate(
            [jnp.broadcast_to(dec8[g * R + j:g * R + j + 1, :], (SSD_HEAD_DIM, SSD_STATE)) for j in range(R)],
            axis=0)
        st_ref[g] = dec * s_old + upd

    y = jnp.concatenate(ydiag, axis=1) + e1x * jnp.concatenate(yoff, axis=1) + dexp_ref[...] * xs
    zf = z_ref[...].astype(F32)
    y_ref[...] = _rms(y * (zf * _sigmoid(zf)), gssd_ref[...]).astype(BF16)


def _ssd(xbc, z, sm, smt, convw, convb, colb, cola, rowb, rowa, dexp, gssd, expand, B, S):
    L = L_SSD
    nc = S // L
    T = B * S
    row = lambda b, c: (b * nc + c, 0)
    const = lambda b, c: (0, 0)
    return pl.pallas_call(
        _ssd_kernel,
        grid=(B, nc),
        in_specs=[pl.BlockSpec((L, D_XBC), row),
                  pl.BlockSpec((L, D_SSD), row),
                  pl.BlockSpec((L, LANES), row),
                  pl.BlockSpec((16, L), lambda b, c: (0, b * nc + c)),
                  pl.BlockSpec(convw.shape, const),
                  pl.BlockSpec(convb.shape, const),
                  pl.BlockSpec(colb.shape, const),
                  pl.BlockSpec(cola.shape, const),
                  pl.BlockSpec(rowb.shape, const),
                  pl.BlockSpec(rowa.shape, const),
                  pl.BlockSpec(dexp.shape, const),
                  pl.BlockSpec(gssd.shape, const),
                  pl.BlockSpec(expand.shape, const)],
        out_specs=[pl.BlockSpec((L, D_SSD), row),
                   pl.BlockSpec((8, L), lambda b, c: (0, b * nc + c))],
        out_shape=[jax.ShapeDtypeStruct((T, D_SSD), BF16),
                   jax.ShapeDtypeStruct((8, T), F32)],
        scratch_shapes=[pltpu.VMEM((L + 8, D_XBC), F32),
                        pltpu.VMEM((SSD_GROUPS, (SSD_HEADS // SSD_GROUPS) * SSD_HEAD_DIM, SSD_STATE), F32),
                        pltpu.VMEM((8, LANES), F32)],
        compiler_params=pltpu.CompilerParams(dimension_semantics=("arbitrary", "arbitrary"),
                                             vmem_limit_bytes=VMEM_LIMIT),
        name="ssd",
    )(xbc, z, sm, smt, convw, convb, colb, cola, rowb, rowa, dexp, gssd, expand)


def _fox_kernel(q_ref, k_ref, v_ref, f_ref, o_ref):
    TQ = q_ref.shape[0]
    i = pl.program_id(2)
    q0 = pl.multiple_of(i * TQ, TQ)
    ri = lax.broadcasted_iota(I32, (TQ, TQ), 0)
    ci = lax.broadcasted_iota(I32, (TQ, TQ), 1)
    causal = ri >= ci
    outs = []
    for hh in range(2):
        sl = slice(hh * FOX_HEAD_DIM, (hh + 1) * FOX_HEAD_DIM)
        qh = q_ref[:, sl]
        fq0 = f_ref[hh:hh + 1, pl.ds(q0, LANES)][:, 0:1]

        def scores(j):
            ks = pl.multiple_of(j * TQ, TQ)
            kb = k_ref[pl.ds(ks, TQ), sl]
            vb = v_ref[pl.ds(ks, TQ), sl]
            s = lax.dot_general(qh, kb, _NT, preferred_element_type=F32)
            fj = f_ref[hh:hh + 1, pl.ds(ks, TQ)]
            return s + (fq0 - fj), vb

        def update(carry, s, vb):
            m, l, acc = carry
            mn = jnp.maximum(m, jnp.max(s, axis=-1, keepdims=True))
            a = jnp.exp(m - mn)
            p = jnp.exp(s - mn)
            l = a * l + jnp.sum(p, axis=-1, keepdims=True)
            acc = a * acc + _dot(p.astype(BF16), vb)
            return mn, l, acc

        def step(j, carry):
            s, vb = scores(j)
            return update(carry, s, vb)

        init = (jnp.full((TQ, 1), -1e30, F32), jnp.zeros((TQ, 1), F32), jnp.zeros((TQ, FOX_HEAD_DIM), F32))
        carry = lax.fori_loop(0, i, step, init)
        s, vb = scores(i)
        m, l, acc = update(carry, jnp.where(causal, s, -jnp.inf), vb)
        outs.append(acc / l)
    o_ref[...] = jnp.concatenate(outs, axis=1).astype(BF16)


def _fox(q, k, v, fcum, B, S):
    TQ = TQ_FOX
    nq = S // TQ
    T = B * S
    return pl.pallas_call(
        _fox_kernel,
        grid=(B, FOX_HEADS // 2, nq),
        in_specs=[pl.BlockSpec((TQ, LANES), lambda b, h, i: (b * nq + i, h)),
                  pl.BlockSpec((S, LANES), lambda b, h, i: (b, h)),
                  pl.BlockSpec((S, LANES), lambda b, h, i: (b, h)),
                  pl.BlockSpec((None, 2, S), lambda b, h, i: (h, 0, b))],
        out_specs=pl.BlockSpec((TQ, LANES), lambda b, h, i: (b * nq + i, h)),
        out_shape=jax.ShapeDtypeStruct((T, D_FOX), BF16),
        compiler_params=pltpu.CompilerParams(dimension_semantics=("arbitrary", "arbitrary", "arbitrary"),
                                             vmem_limit_bytes=VMEM_LIMIT),
        name="fox",
    )(q, k, v, fcum.reshape(FOX_HEADS // 2, 2, T))


def _outproj_kernel(x_ref, ys_ref, yf_ref, wo1_ref, wo2_ref, g_ref, wr_ref, br_ref,
                    h1_ref, xrow_ref, route_ref, cnt_ref, carry_ref):
    TM = x_ref.shape[0]
    i = pl.program_id(0)

    @pl.when(i == 0)
    def _():
        carry_ref[...] = jnp.zeros(carry_ref.shape, F32)

    h1 = x_ref[...] + _dot(ys_ref[...], wo1_ref[...]) + _dot(yf_ref[...], wo2_ref[...])
    h1_ref[...] = h1
    hn = _rms(h1, g_ref[...])
    logits = jnp.dot(hn, wr_ref[...], preferred_element_type=F32,
                     precision=lax.Precision.HIGHEST) + br_ref[...]
    lane = lax.broadcasted_iota(I32, (TM, LANES), 1)
    ninf = -jnp.inf
    gl = jnp.where(lane < N_GROUPS, logits, ninf)
    gmax = jnp.max(gl, axis=-1, keepdims=True)
    p_grp = 1.0 / jnp.sum(jnp.exp(gl - gmax), axis=-1, keepdims=True)
    g_idx = jnp.min(jnp.where(gl == gmax, lane, LANES), axis=-1, keepdims=True)
    e0 = N_GROUPS + EXPERTS_PER_GROUP * g_idx
    el = jnp.where((lane >= e0) & (lane < e0 + EXPERTS_PER_GROUP), logits, ninf)
    v1 = jnp.max(el, axis=-1, keepdims=True)
    i1 = jnp.min(jnp.where(el == v1, lane, LANES), axis=-1, keepdims=True)
    el2 = jnp.where(lane == i1, ninf, el)
    v2 = jnp.max(el2, axis=-1, keepdims=True)
    i2 = jnp.min(jnp.where(el2 == v2, lane, LANES), axis=-1, keepdims=True)
    e2 = jnp.exp(v2 - v1)
    w1 = p_grp / (1.0 + e2)
    w2 = p_grp * e2 / (1.0 + e2)
    l1 = i1 - e0
    l2 = i2 - e0
    first = l1 < l2
    a = jnp.where(first, l1, l2)
    b = jnp.where(first, l2, l1)
    wa = jnp.where(first, w1, w2)
    wb = jnp.where(first, w2, w1)
    pair = jnp.right_shift(a * (5 - a), 1) + b - 1
    bucket = g_idx * N_PAIRS + pair

    onehot = (lane == bucket).astype(BF16)
    ri = lax.broadcasted_iota(I32, (TM, TM), 0)
    ci = lax.broadcasted_iota(I32, (TM, TM), 1)
    incl = _dot((ri >= ci).astype(BF16), onehot)
    oh = onehot.astype(F32)
    rank = jnp.sum((incl - oh + carry_ref[0:1, :]) * oh, axis=-1, keepdims=True)
    carry_ref[...] = carry_ref[...] + jnp.broadcast_to(incl[TM - 1:TM, :], carry_ref.shape)
    cnt_ref[...] = carry_ref[...]

    route_ref[...] = jnp.where(lane == 0, wa, jnp.where(lane == 1, wb, jnp.where(
        lane == 2, bucket.astype(F32), jnp.where(lane == 3, rank, 0.0))))

    xrow_ref[:, 0:D_MODEL] = hn
    xrow_ref[:, D_MODEL:] = jnp.where(lane == 0, wa, jnp.where(lane == 1, wb, 0.0))


def _out_proj(x2, yssd, yfox, wo1, wo2, g_ffn, wr, br):
    T = x2.shape[0]
    tm = TM_PROJ
    row = lambda i: (i, 0)
    const = lambda i: (0, 0)
    return pl.pallas_call(
        _outproj_kernel,
        grid=(T // tm,),
        in_specs=[pl.BlockSpec((tm, D_MODEL), row),
                  pl.BlockSpec((tm, D_SSD), row),
                  pl.BlockSpec((tm, D_FOX), row),
                  pl.BlockSpec(wo1.shape, const),
                  pl.BlockSpec(wo2.shape, const),
                  pl.BlockSpec((1, D_MODEL), const),
                  pl.BlockSpec(wr.shape, const),
                  pl.BlockSpec((1, LANES), const)],
        out_specs=[pl.BlockSpec((tm, D_MODEL), row),
                   pl.BlockSpec((tm, ROW_WORDS), row),
                   pl.BlockSpec((tm, LANES), row),
                   pl.BlockSpec((8, LANES), const)],
        out_shape=[jax.ShapeDtypeStruct((T, D_MODEL), F32),
                   jax.ShapeDtypeStruct((T, ROW_WORDS), F32),
                   jax.ShapeDtypeStruct((T, LANES), F32),
                   jax.ShapeDtypeStruct((8, LANES), F32)],
        scratch_shapes=[pltpu.VMEM((8, LANES), F32)],
        compiler_params=pltpu.CompilerParams(dimension_semantics=("arbitrary",),
                                             vmem_limit_bytes=VMEM_LIMIT),
        name="out_proj_router",
    )(x2, yssd, yfox, wo1, wo2, g_ffn, wr, br)


def _row_copy(src_hbm, dst_hbm, src_row, dst_row, sem):
    return pltpu.make_async_copy(src_hbm.at[pl.ds(src_row, 1)], dst_hbm.at[pl.ds(dst_row, 1)], sem)


def _dispatch_kernel(pos_ref, src_hbm, dst_init_hbm, dst_hbm, sem):
    del dst_init_hbm
    TD = pos_ref.shape[-1]
    base = pl.program_id(0) * TD

    def start(r, c):
        _row_copy(src_hbm, dst_hbm, base + r, pos_ref[0, r], sem).start()
        return c

    lax.fori_loop(0, TD, start, 0)

    def wait(r, c):
        _row_copy(src_hbm, dst_hbm, base + r, pos_ref[0, r], sem).wait()
        return c

    lax.fori_loop(0, TD, wait, 0)


def _dispatch(pos, xrow, n_rows):
    T = xrow.shape[0]
    td = TD_DISP
    pos3 = pos.reshape(T // td, 1, td)
    dst0 = jnp.zeros((n_rows, ROW_WORDS), F32)
    return pl.pallas_call(
        _dispatch_kernel,
        grid=(T // td,),
        in_specs=[pl.BlockSpec((None, 1, td), lambda i: (i, 0, 0), memory_space=pltpu.SMEM),
                  pl.BlockSpec(memory_space=pl.ANY),
                  pl.BlockSpec(memory_space=pl.ANY)],
        out_specs=pl.BlockSpec(memory_space=pl.ANY),
        out_shape=jax.ShapeDtypeStruct((n_rows, ROW_WORDS), F32),
        scratch_shapes=[pltpu.SemaphoreType.DMA(())],
        input_output_aliases={2: 0},
        compiler_params=pltpu.CompilerParams(dimension_semantics=("arbitrary",)),
        name="dispatch",
    )(pos3, xrow, dst0)


def _experts_kernel(ea_ref, eb_ref, blk_ref, valid_ref,
                    xs_ref, wga_ref, wua_ref, wda_ref, wgb_ref, wub_ref, wdb_ref, ys_ref):
    del ea_ref, eb_ref, blk_ref
    i = pl.program_id(0)

    @pl.when(valid_ref[i] > 0)
    def _():
        x = xs_ref[:, 0:D_MODEL].astype(BF16)
        wa = xs_ref[:, D_MODEL:D_MODEL + 1]
        wb = xs_ref[:, D_MODEL + 1:D_MODEL + 2]

        def hidden(wg_ref, wu_ref, wt):
            g = _dot(x, wg_ref[...])
            u = _dot(x, wu_ref[...])
            return (g * _sigmoid(g) * u * wt).astype(BF16)

        ys_ref[...] = (_dot(hidden(wga_ref, wua_ref, wa), wda_ref[...])
                       + _dot(hidden(wgb_ref, wub_ref, wb), wdb_ref[...]))

    @pl.when(valid_ref[i] == 0)
    def _():
        ys_ref[...] = jnp.zeros(ys_ref.shape, F32)


def _experts(ea, eb, blk, valid, xs, wg, wu, wd):
    n_rows = xs.shape[0]
    tm = TM_MOE
    nt = n_rows // tm
    xmap = lambda i, ea, eb, blk, valid: (blk[i], 0)
    amap = lambda i, ea, eb, blk, valid: (ea[i], 0, 0)
    bmap = lambda i, ea, eb, blk, valid: (eb[i], 0, 0)
    up_spec = lambda m: pl.BlockSpec((None, D_MODEL, D_EXPERT), m)
    dn_spec = lambda m: pl.BlockSpec((None, D_EXPERT, D_MODEL), m)
    grid_spec = pltpu.PrefetchScalarGridSpec(
        num_scalar_prefetch=4,
        grid=(nt,),
        in_specs=[pl.BlockSpec((tm, ROW_WORDS), xmap),
                  up_spec(amap), up_spec(amap), dn_spec(amap),
                  up_spec(bmap), up_spec(bmap), dn_spec(bmap)],
        out_specs=pl.BlockSpec((tm, D_MODEL), lambda i, ea, eb, blk, valid: (i, 0)),
    )
    return pl.pallas_call(
        _experts_kernel,
        grid_spec=grid_spec,
        out_shape=jax.ShapeDtypeStruct((n_rows, D_MODEL), F32),
        compiler_params=pltpu.CompilerParams(dimension_semantics=("arbitrary",),
                                             vmem_limit_bytes=VMEM_LIMIT),
        name="experts",
    )(ea, eb, blk, valid, xs, wg, wu, wd, wg, wu, wd)


def _combine_kernel(pos_cur_ref, pos_nxt_ref, h1_ref, p_ref, wpp_ref, wpg_ref, gple_ref, gfin_ref, ys_hbm,
                    o_ref, buf_ref, sem):
    TM = h1_ref.shape[0]
    i = pl.program_id(0)
    n = pl.num_programs(0)
    slot = i % 2

    def gather(pos_ref, s):
        def start(r, c):
            pltpu.make_async_copy(ys_hbm.at[pl.ds(pos_ref[0, r], 1)], buf_ref.at[s, pl.ds(r, 1)],
                                  sem.at[s]).start()
            return c
        lax.fori_loop(0, TM, start, 0)

    @pl.when(i == 0)
    def _():
        gather(pos_cur_ref, 0)

    @pl.when(i + 1 < n)
    def _():
        gather(pos_nxt_ref, 1 - slot)

    def wait(r, c):
        pltpu.make_async_copy(ys_hbm.at[pl.ds(pos_cur_ref[0, r], 1)], buf_ref.at[slot, pl.ds(r, 1)],
                              sem.at[slot]).wait()
        return c

    lax.fori_loop(0, TM, wait, 0)

    h2 = h1_ref[...] + buf_ref[slot]
    n3 = _rms(h2, gple_ref[...]).astype(BF16)
    gate = _sigmoid(_dot(n3, wpg_ref[...]))
    pp = _dot(p_ref[...].astype(BF16), wpp_ref[...])
    o_ref[...] = _rms(h2 + pp * gate, gfin_ref[...])


def _combine(pos, h1, p2, wpp, wpg, g_ple, g_final, ys):
    T = h1.shape[0]
    tm = TM_COMB
    nt = T // tm
    pos3 = pos.reshape(nt, 1, tm)
    row = lambda i: (i, 0)
    const = lambda i: (0, 0)
    return pl.pallas_call(
        _combine_kernel,
        grid=(nt,),
        in_specs=[pl.BlockSpec((None, 1, tm), lambda i: (i, 0, 0), memory_space=pltpu.SMEM),
                  pl.BlockSpec((None, 1, tm), lambda i: (jnp.minimum(i + 1, nt - 1), 0, 0),
                               memory_space=pltpu.SMEM),
                  pl.BlockSpec((tm, D_MODEL), row),
                  pl.BlockSpec((tm, D_PLE), row),
                  pl.BlockSpec(wpp.shape, const),
                  pl.BlockSpec(wpg.shape, const),
                  pl.BlockSpec((1, D_MODEL), const),
                  pl.BlockSpec((1, D_MODEL), const),
                  pl.BlockSpec(memory_space=pl.ANY)],
        out_specs=pl.BlockSpec((tm, D_MODEL), row),
        out_shape=jax.ShapeDtypeStruct((T, D_MODEL), F32),
        scratch_shapes=[pltpu.VMEM((2, tm, D_MODEL), F32),
                        pltpu.SemaphoreType.DMA((2,))],
        compiler_params=pltpu.CompilerParams(dimension_semantics=("arbitrary",),
                                             vmem_limit_bytes=VMEM_LIMIT),
        name="combine_ple",
    )(pos3, pos3, h1, p2, wpp, wpg, g_ple, g_final, ys)


def _tile_tables(counts, n_tiles):
    nt = (counts + TM_MOE - 1) // TM_MOE
    cum = jnp.cumsum(nt)
    off = (cum - nt) * TM_MOE
    total = cum[-1]
    tile = jnp.arange(n_tiles, dtype=I32)
    valid = (tile < total).astype(I32)
    blk = jnp.minimum(tile, total - 1)
    tb = jnp.minimum(jnp.sum((cum[None, :] <= blk[:, None]).astype(I32), axis=1), N_BUCKETS - 1)
    pair_a = jnp.array([0, 0, 0, 1, 1, 2], I32)
    pair_b = jnp.array([1, 2, 3, 2, 3, 3], I32)
    grp = tb // N_PAIRS
    ea = grp * EXPERTS_PER_GROUP + pair_a[tb % N_PAIRS]
    eb = grp * EXPERTS_PER_GROUP + pair_b[tb % N_PAIRS]
    return off, ea, eb, blk, valid


def kernel(x, p, w_in, conv_w, conv_b, dt_bias, a_log, d_skip, g_ssd, fox_fbias, w_out, g_mix, g_ffn,
           w_route_group, b_route_group, w_route_expert, b_route_expert, w_exp_gate, w_exp_up, w_exp_down,
           g_ple, w_ple_proj, w_ple_gate, g_final):
    B, S, _ = x.shape
    T = B * S
    assert S % L_SSD == 0 and S % TQ_FOX == 0 and T % TD_DISP == 0 and T % TM_PROJ == 0
    x2 = x.reshape(T, D_MODEL)
    p2 = p[0].reshape(T, D_PLE)

    wi = w_in[0]
    o_xbc = D_SSD
    o_dt = o_xbc + D_XBC
    o_q = o_dt + SSD_HEADS
    o_f = o_q + 3 * D_FOX
    wbig = jnp.concatenate([wi[:, 0:o_dt], wi[:, o_q:o_f]], axis=1).astype(BF16)
    wsm_cols = jnp.concatenate([wi[:, o_dt:o_q], wi[:, o_f:o_f + FOX_HEADS]], axis=1)
    wsm = jnp.pad(wsm_cols, ((0, 0), (0, LANES - 16))).astype(BF16)
    wsmt = wsm_cols.T.astype(BF16)
    lane_pad = lambda v: jnp.pad(v.reshape(1, -1), ((0, 0), (0, LANES - v.shape[-1])))
    colb = lane_pad(jnp.concatenate([dt_bias[0], fox_fbias[0]]))
    cola = lane_pad(a_log[0])
    rowb = jnp.broadcast_to(jnp.concatenate([dt_bias[0], fox_fbias[0]])[:, None], (16, LANES))
    rowa = jnp.broadcast_to(jnp.pad(a_log[0], (0, 8))[:, None], (16, LANES))
    dexp = jnp.repeat(d_skip[0], SSD_HEAD_DIM).reshape(1, D_SSD)
    expand = (jnp.arange(LANES)[:, None] == (jnp.arange(D_SSD) // SSD_HEAD_DIM)[None, :]).astype(BF16)
    wo = w_out[0].astype(BF16)
    wr = jnp.pad(jnp.concatenate([w_route_group[0], w_route_expert[0]], axis=1),
                 ((0, 0), (0, LANES - N_GROUPS - N_EXPERTS)))
    br = lane_pad(jnp.concatenate([b_route_group[0], b_route_expert[0]]))
    wg = w_exp_gate[0].reshape(N_EXPERTS, D_MODEL, D_EXPERT).astype(BF16)
    wu = w_exp_up[0].reshape(N_EXPERTS, D_MODEL, D_EXPERT).astype(BF16)
    wd = w_exp_down[0].reshape(N_EXPERTS, D_EXPERT, D_MODEL).astype(BF16)

    z, xbc, q, k, v, sm, smt = _in_proj(x2, g_mix[0].reshape(1, -1), wbig, wsm, wsmt)
    yssd, fcum = _ssd(xbc, z, sm, smt, conv_w[0].reshape(SSD_CONV, D_XBC), conv_b[0].reshape(1, -1),
                      colb, cola, rowb, rowa, dexp, g_ssd[0].reshape(1, -1), expand, B, S)
    yfox = _fox(q, k, v, fcum, B, S)
    h1, xrow, route, cnt = _out_proj(x2, yssd, yfox, wo[0:D_SSD], wo[D_SSD:], g_ffn[0].reshape(1, -1), wr, br)

    n_rows = T + N_BUCKETS * TM_MOE
    bucket = route[:, 2].astype(I32)
    rank = route[:, 3].astype(I32)
    off, ea, eb, blk, valid = _tile_tables(cnt[0, 0:N_BUCKETS].astype(I32), n_rows // TM_MOE)
    pos = off[bucket] + rank

    xs = _dispatch(pos, xrow, n_rows)
    ys = _experts(ea, eb, blk, valid, xs, wg, wu, wd)
    out = _combine(pos, h1, p2, w_ple_proj[0].astype(BF16), w_ple_gate[0].astype(BF16),
                   g_ple[0].reshape(1, -1), g_final.reshape(1, -1), ys)
    return out.reshape(B, S, D_MODEL)
```

```python
import functools

import jax
import jax.numpy as jnp
from jax import lax
from jax.experimental import pallas as pl
from jax.experimental.pallas import tpu as pltpu

F32 = jnp.float32
BF16 = jnp.bfloat16
I32 = jnp.int32

D_MODEL = 1024
SSD_HEADS = 8
SSD_HEAD_DIM = 64
SSD_GROUPS = 2
SSD_STATE = 128
SSD_CONV = 4
D_SSD = SSD_HEADS * SSD_HEAD_DIM
D_XBC = D_SSD + 2 * SSD_GROUPS * SSD_STATE
FOX_HEADS = 8
FOX_HEAD_DIM = 64
D_FOX = FOX_HEADS * FOX_HEAD_DIM
N_GROUPS = 4
EXPERTS_PER_GROUP = 4
N_EXPERTS = N_GROUPS * EXPERTS_PER_GROUP
N_PAIRS = 6
N_BUCKETS = N_GROUPS * N_PAIRS
D_EXPERT = 512
D_PLE = 256
EPS = 1e-6
LOG2E = 1.4426950408889634
LANES = 128
SUBLANES = 8
TOK_ROWS = D_MODEL // LANES
D_QK = FOX_HEADS * LANES
F_PARTS = 3
VMEM_LIMIT = 52 * 1024 * 1024

TM_PROJ = 512
L_SSD = 256
TQ_FOX = 256
FOX_HEADS_PER_LOOP = 8
TM_MOE = 256
TD_DISP = 1024
TM_COMB = 256

_NT = (((1,), (1,)), ((), ()))
_TN = (((0,), (0,)), ((), ()))


def _rms(x, g):
    ms = jnp.mean(x * x, axis=-1, keepdims=True)
    return x * lax.rsqrt(ms + EPS) * g


def _sigmoid(x):
    return 1.0 / (1.0 + jnp.exp(-x))


def _softplus(x):
    return jnp.maximum(x, 0.0) + jnp.log(1.0 + jnp.exp(-jnp.abs(x)))


def _split_bf16(x, parts):
    out = []
    r = x
    for _ in range(parts):
        h = r.astype(BF16)
        out.append(h)
        r = r - h.astype(F32)
    return out


def _dot(a, b):
    return jnp.dot(a, b, preferred_element_type=F32)


def _dot_exact(a01, x, parts):
    acc = None
    for piece in _split_bf16(x, parts):
        t = _dot(a01, piece)
        acc = t if acc is None else acc + t
    return acc


def _to_token_tiles(ref, x, stride, offset=0):
    m = x.shape[0]
    for s in range(TOK_ROWS):
        ref[pl.ds(offset + s, m, stride=stride), :] = x[:, s * LANES:(s + 1) * LANES]


def _from_token_tiles(ref, m, stride, offset=0):
    return jnp.concatenate([ref[pl.ds(offset + s, m, stride=stride), :] for s in range(TOK_ROWS)], axis=1)


def _inproj_kernel(x_ref, g_ref, wz_ref, wxbc_ref, wq_ref, wk_ref, wvt_ref, wsm_ref, wsmt_ref,
                   colb_ref, qones_ref, place_ref,
                   z_ref, xbc_ref, q_ref, k_ref, vt_ref, sm_ref, smt_ref, fcar_ref, *, tiles_per_seq):
    TM = x_ref.shape[0]

    @pl.when(pl.program_id(0) % tiles_per_seq == 0)
    def _():
        fcar_ref[...] = jnp.zeros(fcar_ref.shape, F32)

    hn = _rms(x_ref[...], g_ref[...]).astype(BF16)
    z_ref[...] = _dot(hn, wz_ref[...]).astype(BF16)
    xbc_ref[...] = _dot(hn, wxbc_ref[...]).astype(BF16)
    q_ref[...] = (_dot(hn, wq_ref[...]) * (FOX_HEAD_DIM ** -0.5 * LOG2E) + qones_ref[...]).astype(BF16)
    vt_ref[...] = lax.dot_general(wvt_ref[...], hn, _NT, preferred_element_type=F32).astype(BF16)
    sm = _dot(hn, wsm_ref[...])
    sm_ref[...] = sm
    smt_ref[...] = lax.dot_general(wsmt_ref[...], hn, _NT, preferred_element_type=F32)

    raw = sm + colb_ref[...]
    logf = -_softplus(-raw)
    ri = lax.broadcasted_iota(I32, (TM, TM), 0)
    ci = lax.broadcasted_iota(I32, (TM, TM), 1)
    fcum = _dot_exact((ri >= ci).astype(BF16), logf, 3) + fcar_ref[0:1, :]
    fcar_ref[...] = jnp.broadcast_to(fcum[TM - 1:TM, :], fcar_ref.shape)
    fs = fcum * (-LOG2E)
    hi = fs.astype(BF16)
    r1 = fs - hi.astype(F32)
    mid = r1.astype(BF16)
    lo = (r1 - mid.astype(F32)).astype(BF16)
    lane = lax.broadcasted_iota(I32, (TM, LANES), 1)
    pieces = jnp.where(lane < 16, hi, jnp.where(lane < 24, mid, lo))
    k_ref[...] = (_dot(hn, wk_ref[...]) + _dot(pieces, place_ref[...])).astype(BF16)


def _in_proj(x2, g_mix, wz, wxbc, wq, wk, wvt, wsm, wsmt, colb, qones, place, S):
    T = x2.shape[0]
    tm = TM_PROJ
    row = lambda i: (i, 0)
    const = lambda i: (0, 0)
    full = lambda a: pl.BlockSpec(a.shape, const)
    return pl.pallas_call(
        functools.partial(_inproj_kernel, tiles_per_seq=S // tm),
        grid=(T // tm,),
        in_specs=[pl.BlockSpec((tm, D_MODEL), row), full(g_mix), full(wz), full(wxbc), full(wq), full(wk),
                  full(wvt), full(wsm), full(wsmt), full(colb), full(qones), full(place)],
        out_specs=[pl.BlockSpec((tm, D_SSD), row),
                   pl.BlockSpec((tm, D_XBC), row),
                   pl.BlockSpec((tm, D_QK), row),
                   pl.BlockSpec((tm, D_QK), row),
                   pl.BlockSpec((D_FOX, tm), lambda i: (0, i)),
                   pl.BlockSpec((tm, LANES), row),
                   pl.BlockSpec((SSD_HEADS, tm), lambda i: (0, i))],
        out_shape=[jax.ShapeDtypeStruct((T, D_SSD), BF16),
                   jax.ShapeDtypeStruct((T, D_XBC), BF16),
                   jax.ShapeDtypeStruct((T, D_QK), BF16),
                   jax.ShapeDtypeStruct((T, D_QK), BF16),
                   jax.ShapeDtypeStruct((D_FOX, T), BF16),
                   jax.ShapeDtypeStruct((T, LANES), F32),
                   jax.ShapeDtypeStruct((SSD_HEADS, T), F32)],
        scratch_shapes=[pltpu.VMEM((SUBLANES, LANES), F32)],
        compiler_params=pltpu.CompilerParams(dimension_semantics=("arbitrary",),
                                             vmem_limit_bytes=VMEM_LIMIT),
        name="in_proj",
    )(x2, g_mix, wz, wxbc, wq, wk, wvt, wsm, wsmt, colb, qones, place)


def _ssd_kernel(xbc_ref, z_ref, sm_ref, smt_ref, convw_ref, convb_ref, colb_ref, cola_ref,
                rowb_ref, rowa_ref, dexp_ref, gssd_ref, expand_ref,
                y_ref, ext_ref, st_ref):
    L = xbc_ref.shape[0]
    R = SSD_HEADS // SSD_GROUPS
    c = pl.program_id(1)

    @pl.when(c == 0)
    def _():
        ext_ref[0:8, :] = jnp.zeros((8, D_XBC), F32)
        st_ref[...] = jnp.zeros(st_ref.shape, F32)

    ext_ref[8:8 + L, :] = xbc_ref[...].astype(F32)
    conv = convb_ref[...] + convw_ref[0:1, :] * ext_ref[5:5 + L, :]
    for kk in range(1, SSD_CONV):
        conv = conv + convw_ref[kk:kk + 1, :] * ext_ref[5 + kk:5 + kk + L, :]
    ext_ref[0:8, :] = ext_ref[L:L + 8, :]
    xc = conv * _sigmoid(conv)
    xs = xc[:, 0:D_SSD]
    xs_bf = xs.astype(BF16)
    bm = xc[:, D_SSD:D_SSD + SSD_GROUPS * SSD_STATE].astype(BF16)
    cm = xc[:, D_SSD + SSD_GROUPS * SSD_STATE:].astype(BF16)

    ri = lax.broadcasted_iota(I32, (L, L), 0)
    ci = lax.broadcasted_iota(I32, (L, L), 1)
    causal = ri >= ci
    tril = causal.astype(BF16)
    triu = (ri <= ci).astype(BF16)

    dtr = _softplus(smt_ref[...] + rowb_ref[:, 0:1])
    adt_r = dtr * (-jnp.exp(rowa_ref[:, 0:1]))
    csr = None
    for part in _split_bf16(adt_r, 3):
        t = _dot(part, triu)
        csr = t if csr is None else csr + t

    dtc = _softplus(sm_ref[...] + colb_ref[...])
    adt_c = dtc * (-jnp.exp(cola_ref[...]))
    cs_c = _dot_exact(tril, adt_c, 3)
    cs_last = cs_c[L - 1:L, :]
    e1 = jnp.exp(cs_c)
    wst = dtc * jnp.exp(cs_last - cs_c)
    ex = expand_ref[...]
    p1 = _split_bf16(e1, 2)
    e1x = _dot(p1[0], ex) + _dot(p1[1], ex)
    p2 = _split_bf16(wst, 2)
    wstx = _dot(p2[0], ex) + _dot(p2[1], ex)

    xw = (xs * wstx).astype(BF16)
    dec8 = jnp.broadcast_to(jnp.exp(csr[:, L - 1:L]), (SSD_HEADS, SSD_STATE))

    ydiag = []
    yoff = []
    for g in range(SSD_GROUPS):
        bg = bm[:, g * SSD_STATE:(g + 1) * SSD_STATE]
        cg = cm[:, g * SSD_STATE:(g + 1) * SSD_STATE]
        gmat = lax.dot_general(cg, bg, _NT, preferred_element_type=F32)
        s_old = st_ref[g]
        yoff.append(lax.dot_general(cg, s_old.astype(BF16), _NT, preferred_element_type=F32))
        for j in range(R):
            h = g * R + j
            seg = cs_c[:, h:h + 1] - csr[h:h + 1, :]
            lm = jnp.exp(jnp.where(causal, seg, -jnp.inf))
            m = (gmat * lm * dtr[h:h + 1, :]).astype(BF16)
            ydiag.append(_dot(m, xs_bf[:, h * SSD_HEAD_DIM:(h + 1) * SSD_HEAD_DIM]))
        upd = lax.dot_general(xw[:, g * R * SSD_HEAD_DIM:(g + 1) * R * SSD_HEAD_DIM], bg, _TN,
                              preferred_element_type=F32)
        dec = jnp.concatenate(
            [jnp.broadcast_to(dec8[g * R + j:g * R + j + 1, :], (SSD_HEAD_DIM, SSD_STATE)) for j in range(R)],
            axis=0)
        st_ref[g] = dec * s_old + upd

    y = jnp.concatenate(ydiag, axis=1) + e1x * jnp.concatenate(yoff, axis=1) + dexp_ref[...] * xs
    zf = z_ref[...].astype(F32)
    y_ref[...] = _rms(y * (zf * _sigmoid(zf)), gssd_ref[...]).astype(BF16)


def _ssd(xbc, z, sm, smt, convw, convb, colb, cola, rowb, rowa, dexp, gssd, expand, B, S):
    L = L_SSD
    nc = S // L
    T = B * S
    row = lambda b, c: (b * nc + c, 0)
    const = lambda b, c: (0, 0)
    full = lambda a: pl.BlockSpec(a.shape, const)
    return pl.pallas_call(
        _ssd_kernel,
        grid=(B, nc),
        in_specs=[pl.BlockSpec((L, D_XBC), row),
                  pl.BlockSpec((L, D_SSD), row),
                  pl.BlockSpec((L, LANES), row),
                  pl.BlockSpec((SSD_HEADS, L), lambda b, c: (0, b * nc + c)),
                  full(convw), full(convb), full(colb), full(cola), full(rowb), full(rowa),
                  full(dexp), full(gssd), full(expand)],
        out_specs=pl.BlockSpec((L, D_SSD), row),
        out_shape=jax.ShapeDtypeStruct((T, D_SSD), BF16),
        scratch_shapes=[pltpu.VMEM((L + 8, D_XBC), F32),
                        pltpu.VMEM((SSD_GROUPS, (SSD_HEADS // SSD_GROUPS) * SSD_HEAD_DIM, SSD_STATE), F32)],
        compiler_params=pltpu.CompilerParams(dimension_semantics=("arbitrary", "arbitrary"),
                                             vmem_limit_bytes=VMEM_LIMIT),
        name="ssd",
    )(xbc, z, sm, smt, convw, convb, colb, cola, rowb, rowa, dexp, gssd, expand)


def _fox_kernel(q_ref, k_ref, vt_ref, o_ref):
    TQ = q_ref.shape[0]
    i = pl.program_id(1)
    key_idx = lax.broadcasted_iota(I32, (TQ, TQ), 0)
    qry_idx = lax.broadcasted_iota(I32, (TQ, TQ), 1)
    causal = key_idx <= qry_idx
    outs = []
    for g in range(FOX_HEADS // FOX_HEADS_PER_LOOP):
        heads = range(g * FOX_HEADS_PER_LOOP, (g + 1) * FOX_HEADS_PER_LOOP)
        qs = [q_ref[:, h * LANES:(h + 1) * LANES] for h in heads]

        def block(j, carry, masked):
            ks = pl.multiple_of(j * TQ, TQ)
            scores = [lax.dot_general(k_ref[pl.ds(ks, TQ), h * LANES:(h + 1) * LANES], qs[n], _NT,
                                      preferred_element_type=F32) for n, h in enumerate(heads)]
            new = []
            for n, h in enumerate(heads):
                m, l, acc = carry[n]
                s = scores[n]
                if masked:
                    s = jnp.where(causal, s, -jnp.inf)
                mn = jnp.maximum(m, jnp.max(s, axis=0, keepdims=True))
                a = jnp.exp2(m - mn)
                p = jnp.exp2(s - mn)
                l = a * l + jnp.sum(p, axis=0, keepdims=True)
                vt = vt_ref[h * FOX_HEAD_DIM:(h + 1) * FOX_HEAD_DIM, pl.ds(ks, TQ)]
                acc = a * acc + _dot(vt, p.astype(BF16))
                new.append((mn, l, acc))
            return tuple(new)

        init = tuple((jnp.full((1, TQ), -1e30, F32), jnp.zeros((1, TQ), F32),
                      jnp.zeros((FOX_HEAD_DIM, TQ), F32)) for _ in heads)
        carry = lax.fori_loop(0, i, lambda j, c: block(j, c, False), init)
        for m, l, acc in block(i, carry, True):
            outs.append(acc / l)
    o_ref[...] = jnp.concatenate(outs, axis=0).astype(BF16)


def _fox(q, k, vt, B, S):
    TQ = TQ_FOX
    nq = S // TQ
    T = B * S
    return pl.pallas_call(
        _fox_kernel,
        grid=(B, nq),
        in_specs=[pl.BlockSpec((TQ, D_QK), lambda b, i: (b * nq + i, 0)),
                  pl.BlockSpec((S, D_QK), lambda b, i: (b, 0)),
                  pl.BlockSpec((D_FOX, S), lambda b, i: (0, b))],
        out_specs=pl.BlockSpec((D_FOX, TQ), lambda b, i: (0, b * nq + i)),
        out_shape=jax.ShapeDtypeStruct((D_FOX, T), BF16),
        compiler_params=pltpu.CompilerParams(dimension_semantics=("arbitrary", "arbitrary"),
                                             vmem_limit_bytes=VMEM_LIMIT),
        name="fox",
    )(q, k, vt)


def _outproj_kernel(x_ref, ys_ref, yft_ref, wo1_ref, wo2_ref, g_ref, wr_ref, br_ref,
                    h1_ref, xrow_ref, route_ref, cnt_ref, carry_ref):
    TM = x_ref.shape[0]
    i = pl.program_id(0)

    @pl.when(i == 0)
    def _():
        carry_ref[...] = jnp.zeros(carry_ref.shape, F32)

    h1 = (x_ref[...] + _dot(ys_ref[...], wo1_ref[...])
          + lax.dot_general(yft_ref[...], wo2_ref[...], _TN, preferred_element_type=F32))
    h1_ref[...] = h1
    hn = _rms(h1, g_ref[...])
    logits = jnp.dot(hn, wr_ref[...], preferred_element_type=F32,
                     precision=lax.Precision.HIGHEST) + br_ref[...]
    lane = lax.broadcasted_iota(I32, (TM, LANES), 1)
    ninf = -jnp.inf
    gl = jnp.where(lane < N_GROUPS, logits, ninf)
    gmax = jnp.max(gl, axis=-1, keepdims=True)
    p_grp = 1.0 / jnp.sum(jnp.exp(gl - gmax), axis=-1, keepdims=True)
    g_idx = jnp.min(jnp.where(gl == gmax, lane, LANES), axis=-1, keepdims=True)
    e0 = N_GROUPS + EXPERTS_PER_GROUP * g_idx
    el = jnp.where((lane >= e0) & (lane < e0 + EXPERTS_PER_GROUP), logits, ninf)
    v1 = jnp.max(el, axis=-1, keepdims=True)
    i1 = jnp.min(jnp.where(el == v1, lane, LANES), axis=-1, keepdims=True)
    el2 = jnp.where(lane == i1, ninf, el)
    v2 = jnp.max(el2, axis=-1, keepdims=True)
    i2 = jnp.min(jnp.where(el2 == v2, lane, LANES), axis=-1, keepdims=True)
    e2 = jnp.exp(v2 - v1)
    w1 = p_grp / (1.0 + e2)
    w2 = p_grp * e2 / (1.0 + e2)
    l1 = i1 - e0
    l2 = i2 - e0
    first = l1 < l2
    a = jnp.where(first, l1, l2)
    b = jnp.where(first, l2, l1)
    wa = jnp.where(first, w1, w2)
    wb = jnp.where(first, w2, w1)
    pair = jnp.right_shift(a * (5 - a), 1) + b - 1
    bucket = g_idx * N_PAIRS + pair

    onehot = (lane == bucket).astype(BF16)
    ri = lax.broadcasted_iota(I32, (TM, TM), 0)
    ci = lax.broadcasted_iota(I32, (TM, TM), 1)
    incl = _dot((ri >= ci).astype(BF16), onehot)
    oh = onehot.astype(F32)
    rank = jnp.sum((incl - oh + carry_ref[0:1, :]) * oh, axis=-1, keepdims=True)
    carry_ref[...] = carry_ref[...] + jnp.broadcast_to(incl[TM - 1:TM, :], carry_ref.shape)
    cnt_ref[...] = carry_ref[...]

    route_ref[...] = jnp.where(lane == 0, wa, jnp.where(lane == 1, wb, jnp.where(
        lane == 2, bucket.astype(F32), jnp.where(lane == 3, rank, 0.0))))
    _to_token_tiles(xrow_ref, hn, TOK_ROWS)


def _out_proj(x2, yssd, yfoxt, wo1, wo2, g_ffn, wr, br):
    T = x2.shape[0]
    tm = TM_PROJ
    row = lambda i: (i, 0)
    const = lambda i: (0, 0)
    return pl.pallas_call(
        _outproj_kernel,
        grid=(T // tm,),
        in_specs=[pl.BlockSpec((tm, D_MODEL), row),
                  pl.BlockSpec((tm, D_SSD), row),
                  pl.BlockSpec((D_FOX, tm), lambda i: (0, i)),
                  pl.BlockSpec(wo1.shape, const),
                  pl.BlockSpec(wo2.shape, const),
                  pl.BlockSpec((1, D_MODEL), const),
                  pl.BlockSpec(wr.shape, const),
                  pl.BlockSpec((1, LANES), const)],
        out_specs=[pl.BlockSpec((tm, D_MODEL), row),
                   pl.BlockSpec((tm * TOK_ROWS, LANES), row),
                   pl.BlockSpec((tm, LANES), row),
                   pl.BlockSpec((SUBLANES, LANES), const)],
        out_shape=[jax.ShapeDtypeStruct((T, D_MODEL), F32),
                   jax.ShapeDtypeStruct((T * TOK_ROWS, LANES), F32),
                   jax.ShapeDtypeStruct((T, LANES), F32),
                   jax.ShapeDtypeStruct((SUBLANES, LANES), F32)],
        scratch_shapes=[pltpu.VMEM((SUBLANES, LANES), F32)],
        compiler_params=pltpu.CompilerParams(dimension_semantics=("arbitrary",),
                                             vmem_limit_bytes=VMEM_LIMIT),
        name="out_proj_router",
    )(x2, yssd, yfoxt, wo1, wo2, g_ffn, wr, br)


def _dispatch_kernel(pos_ref, src_ref, dst_init_hbm, dst_hbm, sem):
    del dst_init_hbm
    TD = pos_ref.shape[-1]

    def copy(r):
        dst = pl.multiple_of(pos_ref[0, r] * TOK_ROWS, TOK_ROWS)
        src = pl.multiple_of(r * TOK_ROWS, TOK_ROWS)
        return pltpu.make_async_copy(src_ref.at[pl.ds(src, TOK_ROWS)], dst_hbm.at[pl.ds(dst, TOK_ROWS)], sem)

    def start(r, c):
        copy(r).start()
        return c

    lax.fori_loop(0, TD, start, 0)

    def wait(r, c):
        copy(r).wait()
        return c

    lax.fori_loop(0, TD, wait, 0)


def _dispatch(pos, xrow, n_rows):
    T = pos.shape[0]
    td = TD_DISP
    pos3 = pos.reshape(T // td, 1, td)
    dst0 = jnp.zeros((n_rows * TOK_ROWS, LANES), F32)
    return pl.pallas_call(
        _dispatch_kernel,
        grid=(T // td,),
        in_specs=[pl.BlockSpec((None, 1, td), lambda i: (i, 0, 0), memory_space=pltpu.SMEM),
                  pl.BlockSpec((td * TOK_ROWS, LANES), lambda i: (i, 0)),
                  pl.BlockSpec(memory_space=pl.ANY)],
        out_specs=pl.BlockSpec(memory_space=pl.ANY),
        out_shape=jax.ShapeDtypeStruct((n_rows * TOK_ROWS, LANES), F32),
        scratch_shapes=[pltpu.SemaphoreType.DMA(())],
        input_output_aliases={2: 0},
        compiler_params=pltpu.CompilerParams(dimension_semantics=("arbitrary",)),
        name="dispatch",
    )(pos3, xrow, dst0)


def _experts_kernel(ea_ref, eb_ref, blk_ref, valid_ref,
                    xs_ref, wga_ref, wua_ref, wda_ref, wgb_ref, wub_ref, wdb_ref, ys_ref):
    del ea_ref, eb_ref, blk_ref
    i = pl.program_id(0)
    TM = xs_ref.shape[0] // TOK_ROWS

    @pl.when(valid_ref[i] > 0)
    def _():
        x = _from_token_tiles(xs_ref, TM, TOK_ROWS).astype(BF16)

        def expert(wg_ref, wu_ref, wd_ref):
            g = _dot(x, wg_ref[...])
            u = _dot(x, wu_ref[...])
            return _dot((g * _sigmoid(g) * u).astype(BF16), wd_ref[...])

        _to_token_tiles(ys_ref, expert(wga_ref, wua_ref, wda_ref), 2 * TOK_ROWS)
        _to_token_tiles(ys_ref, expert(wgb_ref, wub_ref, wdb_ref), 2 * TOK_ROWS, TOK_ROWS)

    @pl.when(valid_ref[i] == 0)
    def _():
        ys_ref[...] = jnp.zeros(ys_ref.shape, F32)


def _experts(ea, eb, blk, valid, xs, wg, wu, wd):
    tm = TM_MOE
    n_rows = xs.shape[0] // TOK_ROWS
    nt = n_rows // tm
    xmap = lambda i, ea, eb, blk, valid: (blk[i], 0)
    amap = lambda i, ea, eb, blk, valid: (ea[i], 0, 0)
    bmap = lambda i, ea, eb, blk, valid: (eb[i], 0, 0)
    up_spec = lambda m: pl.BlockSpec((None, D_MODEL, D_EXPERT), m)
    dn_spec = lambda m: pl.BlockSpec((None, D_EXPERT, D_MODEL), m)
    grid_spec = pltpu.PrefetchScalarGridSpec(
        num_scalar_prefetch=4,
        grid=(nt,),
        in_specs=[pl.BlockSpec((tm * TOK_ROWS, LANES), xmap),
                  up_spec(amap), up_spec(amap), dn_spec(amap),
                  up_spec(bmap), up_spec(bmap), dn_spec(bmap)],
        out_specs=pl.BlockSpec((tm * 2 * TOK_ROWS, LANES), lambda i, ea, eb, blk, valid: (i, 0)),
    )
    return pl.pallas_call(
        _experts_kernel,
        grid_spec=grid_spec,
        out_shape=jax.ShapeDtypeStruct((n_rows * 2 * TOK_ROWS, LANES), F32),
        compiler_params=pltpu.CompilerParams(dimension_semantics=("arbitrary",),
                                             vmem_limit_bytes=VMEM_LIMIT),
        name="experts",
    )(ea, eb, blk, valid, xs, wg, wu, wd, wg, wu, wd)


def _combine_kernel(pos_cur_ref, pos_nxt_ref, h1_ref, route_ref, p_ref, wpp_ref, wpg_ref, gple_ref, gfin_ref,
                    ys_hbm, o_ref, buf_ref, sem):
    TM = h1_ref.shape[0]
    ROWS = 2 * TOK_ROWS
    i = pl.program_id(0)
    n = pl.num_programs(0)
    slot = i % 2

    def copy(pos_ref, s, r):
        src = pl.multiple_of(pos_ref[0, r] * ROWS, ROWS)
        dst = pl.multiple_of(r * ROWS, ROWS)
        return pltpu.make_async_copy(ys_hbm.at[pl.ds(src, ROWS)], buf_ref.at[s, pl.ds(dst, ROWS)], sem.at[s])

    def gather(pos_ref, s):
        def start(r, c):
            copy(pos_ref, s, r).start()
            return c
        lax.fori_loop(0, TM, start, 0)

    @pl.when(i == 0)
    def _():
        gather(pos_cur_ref, 0)

    @pl.when(i + 1 < n)
    def _():
        gather(pos_nxt_ref, 1 - slot)

    def wait(r, c):
        copy(pos_cur_ref, slot, r).wait()
        return c

    lax.fori_loop(0, TM, wait, 0)

    cur = buf_ref.at[slot]
    ya = _from_token_tiles(cur, TM, ROWS)
    yb = _from_token_tiles(cur, TM, ROWS, TOK_ROWS)
    h2 = h1_ref[...] + route_ref[:, 0:1] * ya + route_ref[:, 1:2] * yb
    n3 = _rms(h2, gple_ref[...]).astype(BF16)
    gate = _sigmoid(_dot(n3, wpg_ref[...]))
    pp = _dot(p_ref[...].astype(BF16), wpp_ref[...])
    o_ref[...] = _rms(h2 + pp * gate, gfin_ref[...])


def _combine(pos, h1, route, p2, wpp, wpg, g_ple, g_final, ys):
    T = h1.shape[0]
    tm = TM_COMB
    nt = T // tm
    pos3 = pos.reshape(nt, 1, tm)
    row = lambda i: (i, 0)
    const = lambda i: (0, 0)
    return pl.pallas_call(
        _combine_kernel,
        grid=(nt,),
        in_specs=[pl.BlockSpec((None, 1, tm), lambda i: (i, 0, 0), memory_space=pltpu.SMEM),
                  pl.BlockSpec((None, 1, tm), lambda i: (jnp.minimum(i + 1, nt - 1), 0, 0),
                               memory_space=pltpu.SMEM),
                  pl.BlockSpec((tm, D_MODEL), row),
                  pl.BlockSpec((tm, LANES), row),
                  pl.BlockSpec((tm, D_PLE), row),
                  pl.BlockSpec(wpp.shape, const),
                  pl.BlockSpec(wpg.shape, const),
                  pl.BlockSpec((1, D_MODEL), const),
                  pl.BlockSpec((1, D_MODEL), const),
                  pl.BlockSpec(memory_space=pl.ANY)],
        out_specs=pl.BlockSpec((tm, D_MODEL), row),
        out_shape=jax.ShapeDtypeStruct((T, D_MODEL), F32),
        scratch_shapes=[pltpu.VMEM((2, tm * 2 * TOK_ROWS, LANES), F32),
                        pltpu.SemaphoreType.DMA((2,))],
        compiler_params=pltpu.CompilerParams(dimension_semantics=("arbitrary",),
                                             vmem_limit_bytes=VMEM_LIMIT),
        name="combine_ple",
    )(pos3, pos3, h1, route, p2, wpp, wpg, g_ple, g_final, ys)


def _tile_tables(counts, n_tiles):
    nt = (counts + TM_MOE - 1) // TM_MOE
    cum = jnp.cumsum(nt)
    off = (cum - nt) * TM_MOE
    total = cum[-1]
    tile = jnp.arange(n_tiles, dtype=I32)
    valid = (tile < total).astype(I32)
    blk = jnp.maximum(jnp.minimum(tile, total - 1), 0)
    tb = jnp.minimum(jnp.sum((cum[None, :] <= blk[:, None]).astype(I32), axis=1), N_BUCKETS - 1)
    pair_a = jnp.array([0, 0, 0, 1, 1, 2], I32)
    pair_b = jnp.array([1, 2, 3, 2, 3, 3], I32)
    grp = tb // N_PAIRS
    ea = grp * EXPERTS_PER_GROUP + pair_a[tb % N_PAIRS]
    eb = grp * EXPERTS_PER_GROUP + pair_b[tb % N_PAIRS]
    return off, ea, eb, blk, valid


def _head_tiles(w):
    d = w.shape[0]
    return jnp.pad(w.reshape(d, FOX_HEADS, FOX_HEAD_DIM), ((0, 0), (0, 0), (0, LANES - FOX_HEAD_DIM))).reshape(d, D_QK)


def kernel(x, p, w_in, conv_w, conv_b, dt_bias, a_log, d_skip, g_ssd, fox_fbias, w_out, g_mix, g_ffn,
           w_route_group, b_route_group, w_route_expert, b_route_expert, w_exp_gate, w_exp_up, w_exp_down,
           g_ple, w_ple_proj, w_ple_gate, g_final):
    B, S, _ = x.shape
    T = B * S
    assert S % L_SSD == 0 and S % TQ_FOX == 0 and S % TM_PROJ == 0 and T % TD_DISP == 0
    x2 = x.reshape(T, D_MODEL)
    p2 = p[0].reshape(T, D_PLE)

    wi = w_in[0]
    o_xbc = D_SSD
    o_dt = o_xbc + D_XBC
    o_q = o_dt + SSD_HEADS
    o_k = o_q + D_FOX
    o_v = o_k + D_FOX
    o_f = o_v + D_FOX
    wz = wi[:, 0:o_xbc].astype(BF16)
    wxbc = wi[:, o_xbc:o_dt].astype(BF16)
    wq = _head_tiles(wi[:, o_q:o_k]).astype(BF16)
    wk = _head_tiles(wi[:, o_k:o_v]).astype(BF16)
    wvt = wi[:, o_v:o_f].T.astype(BF16)
    w_dt = wi[:, o_dt:o_q]
    w_f = wi[:, o_f:o_f + FOX_HEADS]
    wsm = jnp.pad(jnp.concatenate([w_dt] + [w_f] * F_PARTS, axis=1),
                  ((0, 0), (0, LANES - SSD_HEADS - F_PARTS * FOX_HEADS))).astype(BF16)
    wsmt = w_dt.T.astype(BF16)
    lane_pad = lambda v: jnp.pad(v.reshape(1, -1), ((0, 0), (0, LANES - v.shape[-1])))
    colb = lane_pad(jnp.concatenate([dt_bias[0]] + [fox_fbias[0]] * F_PARTS))
    cola = lane_pad(a_log[0])
    rowb = jnp.broadcast_to(dt_bias[0][:, None], (SSD_HEADS, LANES))
    rowa = jnp.broadcast_to(a_log[0][:, None], (SSD_HEADS, LANES))
    dexp = jnp.repeat(d_skip[0], SSD_HEAD_DIM).reshape(1, D_SSD)
    expand = (jnp.arange(LANES)[:, None] == (jnp.arange(D_SSD) // SSD_HEAD_DIM)[None, :]).astype(BF16)
    col = jnp.arange(D_QK)
    bias_col = (col % LANES >= FOX_HEAD_DIM) & (col % LANES < FOX_HEAD_DIM + F_PARTS)
    qones = bias_col.astype(F32).reshape(1, D_QK)
    src_lane = SSD_HEADS + (col % LANES - FOX_HEAD_DIM) * FOX_HEADS + col // LANES
    place = (bias_col[None, :] & (jnp.arange(LANES)[:, None] == src_lane[None, :])).astype(BF16)
    wo = w_out[0].astype(BF16)
    wr = jnp.pad(jnp.concatenate([w_route_group[0], w_route_expert[0]], axis=1),
                 ((0, 0), (0, LANES - N_GROUPS - N_EXPERTS)))
    br = lane_pad(jnp.concatenate([b_route_group[0], b_route_expert[0]]))
    wg = w_exp_gate[0].reshape(N_EXPERTS, D_MODEL, D_EXPERT).astype(BF16)
    wu = w_exp_up[0].reshape(N_EXPERTS, D_MODEL, D_EXPERT).astype(BF16)
    wd = w_exp_down[0].reshape(N_EXPERTS, D_EXPERT, D_MODEL).astype(BF16)

    z, xbc, q, k, vt, sm, smt = _in_proj(x2, g_mix[0].reshape(1, -1), wz, wxbc, wq, wk, wvt, wsm, wsmt,
                                         colb, qones, place, S)
    yssd = _ssd(xbc, z, sm, smt, conv_w[0].reshape(SSD_CONV, D_XBC), conv_b[0].reshape(1, -1),
                colb, cola, rowb, rowa, dexp, g_ssd[0].reshape(1, -1), expand, B, S)
    yfoxt = _fox(q, k, vt, B, S)
    h1, xrow, route, cnt = _out_proj(x2, yssd, yfoxt, wo[0:D_SSD], wo[D_SSD:], g_ffn[0].reshape(1, -1), wr, br)

    n_rows = T + N_BUCKETS * TM_MOE
    bucket = route[:, 2].astype(I32)
    rank = route[:, 3].astype(I32)
    off, ea, eb, blk, valid = _tile_tables(cnt[0, 0:N_BUCKETS].astype(I32), n_rows // TM_MOE)
    pos = off[bucket] + rank

    xs = _dispatch(pos, xrow, n_rows)
    ys = _experts(ea, eb, blk, valid, xs, wg, wu, wd)
    out = _combine(pos, h1, route, p2, w_ple_proj[0].astype(BF16), w_ple_gate[0].astype(BF16),
                   g_ple[0].reshape(1, -1), g_final.reshape(1, -1), ys)
    return out.reshape(B, S, D_MODEL)
```

```python
import functools

import jax
import jax.numpy as jnp
from jax import lax
from jax.experimental import pallas as pl
from jax.experimental.pallas import tpu as pltpu

F32 = jnp.float32
BF16 = jnp.bfloat16
I32 = jnp.int32

D_MODEL = 1024
SSD_HEADS = 8
SSD_HEAD_DIM = 64
SSD_GROUPS = 2
SSD_STATE = 128
SSD_CONV = 4
D_SSD = SSD_HEADS * SSD_HEAD_DIM
D_XBC = D_SSD + 2 * SSD_GROUPS * SSD_STATE
FOX_HEADS = 8
FOX_HEAD_DIM = 64
D_FOX = FOX_HEADS * FOX_HEAD_DIM
N_GROUPS = 4
EXPERTS_PER_GROUP = 4
N_EXPERTS = N_GROUPS * EXPERTS_PER_GROUP
N_PAIRS = 6
N_BUCKETS = N_GROUPS * N_PAIRS
D_EXPERT = 512
D_PLE = 256
EPS = 1e-6
LOG2E = 1.4426950408889634
LANES = 128
SUBLANES = 8
TOK_ROWS = D_MODEL // LANES
D_QK = FOX_HEADS * LANES
F_PARTS = 3
VMEM_LIMIT = 52 * 1024 * 1024

TM_PROJ = 512
L_SSD = 256
TQ_FOX = 512
TK_FOX = 256
FOX_HEADS_PER_LOOP = 8
TM_MOE = 256
TD_DISP = 1024
TM_COMB = 256
DMA_WAIT_UNROLL = 16

_NT = (((1,), (1,)), ((), ()))
_TN = (((0,), (0,)), ((), ()))


def _rms(x, g):
    ms = jnp.mean(x * x, axis=-1, keepdims=True)
    return x * lax.rsqrt(ms + EPS) * g


def _sigmoid(x):
    return 1.0 / (1.0 + jnp.exp(-x))


def _softplus(x):
    return jnp.maximum(x, 0.0) + jnp.log(1.0 + jnp.exp(-jnp.abs(x)))


def _split_bf16(x, parts):
    out = []
    r = x
    for _ in range(parts):
        h = r.astype(BF16)
        out.append(h)
        r = r - h.astype(F32)
    return out


def _dot(a, b):
    return jnp.dot(a, b, preferred_element_type=F32)


def _dot_exact(a01, x, parts):
    acc = None
    for piece in _split_bf16(x, parts):
        t = _dot(a01, piece)
        acc = t if acc is None else acc + t
    return acc


def _to_token_tiles(ref, x, stride, offset=0):
    m = x.shape[0]
    for s in range(TOK_ROWS):
        ref[pl.ds(offset + s, m, stride=stride), :] = x[:, s * LANES:(s + 1) * LANES]


def _from_token_tiles(ref, m, stride, offset=0):
    return jnp.concatenate([ref[pl.ds(offset + s, m, stride=stride), :] for s in range(TOK_ROWS)], axis=1)


def _inproj_kernel(x_ref, g_ref, wz_ref, wxbc_ref, wq_ref, wk_ref, wvt_ref, wsm_ref, wsmt_ref,
                   colb_ref, qones_ref, place_ref,
                   z_ref, xbc_ref, q_ref, k_ref, vt_ref, sm_ref, smt_ref, fcar_ref, *, tiles_per_seq):
    TM = x_ref.shape[0]

    @pl.when(pl.program_id(0) % tiles_per_seq == 0)
    def _():
        fcar_ref[...] = jnp.zeros(fcar_ref.shape, F32)

    hn = _rms(x_ref[...], g_ref[...]).astype(BF16)
    z_ref[...] = _dot(hn, wz_ref[...]).astype(BF16)
    xbc_ref[...] = _dot(hn, wxbc_ref[...]).astype(BF16)
    q_ref[...] = (_dot(hn, wq_ref[...]) * (FOX_HEAD_DIM ** -0.5 * LOG2E) + qones_ref[...]).astype(BF16)
    vt_ref[...] = lax.dot_general(wvt_ref[...], hn, _NT, preferred_element_type=F32).astype(BF16)
    sm = _dot(hn, wsm_ref[...])
    sm_ref[...] = sm
    smt_ref[...] = lax.dot_general(wsmt_ref[...], hn, _NT, preferred_element_type=F32)

    raw = sm + colb_ref[...]
    logf = -_softplus(-raw)
    ri = lax.broadcasted_iota(I32, (TM, TM), 0)
    ci = lax.broadcasted_iota(I32, (TM, TM), 1)
    fcum = _dot_exact((ri >= ci).astype(BF16), logf, 3) + fcar_ref[0:1, :]
    fcar_ref[...] = jnp.broadcast_to(fcum[TM - 1:TM, :], fcar_ref.shape)
    fs = fcum * (-LOG2E)
    hi = fs.astype(BF16)
    r1 = fs - hi.astype(F32)
    mid = r1.astype(BF16)
    lo = (r1 - mid.astype(F32)).astype(BF16)
    lane = lax.broadcasted_iota(I32, (TM, LANES), 1)
    pieces = jnp.where(lane < 16, hi, jnp.where(lane < 24, mid, lo))
    k_ref[...] = (_dot(hn, wk_ref[...]) + _dot(pieces, place_ref[...])).astype(BF16)


def _in_proj(x2, g_mix, wz, wxbc, wq, wk, wvt, wsm, wsmt, colb, qones, place, S):
    T = x2.shape[0]
    tm = TM_PROJ
    row = lambda i: (i, 0)
    const = lambda i: (0, 0)
    full = lambda a: pl.BlockSpec(a.shape, const)
    return pl.pallas_call(
        functools.partial(_inproj_kernel, tiles_per_seq=S // tm),
        grid=(T // tm,),
        in_specs=[pl.BlockSpec((tm, D_MODEL), row), full(g_mix), full(wz), full(wxbc), full(wq), full(wk),
                  full(wvt), full(wsm), full(wsmt), full(colb), full(qones), full(place)],
        out_specs=[pl.BlockSpec((tm, D_SSD), row),
                   pl.BlockSpec((tm, D_XBC), row),
                   pl.BlockSpec((tm, D_QK), row),
                   pl.BlockSpec((tm, D_QK), row),
                   pl.BlockSpec((D_FOX, tm), lambda i: (0, i)),
                   pl.BlockSpec((tm, LANES), row),
                   pl.BlockSpec((SSD_HEADS, tm), lambda i: (0, i))],
        out_shape=[jax.ShapeDtypeStruct((T, D_SSD), BF16),
                   jax.ShapeDtypeStruct((T, D_XBC), BF16),
                   jax.ShapeDtypeStruct((T, D_QK), BF16),
                   jax.ShapeDtypeStruct((T, D_QK), BF16),
                   jax.ShapeDtypeStruct((D_FOX, T), BF16),
                   jax.ShapeDtypeStruct((T, LANES), F32),
                   jax.ShapeDtypeStruct((SSD_HEADS, T), F32)],
        scratch_shapes=[pltpu.VMEM((SUBLANES, LANES), F32)],
        compiler_params=pltpu.CompilerParams(dimension_semantics=("arbitrary",),
                                             vmem_limit_bytes=VMEM_LIMIT),
        name="in_proj",
    )(x2, g_mix, wz, wxbc, wq, wk, wvt, wsm, wsmt, colb, qones, place)


def _ssd_kernel(xbc_ref, z_ref, sm_ref, smt_ref, convw_ref, convb_ref, colb_ref, cola_ref,
                rowb_ref, rowa_ref, dexp_ref, gssd_ref, expand_ref,
                y_ref, ext_ref, st_ref):
    L = xbc_ref.shape[0]
    R = SSD_HEADS // SSD_GROUPS
    c = pl.program_id(1)

    @pl.when(c == 0)
    def _():
        ext_ref[0:8, :] = jnp.zeros((8, D_XBC), F32)
        st_ref[...] = jnp.zeros(st_ref.shape, F32)

    ext_ref[8:8 + L, :] = xbc_ref[...].astype(F32)
    conv = convb_ref[...] + convw_ref[0:1, :] * ext_ref[5:5 + L, :]
    for kk in range(1, SSD_CONV):
        conv = conv + convw_ref[kk:kk + 1, :] * ext_ref[5 + kk:5 + kk + L, :]
    ext_ref[0:8, :] = ext_ref[L:L + 8, :]
    xc = conv * _sigmoid(conv)
    xs = xc[:, 0:D_SSD]
    xs_bf = xs.astype(BF16)
    bm = xc[:, D_SSD:D_SSD + SSD_GROUPS * SSD_STATE].astype(BF16)
    cm = xc[:, D_SSD + SSD_GROUPS * SSD_STATE:].astype(BF16)

    ri = lax.broadcasted_iota(I32, (L, L), 0)
    ci = lax.broadcasted_iota(I32, (L, L), 1)
    causal = ri >= ci
    tril = causal.astype(BF16)
    triu = (ri <= ci).astype(BF16)

    dtr = _softplus(smt_ref[...] + rowb_ref[:, 0:1])
    adt_r = dtr * (-jnp.exp(rowa_ref[:, 0:1]))
    csr = None
    for part in _split_bf16(adt_r, 3):
        t = _dot(part, triu)
        csr = t if csr is None else csr + t

    dtc = _softplus(sm_ref[...] + colb_ref[...])
    adt_c = dtc * (-jnp.exp(cola_ref[...]))
    cs_c = _dot_exact(tril, adt_c, 3)
    cs_last = cs_c[L - 1:L, :]
    e1 = jnp.exp(cs_c)
    wst = dtc * jnp.exp(cs_last - cs_c)
    ex = expand_ref[...]
    p1 = _split_bf16(e1, 2)
    e1x = _dot(p1[0], ex) + _dot(p1[1], ex)
    p2 = _split_bf16(wst, 2)
    wstx = _dot(p2[0], ex) + _dot(p2[1], ex)

    xw = (xs * wstx).astype(BF16)
    dec8 = jnp.broadcast_to(jnp.exp(csr[:, L - 1:L]), (SSD_HEADS, SSD_STATE))

    ydiag = []
    yoff = []
    for g in range(SSD_GROUPS):
        bg = bm[:, g * SSD_STATE:(g + 1) * SSD_STATE]
        cg = cm[:, g * SSD_STATE:(g + 1) * SSD_STATE]
        gmat = lax.dot_general(cg, bg, _NT, preferred_element_type=F32)
        s_old = st_ref[g]
        yoff.append(lax.dot_general(cg, s_old.astype(BF16), _NT, preferred_element_type=F32))
        for j in range(R):
            h = g * R + j
            seg = cs_c[:, h:h + 1] - csr[h:h + 1, :]
            lm = jnp.exp(jnp.where(causal, seg, -jnp.inf))
            m = (gmat * lm * dtr[h:h + 1, :]).astype(BF16)
            ydiag.append(_dot(m, xs_bf[:, h * SSD_HEAD_DIM:(h + 1) * SSD_HEAD_DIM]))
        upd = lax.dot_general(xw[:, g * R * SSD_HEAD_DIM:(g + 1) * R * SSD_HEAD_DIM], bg, _TN,
                              preferred_element_type=F32)
        dec = jnp.concatenate(
            [jnp.broadcast_to(dec8[g * R + j:g * R + j + 1, :], (SSD_HEAD_DIM, SSD_STATE)) for j in range(R)],
            axis=0)
        st_ref[g] = dec * s_old + upd

    y = jnp.concatenate(ydiag, axis=1) + e1x * jnp.concatenate(yoff, axis=1) + dexp_ref[...] * xs
    zf = z_ref[...].astype(F32)
    y_ref[...] = _rms(y * (zf * _sigmoid(zf)), gssd_ref[...]).astype(BF16)


def _ssd(xbc, z, sm, smt, convw, convb, colb, cola, rowb, rowa, dexp, gssd, expand, B, S):
    L = L_SSD
    nc = S // L
    T = B * S
    row = lambda b, c: (b * nc + c, 0)
    const = lambda b, c: (0, 0)
    full = lambda a: pl.BlockSpec(a.shape, const)
    return pl.pallas_call(
        _ssd_kernel,
        grid=(B, nc),
        in_specs=[pl.BlockSpec((L, D_XBC), row),
                  pl.BlockSpec((L, D_SSD), row),
                  pl.BlockSpec((L, LANES), row),
                  pl.BlockSpec((SSD_HEADS, L), lambda b, c: (0, b * nc + c)),
                  full(convw), full(convb), full(colb), full(cola), full(rowb), full(rowa),
                  full(dexp), full(gssd), full(expand)],
        out_specs=pl.BlockSpec((L, D_SSD), row),
        out_shape=jax.ShapeDtypeStruct((T, D_SSD), BF16),
        scratch_shapes=[pltpu.VMEM((L + 8, D_XBC), F32),
                        pltpu.VMEM((SSD_GROUPS, (SSD_HEADS // SSD_GROUPS) * SSD_HEAD_DIM, SSD_STATE), F32)],
        compiler_params=pltpu.CompilerParams(dimension_semantics=("arbitrary", "arbitrary"),
                                             vmem_limit_bytes=VMEM_LIMIT),
        name="ssd",
    )(xbc, z, sm, smt, convw, convb, colb, cola, rowb, rowa, dexp, gssd, expand)


def _fox_kernel(q_ref, k_ref, vt_ref, o_ref):
    TQ = q_ref.shape[0]
    TK = TK_FOX
    n_diag = TQ // TK
    n_full = pl.program_id(1) * n_diag
    key_idx = lax.broadcasted_iota(I32, (TK, TQ), 0)
    qry_idx = lax.broadcasted_iota(I32, (TK, TQ), 1)
    outs = []
    for g in range(FOX_HEADS // FOX_HEADS_PER_LOOP):
        heads = range(g * FOX_HEADS_PER_LOOP, (g + 1) * FOX_HEADS_PER_LOOP)
        qs = [q_ref[:, h * LANES:(h + 1) * LANES] for h in heads]

        def block(j, carry, diag):
            ks = pl.multiple_of(j * TK, TK)
            scores = [lax.dot_general(k_ref[pl.ds(ks, TK), h * LANES:(h + 1) * LANES], qs[n], _NT,
                                      preferred_element_type=F32) for n, h in enumerate(heads)]
            new = []
            for n, h in enumerate(heads):
                m, l, acc = carry[n]
                s = scores[n]
                if diag is not None:
                    s = jnp.where(key_idx + diag * TK <= qry_idx, s, -jnp.inf)
                mn = jnp.maximum(m, jnp.max(s, axis=0, keepdims=True))
                a = jnp.exp2(m - mn)
                p = jnp.exp2(s - mn)
                l = a * l + jnp.sum(p, axis=0, keepdims=True)
                vt = vt_ref[h * FOX_HEAD_DIM:(h + 1) * FOX_HEAD_DIM, pl.ds(ks, TK)]
                acc = a * acc + _dot(vt, p.astype(BF16))
                new.append((mn, l, acc))
            return tuple(new)

        carry = tuple((jnp.full((1, TQ), -1e30, F32), jnp.zeros((1, TQ), F32),
                       jnp.zeros((FOX_HEAD_DIM, TQ), F32)) for _ in heads)
        carry = lax.fori_loop(0, n_full, lambda j, c: block(j, c, None), carry)
        for d in range(n_diag):
            carry = block(n_full + d, carry, d)
        for m, l, acc in carry:
            outs.append(acc / l)
    o_ref[...] = jnp.concatenate(outs, axis=0).astype(BF16)


def _fox(q, k, vt, B, S):
    TQ = TQ_FOX
    nq = S // TQ
    T = B * S
    return pl.pallas_call(
        _fox_kernel,
        grid=(B, nq),
        in_specs=[pl.BlockSpec((TQ, D_QK), lambda b, i: (b * nq + i, 0)),
                  pl.BlockSpec((S, D_QK), lambda b, i: (b, 0)),
                  pl.BlockSpec((D_FOX, S), lambda b, i: (0, b))],
        out_specs=pl.BlockSpec((D_FOX, TQ), lambda b, i: (0, b * nq + i)),
        out_shape=jax.ShapeDtypeStruct((D_FOX, T), BF16),
        compiler_params=pltpu.CompilerParams(dimension_semantics=("arbitrary", "arbitrary"),
                                             vmem_limit_bytes=VMEM_LIMIT),
        name="fox",
    )(q, k, vt)


def _outproj_kernel(x_ref, ys_ref, yft_ref, wo1_ref, wo2_ref, g_ref, wr_ref, br_ref,
                    h1_ref, xrow_ref, route_ref, cnt_ref, carry_ref):
    TM = x_ref.shape[0]
    i = pl.program_id(0)

    @pl.when(i == 0)
    def _():
        carry_ref[...] = jnp.zeros(carry_ref.shape, F32)

    h1 = (x_ref[...] + _dot(ys_ref[...], wo1_ref[...])
          + lax.dot_general(yft_ref[...], wo2_ref[...], _TN, preferred_element_type=F32))
    h1_ref[...] = h1
    hn = _rms(h1, g_ref[...])
    h_hi, h_lo = _split_bf16(hn, 2)
    logits = (_dot(h_hi, wr_ref[0]) + _dot(h_lo, wr_ref[0]) + _dot(h_hi, wr_ref[1])) + br_ref[...]
    lane = lax.broadcasted_iota(I32, (TM, LANES), 1)
    ninf = -jnp.inf
    gl = jnp.where(lane < N_GROUPS, logits, ninf)
    gmax = jnp.max(gl, axis=-1, keepdims=True)
    p_grp = 1.0 / jnp.sum(jnp.exp(gl - gmax), axis=-1, keepdims=True)
    g_idx = jnp.min(jnp.where(gl == gmax, lane, LANES), axis=-1, keepdims=True)
    e0 = N_GROUPS + EXPERTS_PER_GROUP * g_idx
    el = jnp.where((lane >= e0) & (lane < e0 + EXPERTS_PER_GROUP), logits, ninf)
    v1 = jnp.max(el, axis=-1, keepdims=True)
    i1 = jnp.min(jnp.where(el == v1, lane, LANES), axis=-1, keepdims=True)
    el2 = jnp.where(lane == i1, ninf, el)
    v2 = jnp.max(el2, axis=-1, keepdims=True)
    i2 = jnp.min(jnp.where(el2 == v2, lane, LANES), axis=-1, keepdims=True)
    e2 = jnp.exp(v2 - v1)
    w1 = p_grp / (1.0 + e2)
    w2 = p_grp * e2 / (1.0 + e2)
    l1 = i1 - e0
    l2 = i2 - e0
    first = l1 < l2
    a = jnp.where(first, l1, l2)
    b = jnp.where(first, l2, l1)
    wa = jnp.where(first, w1, w2)
    wb = jnp.where(first, w2, w1)
    pair = jnp.right_shift(a * (5 - a), 1) + b - 1
    bucket = g_idx * N_PAIRS + pair

    onehot = (lane == bucket).astype(BF16)
    ri = lax.broadcasted_iota(I32, (TM, TM), 0)
    ci = lax.broadcasted_iota(I32, (TM, TM), 1)
    incl = _dot((ri >= ci).astype(BF16), onehot)
    oh = onehot.astype(F32)
    rank = jnp.sum((incl - oh + carry_ref[0:1, :]) * oh, axis=-1, keepdims=True)
    carry_ref[...] = carry_ref[...] + jnp.broadcast_to(incl[TM - 1:TM, :], carry_ref.shape)
    cnt_ref[...] = carry_ref[...]

    route_ref[...] = jnp.where(lane == 0, wa, jnp.where(lane == 1, wb, jnp.where(
        lane == 2, bucket.astype(F32), jnp.where(lane == 3, rank, 0.0))))
    _to_token_tiles(xrow_ref, hn, TOK_ROWS)


def _out_proj(x2, yssd, yfoxt, wo1, wo2, g_ffn, wr, br):
    T = x2.shape[0]
    tm = TM_PROJ
    row = lambda i: (i, 0)
    const = lambda i: (0, 0)
    return pl.pallas_call(
        _outproj_kernel,
        grid=(T // tm,),
        in_specs=[pl.BlockSpec((tm, D_MODEL), row),
                  pl.BlockSpec((tm, D_SSD), row),
                  pl.BlockSpec((D_FOX, tm), lambda i: (0, i)),
                  pl.BlockSpec(wo1.shape, const),
                  pl.BlockSpec(wo2.shape, const),
                  pl.BlockSpec((1, D_MODEL), const),
                  pl.BlockSpec(wr.shape, lambda i: (0, 0, 0)),
                  pl.BlockSpec((1, LANES), const)],
        out_specs=[pl.BlockSpec((tm, D_MODEL), row),
                   pl.BlockSpec((tm * TOK_ROWS, LANES), row),
                   pl.BlockSpec((tm, LANES), row),
                   pl.BlockSpec((SUBLANES, LANES), const)],
        out_shape=[jax.ShapeDtypeStruct((T, D_MODEL), F32),
                   jax.ShapeDtypeStruct((T * TOK_ROWS, LANES), F32),
                   jax.ShapeDtypeStruct((T, LANES), F32),
                   jax.ShapeDtypeStruct((SUBLANES, LANES), F32)],
        scratch_shapes=[pltpu.VMEM((SUBLANES, LANES), F32)],
        compiler_params=pltpu.CompilerParams(dimension_semantics=("arbitrary",),
                                             vmem_limit_bytes=VMEM_LIMIT),
        name="out_proj_router",
    )(x2, yssd, yfoxt, wo1, wo2, g_ffn, wr, br)


def _dispatch_kernel(pos_ref, src_ref, dst_init_hbm, dst_hbm, sem):
    del dst_init_hbm
    TD = pos_ref.shape[-1]

    def copy(r):
        dst = pl.multiple_of(pos_ref[0, r] * TOK_ROWS, TOK_ROWS)
        src = pl.multiple_of(r * TOK_ROWS, TOK_ROWS)
        return pltpu.make_async_copy(src_ref.at[pl.ds(src, TOK_ROWS)], dst_hbm.at[pl.ds(dst, TOK_ROWS)], sem)

    def start(r2, c):
        for u in range(2):
            copy(2 * r2 + u).start(priority=u)
        return c

    lax.fori_loop(0, TD // 2, start, 0)

    def wait(r, c):
        copy(r).wait()
        return c

    lax.fori_loop(0, TD, wait, 0, unroll=DMA_WAIT_UNROLL)


def _dispatch(pos, xrow, n_rows):
    T = pos.shape[0]
    td = TD_DISP
    pos3 = pos.reshape(T // td, 1, td)
    dst0 = jnp.zeros((n_rows * TOK_ROWS, LANES), F32)
    return pl.pallas_call(
        _dispatch_kernel,
        grid=(T // td,),
        in_specs=[pl.BlockSpec((None, 1, td), lambda i: (i, 0, 0), memory_space=pltpu.SMEM),
                  pl.BlockSpec((td * TOK_ROWS, LANES), lambda i: (i, 0)),
                  pl.BlockSpec(memory_space=pl.ANY)],
        out_specs=pl.BlockSpec(memory_space=pl.ANY),
        out_shape=jax.ShapeDtypeStruct((n_rows * TOK_ROWS, LANES), F32),
        scratch_shapes=[pltpu.SemaphoreType.DMA(())],
        input_output_aliases={2: 0},
        compiler_params=pltpu.CompilerParams(dimension_semantics=("arbitrary",)),
        name="dispatch",
    )(pos3, xrow, dst0)


def _experts_kernel(ea_ref, eb_ref, blk_ref, valid_ref,
                    xs_ref, wga_ref, wua_ref, wda_ref, wgb_ref, wub_ref, wdb_ref, ys_ref):
    del ea_ref, eb_ref, blk_ref
    i = pl.program_id(0)
    TM = xs_ref.shape[0] // TOK_ROWS

    @pl.when(valid_ref[i] > 0)
    def _():
        x = _from_token_tiles(xs_ref, TM, TOK_ROWS).astype(BF16)

        def expert(wg_ref, wu_ref, wd_ref):
            g = _dot(x, wg_ref[...])
            u = _dot(x, wu_ref[...])
            return _dot((g * _sigmoid(g) * u).astype(BF16), wd_ref[...])

        _to_token_tiles(ys_ref, expert(wga_ref, wua_ref, wda_ref), 2 * TOK_ROWS)
        _to_token_tiles(ys_ref, expert(wgb_ref, wub_ref, wdb_ref), 2 * TOK_ROWS, TOK_ROWS)

    @pl.when(valid_ref[i] == 0)
    def _():
        ys_ref[...] = jnp.zeros(ys_ref.shape, F32)


def _experts(ea, eb, blk, valid, xs, wg, wu, wd):
    tm = TM_MOE
    n_rows = xs.shape[0] // TOK_ROWS
    nt = n_rows // tm
    xmap = lambda i, ea, eb, blk, valid: (blk[i], 0)
    amap = lambda i, ea, eb, blk, valid: (ea[i], 0, 0)
    bmap = lambda i, ea, eb, blk, valid: (eb[i], 0, 0)
    up_spec = lambda m: pl.BlockSpec((None, D_MODEL, D_EXPERT), m)
    dn_spec = lambda m: pl.BlockSpec((None, D_EXPERT, D_MODEL), m)
    grid_spec = pltpu.PrefetchScalarGridSpec(
        num_scalar_prefetch=4,
        grid=(nt,),
        in_specs=[pl.BlockSpec((tm * TOK_ROWS, LANES), xmap),
                  up_spec(amap), up_spec(amap), dn_spec(amap),
                  up_spec(bmap), up_spec(bmap), dn_spec(bmap)],
        out_specs=pl.BlockSpec((tm * 2 * TOK_ROWS, LANES), lambda i, ea, eb, blk, valid: (i, 0)),
    )
    return pl.pallas_call(
        _experts_kernel,
        grid_spec=grid_spec,
        out_shape=jax.ShapeDtypeStruct((n_rows * 2 * TOK_ROWS, LANES), F32),
        compiler_params=pltpu.CompilerParams(dimension_semantics=("arbitrary",),
                                             vmem_limit_bytes=VMEM_LIMIT),
        name="experts",
    )(ea, eb, blk, valid, xs, wg, wu, wd, wg, wu, wd)


def _combine_kernel(pos_cur_ref, pos_nxt_ref, h1_ref, route_ref, p_ref, wpp_ref, wpg_ref, gple_ref, gfin_ref,
                    ys_hbm, o_ref, buf_ref, sem):
    TM = h1_ref.shape[0]
    ROWS = 2 * TOK_ROWS
    i = pl.program_id(0)
    n = pl.num_programs(0)
    slot = i % 2

    def copy(pos_ref, s, r):
        src = pl.multiple_of(pos_ref[0, r] * ROWS, ROWS)
        dst = pl.multiple_of(r * ROWS, ROWS)
        return pltpu.make_async_copy(ys_hbm.at[pl.ds(src, ROWS)], buf_ref.at[s, pl.ds(dst, ROWS)], sem.at[s])

    def gather(pos_ref, s):
        def start(r2, c):
            for u in range(2):
                copy(pos_ref, s, 2 * r2 + u).start(priority=u)
            return c
        lax.fori_loop(0, TM // 2, start, 0)

    @pl.when(i == 0)
    def _():
        gather(pos_cur_ref, 0)

    @pl.when(i + 1 < n)
    def _():
        gather(pos_nxt_ref, 1 - slot)

    def wait(r, c):
        copy(pos_cur_ref, slot, r).wait()
        return c

    lax.fori_loop(0, TM, wait, 0, unroll=DMA_WAIT_UNROLL)

    cur = buf_ref.at[slot]
    ya = _from_token_tiles(cur, TM, ROWS)
    yb = _from_token_tiles(cur, TM, ROWS, TOK_ROWS)
    h2 = h1_ref[...] + route_ref[:, 0:1] * ya + route_ref[:, 1:2] * yb
    n3 = _rms(h2, gple_ref[...]).astype(BF16)
    gate = _sigmoid(_dot(n3, wpg_ref[...]))
    pp = _dot(p_ref[...].astype(BF16), wpp_ref[...])
    o_ref[...] = _rms(h2 + pp * gate, gfin_ref[...])


def _combine(pos, h1, route, p2, wpp, wpg, g_ple, g_final, ys):
    T = h1.shape[0]
    tm = TM_COMB
    nt = T // tm
    pos3 = pos.reshape(nt, 1, tm)
    row = lambda i: (i, 0)
    const = lambda i: (0, 0)
    return pl.pallas_call(
        _combine_kernel,
        grid=(nt,),
        in_specs=[pl.BlockSpec((None, 1, tm), lambda i: (i, 0, 0), memory_space=pltpu.SMEM),
                  pl.BlockSpec((None, 1, tm), lambda i: (jnp.minimum(i + 1, nt - 1), 0, 0),
                               memory_space=pltpu.SMEM),
                  pl.BlockSpec((tm, D_MODEL), row),
                  pl.BlockSpec((tm, LANES), row),
                  pl.BlockSpec((tm, D_PLE), row),
                  pl.BlockSpec(wpp.shape, const),
                  pl.BlockSpec(wpg.shape, const),
                  pl.BlockSpec((1, D_MODEL), const),
                  pl.BlockSpec((1, D_MODEL), const),
                  pl.BlockSpec(memory_space=pl.ANY)],
        out_specs=pl.BlockSpec((tm, D_MODEL), row),
        out_shape=jax.ShapeDtypeStruct((T, D_MODEL), F32),
        scratch_shapes=[pltpu.VMEM((2, tm * 2 * TOK_ROWS, LANES), F32),
                        pltpu.SemaphoreType.DMA((2,))],
        compiler_params=pltpu.CompilerParams(dimension_semantics=("arbitrary",),
                                             vmem_limit_bytes=VMEM_LIMIT),
        name="combine_ple",
    )(pos3, pos3, h1, route, p2, wpp, wpg, g_ple, g_final, ys)


def _tile_tables(counts, n_tiles):
    nt = (counts + TM_MOE - 1) // TM_MOE
    cum = jnp.cumsum(nt)
    off = (cum - nt) * TM_MOE
    total = cum[-1]
    tile = jnp.arange(n_tiles, dtype=I32)
    valid = (tile < total).astype(I32)
    blk = jnp.maximum(jnp.minimum(tile, total - 1), 0)
    tb = jnp.minimum(jnp.sum((cum[None, :] <= blk[:, None]).astype(I32), axis=1), N_BUCKETS - 1)
    pair_a = jnp.array([0, 0, 0, 1, 1, 2], I32)
    pair_b = jnp.array([1, 2, 3, 2, 3, 3], I32)
    grp = tb // N_PAIRS
    ea = grp * EXPERTS_PER_GROUP + pair_a[tb % N_PAIRS]
    eb = grp * EXPERTS_PER_GROUP + pair_b[tb % N_PAIRS]
    return off, ea, eb, blk, valid


def _head_tiles(w):
    d = w.shape[0]
    return jnp.pad(w.reshape(d, FOX_HEADS, FOX_HEAD_DIM), ((0, 0), (0, 0), (0, LANES - FOX_HEAD_DIM))).reshape(d, D_QK)


def kernel(x, p, w_in, conv_w, conv_b, dt_bias, a_log, d_skip, g_ssd, fox_fbias, w_out, g_mix, g_ffn,
           w_route_group, b_route_group, w_route_expert, b_route_expert, w_exp_gate, w_exp_up, w_exp_down,
           g_ple, w_ple_proj, w_ple_gate, g_final):
    B, S, _ = x.shape
    T = B * S
    assert S % L_SSD == 0 and S % TQ_FOX == 0 and S % TM_PROJ == 0 and T % TD_DISP == 0
    x2 = x.reshape(T, D_MODEL)
    p2 = p[0].reshape(T, D_PLE)

    wi = w_in[0]
    o_xbc = D_SSD
    o_dt = o_xbc + D_XBC
    o_q = o_dt + SSD_HEADS
    o_k = o_q + D_FOX
    o_v = o_k + D_FOX
    o_f = o_v + D_FOX
    wz = wi[:, 0:o_xbc].astype(BF16)
    wxbc = wi[:, o_xbc:o_dt].astype(BF16)
    wq = _head_tiles(wi[:, o_q:o_k]).astype(BF16)
    wk = _head_tiles(wi[:, o_k:o_v]).astype(BF16)
    wvt = wi[:, o_v:o_f].T.astype(BF16)
    w_dt = wi[:, o_dt:o_q]
    w_f = wi[:, o_f:o_f + FOX_HEADS]
    wsm = jnp.pad(jnp.concatenate([w_dt] + [w_f] * F_PARTS, axis=1),
                  ((0, 0), (0, LANES - SSD_HEADS - F_PARTS * FOX_HEADS))).astype(BF16)
    wsmt = w_dt.T.astype(BF16)
    lane_pad = lambda v: jnp.pad(v.reshape(1, -1), ((0, 0), (0, LANES - v.shape[-1])))
    colb = lane_pad(jnp.concatenate([dt_bias[0]] + [fox_fbias[0]] * F_PARTS))
    cola = lane_pad(a_log[0])
    rowb = jnp.broadcast_to(dt_bias[0][:, None], (SSD_HEADS, LANES))
    rowa = jnp.broadcast_to(a_log[0][:, None], (SSD_HEADS, LANES))
    dexp = jnp.repeat(d_skip[0], SSD_HEAD_DIM).reshape(1, D_SSD)
    expand = (jnp.arange(LANES)[:, None] == (jnp.arange(D_SSD) // SSD_HEAD_DIM)[None, :]).astype(BF16)
    col = jnp.arange(D_QK)
    bias_col = (col % LANES >= FOX_HEAD_DIM) & (col % LANES < FOX_HEAD_DIM + F_PARTS)
    qones = bias_col.astype(F32).reshape(1, D_QK)
    src_lane = SSD_HEADS + (col % LANES - FOX_HEAD_DIM) * FOX_HEADS + col // LANES
    place = (bias_col[None, :] & (jnp.arange(LANES)[:, None] == src_lane[None, :])).astype(BF16)
    wo = w_out[0].astype(BF16)
    wr = jnp.pad(jnp.concatenate([w_route_group[0], w_route_expert[0]], axis=1),
                 ((0, 0), (0, LANES - N_GROUPS - N_EXPERTS)))
    wr_hi = wr.astype(BF16)
    wr = jnp.stack([wr_hi, (wr - wr_hi.astype(F32)).astype(BF16)])
    br = lane_pad(jnp.concatenate([b_route_group[0], b_route_expert[0]]))
    wg = w_exp_gate[0].reshape(N_EXPERTS, D_MODEL, D_EXPERT).astype(BF16)
    wu = w_exp_up[0].reshape(N_EXPERTS, D_MODEL, D_EXPERT).astype(BF16)
    wd = w_exp_down[0].reshape(N_EXPERTS, D_EXPERT, D_MODEL).astype(BF16)

    z, xbc, q, k, vt, sm, smt = _in_proj(x2, g_mix[0].reshape(1, -1), wz, wxbc, wq, wk, wvt, wsm, wsmt,
                                         colb, qones, place, S)
    yssd = _ssd(xbc, z, sm, smt, conv_w[0].reshape(SSD_CONV, D_XBC), conv_b[0].reshape(1, -1),
                colb, cola, rowb, rowa, dexp, g_ssd[0].reshape(1, -1), expand, B, S)
    yfoxt = _fox(q, k, vt, B, S)
    h1, xrow, route, cnt = _out_proj(x2, yssd, yfoxt, wo[0:D_SSD], wo[D_SSD:], g_ffn[0].reshape(1, -1), wr, br)

    n_rows = T + N_BUCKETS * TM_MOE
    bucket = route[:, 2].astype(I32)
    rank = route[:, 3].astype(I32)
    off, ea, eb, blk, valid = _tile_tables(cnt[0, 0:N_BUCKETS].astype(I32), n_rows // TM_MOE)
    pos = rank
    for b in range(N_BUCKETS):
        pos = pos + jnp.where(bucket == b, off[b], 0)

    xs = _dispatch(pos, xrow, n_rows)
    ys = _experts(ea, eb, blk, valid, xs, wg, wu, wd)
    out = _combine(pos, h1, route, p2, w_ple_proj[0].astype(BF16), w_ple_gate[0].astype(BF16),
                   g_ple[0].reshape(1, -1), g_final.reshape(1, -1), ys)
    return out.reshape(B, S, D_MODEL)
```

```python
import functools

import jax
import jax.numpy as jnp
from jax import lax
from jax.experimental import pallas as pl
from jax.experimental.pallas import tpu as pltpu

F32 = jnp.float32
BF16 = jnp.bfloat16
I32 = jnp.int32

D_MODEL = 1024
SSD_HEADS = 8
SSD_HEAD_DIM = 64
SSD_GROUPS = 2
SSD_STATE = 128
SSD_CONV = 4
D_SSD = SSD_HEADS * SSD_HEAD_DIM
D_XBC = D_SSD + 2 * SSD_GROUPS * SSD_STATE
FOX_HEADS = 8
FOX_HEAD_DIM = 64
D_FOX = FOX_HEADS * FOX_HEAD_DIM
N_GROUPS = 4
EXPERTS_PER_GROUP = 4
N_EXPERTS = N_GROUPS * EXPERTS_PER_GROUP
N_PAIRS = 6
N_BUCKETS = N_GROUPS * N_PAIRS
D_EXPERT = 512
D_PLE = 256
EPS = 1e-6
LOG2E = 1.4426950408889634
LANES = 128
SUBLANES = 8
TOK_ROWS = D_MODEL // LANES
D_QK = FOX_HEADS * LANES
F_PARTS = 3
VT_ROWS = FOX_HEAD_DIM + 16
VMEM_LIMIT = 52 * 1024 * 1024

TM_PROJ = 512
L_SSD = 256
TQ_FOX = 512
TK_FOX = 256
TM_MOE = 256
TD_DISP = 1024
TM_COMB = 256
DMA_WAIT_UNROLL = 16

_NT = (((1,), (1,)), ((), ()))
_TN = (((0,), (0,)), ((), ()))


def _rms(x, g):
    ms = jnp.mean(x * x, axis=-1, keepdims=True)
    return x * lax.rsqrt(ms + EPS) * g


def _sigmoid(x):
    return 1.0 / (1.0 + jnp.exp(-x))


def _softplus(x):
    return jnp.maximum(x, 0.0) + jnp.log(1.0 + jnp.exp(-jnp.abs(x)))


def _split_bf16(x, parts):
    out = []
    r = x
    for _ in range(parts):
        h = r.astype(BF16)
        out.append(h)
        r = r - h.astype(F32)
    return out


def _dot(a, b):
    return jnp.dot(a, b, preferred_element_type=F32)


def _dot_exact(a01, x, parts):
    acc = None
    for piece in _split_bf16(x, parts):
        t = _dot(a01, piece)
        acc = t if acc is None else acc + t
    return acc


def _to_token_tiles(ref, x, stride, offset=0):
    m = x.shape[0]
    for s in range(TOK_ROWS):
        ref[pl.ds(offset + s, m, stride=stride), :] = x[:, s * LANES:(s + 1) * LANES]


def _from_token_tiles(ref, m, stride, offset=0):
    return jnp.concatenate([ref[pl.ds(offset + s, m, stride=stride), :] for s in range(TOK_ROWS)], axis=1)


def _inproj_kernel(x_ref, g_ref, wz_ref, wxbc_ref, wq_ref, wk_ref, wvt_ref, wsm_ref, wsmt_ref,
                   colb_ref, qones_ref, place_ref,
                   z_ref, xbc_ref, q_ref, k_ref, vt_ref, sm_ref, smt_ref, fcar_ref, *, tiles_per_seq):
    TM = x_ref.shape[0]

    @pl.when(pl.program_id(0) % tiles_per_seq == 0)
    def _():
        fcar_ref[...] = jnp.zeros(fcar_ref.shape, F32)

    hn = _rms(x_ref[...], g_ref[...]).astype(BF16)
    z_ref[...] = _dot(hn, wz_ref[...]).astype(BF16)
    xbc_ref[...] = _dot(hn, wxbc_ref[...]).astype(BF16)
    vt = lax.dot_general(wvt_ref[...], hn, _NT, preferred_element_type=F32).astype(BF16)
    ones_rows = jnp.ones((VT_ROWS - FOX_HEAD_DIM, TM), BF16)
    for h in range(FOX_HEADS):
        vt_ref[h * VT_ROWS:h * VT_ROWS + FOX_HEAD_DIM, :] = vt[h * FOX_HEAD_DIM:(h + 1) * FOX_HEAD_DIM, :]
        vt_ref[h * VT_ROWS + FOX_HEAD_DIM:(h + 1) * VT_ROWS, :] = ones_rows
    sm = _dot(hn, wsm_ref[...])
    sm_ref[...] = sm
    smt_ref[...] = lax.dot_general(wsmt_ref[...], hn, _NT, preferred_element_type=F32)

    raw = sm + colb_ref[...]
    logf = -_softplus(-raw)
    ri = lax.broadcasted_iota(I32, (TM, TM), 0)
    ci = lax.broadcasted_iota(I32, (TM, TM), 1)
    fcum = _dot_exact((ri >= ci).astype(BF16), logf, 3) + fcar_ref[0:1, :]
    fcar_ref[...] = jnp.broadcast_to(fcum[TM - 1:TM, :], fcar_ref.shape)
    fs = fcum * (-LOG2E)
    hi = fs.astype(BF16)
    r1 = fs - hi.astype(F32)
    mid = r1.astype(BF16)
    lo = (r1 - mid.astype(F32)).astype(BF16)
    lane = lax.broadcasted_iota(I32, (TM, LANES), 1)
    pieces = jnp.where(lane < 16, hi, jnp.where(lane < 24, mid, lo))
    kbias = _dot(pieces, place_ref[...])
    qc = _dot(hn, wq_ref[...]) * (FOX_HEAD_DIM ** -0.5 * LOG2E)
    kc = _dot(hn, wk_ref[...])
    for h in range(FOX_HEADS):
        pair = slice((h // 2) * LANES, (h // 2 + 1) * LANES)
        tile = slice(h * LANES, (h + 1) * LANES)
        own = (lane < FOX_HEAD_DIM) if h % 2 == 0 else (lane >= FOX_HEAD_DIM)
        q_ref[:, tile] = jnp.where(own, qc[:, pair], qones_ref[:, tile]).astype(BF16)
        k_ref[:, tile] = jnp.where(own, kc[:, pair], kbias[:, tile]).astype(BF16)


def _in_proj(x2, g_mix, wz, wxbc, wq, wk, wvt, wsm, wsmt, colb, qones, place, S):
    T = x2.shape[0]
    tm = TM_PROJ
    row = lambda i: (i, 0)
    const = lambda i: (0, 0)
    full = lambda a: pl.BlockSpec(a.shape, const)
    return pl.pallas_call(
        functools.partial(_inproj_kernel, tiles_per_seq=S // tm),
        grid=(T // tm,),
        in_specs=[pl.BlockSpec((tm, D_MODEL), row), full(g_mix), full(wz), full(wxbc), full(wq), full(wk),
                  full(wvt), full(wsm), full(wsmt), full(colb), full(qones), full(place)],
        out_specs=[pl.BlockSpec((tm, D_SSD), row),
                   pl.BlockSpec((tm, D_XBC), row),
                   pl.BlockSpec((tm, D_QK), row),
                   pl.BlockSpec((tm, D_QK), row),
                   pl.BlockSpec((FOX_HEADS * VT_ROWS, tm), lambda i: (0, i)),
                   pl.BlockSpec((tm, LANES), row),
                   pl.BlockSpec((SSD_HEADS, tm), lambda i: (0, i))],
        out_shape=[jax.ShapeDtypeStruct((T, D_SSD), BF16),
                   jax.ShapeDtypeStruct((T, D_XBC), BF16),
                   jax.ShapeDtypeStruct((T, D_QK), BF16),
                   jax.ShapeDtypeStruct((T, D_QK), BF16),
                   jax.ShapeDtypeStruct((FOX_HEADS * VT_ROWS, T), BF16),
                   jax.ShapeDtypeStruct((T, LANES), F32),
                   jax.ShapeDtypeStruct((SSD_HEADS, T), F32)],
        scratch_shapes=[pltpu.VMEM((SUBLANES, LANES), F32)],
        compiler_params=pltpu.CompilerParams(dimension_semantics=("arbitrary",),
                                             vmem_limit_bytes=VMEM_LIMIT),
        name="in_proj",
    )(x2, g_mix, wz, wxbc, wq, wk, wvt, wsm, wsmt, colb, qones, place)


def _ssd_kernel(xbc_ref, z_ref, sm_ref, smt_ref, convw_ref, convb_ref, colb_ref, cola_ref,
                rowb_ref, rowa_ref, dexp_ref, gssd_ref, expand_ref,
                y_ref, ext_ref, st_ref):
    L = xbc_ref.shape[0]
    R = SSD_HEADS // SSD_GROUPS
    c = pl.program_id(1)

    @pl.when(c == 0)
    def _():
        ext_ref[0:8, :] = jnp.zeros((8, D_XBC), F32)
        st_ref[...] = jnp.zeros(st_ref.shape, F32)

    ext_ref[8:8 + L, :] = xbc_ref[...].astype(F32)
    conv = convb_ref[...] + convw_ref[0:1, :] * ext_ref[5:5 + L, :]
    for kk in range(1, SSD_CONV):
        conv = conv + convw_ref[kk:kk + 1, :] * ext_ref[5 + kk:5 + kk + L, :]
    ext_ref[0:8, :] = ext_ref[L:L + 8, :]
    xc = conv * _sigmoid(conv)
    xs = xc[:, 0:D_SSD]
    xs_bf = xs.astype(BF16)
    bm = xc[:, D_SSD:D_SSD + SSD_GROUPS * SSD_STATE].astype(BF16)
    cm = xc[:, D_SSD + SSD_GROUPS * SSD_STATE:].astype(BF16)

    ri = lax.broadcasted_iota(I32, (L, L), 0)
    ci = lax.broadcasted_iota(I32, (L, L), 1)
    causal = ri >= ci
    tril = causal.astype(BF16)
    triu = (ri <= ci).astype(BF16)

    dtr = _softplus(smt_ref[...] + rowb_ref[:, 0:1])
    adt_r = dtr * (-jnp.exp(rowa_ref[:, 0:1]))
    csr = None
    for part in _split_bf16(adt_r, 3):
        t = _dot(part, triu)
        csr = t if csr is None else csr + t

    dtc = _softplus(sm_ref[...] + colb_ref[...])
    adt_c = dtc * (-jnp.exp(cola_ref[...]))
    cs_c = _dot_exact(tril, adt_c, 3)
    cs_last = cs_c[L - 1:L, :]
    e1 = jnp.exp(cs_c)
    wst = dtc * jnp.exp(cs_last - cs_c)
    ex = expand_ref[...]
    p1 = _split_bf16(e1, 2)
    e1x = _dot(p1[0], ex) + _dot(p1[1], ex)
    p2 = _split_bf16(wst, 2)
    wstx = _dot(p2[0], ex) + _dot(p2[1], ex)

    xw = (xs * wstx).astype(BF16)
    dec8 = jnp.broadcast_to(jnp.exp(csr[:, L - 1:L]), (SSD_HEADS, SSD_STATE))

    ydiag = []
    yoff = []
    for g in range(SSD_GROUPS):
        bg = bm[:, g * SSD_STATE:(g + 1) * SSD_STATE]
        cg = cm[:, g * SSD_STATE:(g + 1) * SSD_STATE]
        gmat = lax.dot_general(cg, bg, _NT, preferred_element_type=F32)
        s_old = st_ref[g]
        yoff.append(lax.dot_general(cg, s_old.astype(BF16), _NT, preferred_element_type=F32))
        for j in range(R):
            h = g * R + j
            seg = cs_c[:, h:h + 1] - csr[h:h + 1, :]
            lm = jnp.exp(jnp.where(causal, seg, -jnp.inf))
            m = (gmat * lm * dtr[h:h + 1, :]).astype(BF16)
            ydiag.append(_dot(m, xs_bf[:, h * SSD_HEAD_DIM:(h + 1) * SSD_HEAD_DIM]))
        upd = lax.dot_general(xw[:, g * R * SSD_HEAD_DIM:(g + 1) * R * SSD_HEAD_DIM], bg, _TN,
                              preferred_element_type=F32)
        dec = jnp.concatenate(
            [jnp.broadcast_to(dec8[g * R + j:g * R + j + 1, :], (SSD_HEAD_DIM, SSD_STATE)) for j in range(R)],
            axis=0)
        st_ref[g] = dec * s_old + upd

    y = jnp.concatenate(ydiag, axis=1) + e1x * jnp.concatenate(yoff, axis=1) + dexp_ref[...] * xs
    zf = z_ref[...].astype(F32)
    y_ref[...] = _rms(y * (zf * _sigmoid(zf)), gssd_ref[...]).astype(BF16)


def _ssd(xbc, z, sm, smt, convw, convb, colb, cola, rowb, rowa, dexp, gssd, expand, B, S):
    L = L_SSD
    nc = S // L
    T = B * S
    row = lambda b, c: (b * nc + c, 0)
    const = lambda b, c: (0, 0)
    full = lambda a: pl.BlockSpec(a.shape, const)
    return pl.pallas_call(
        _ssd_kernel,
        grid=(B, nc),
        in_specs=[pl.BlockSpec((L, D_XBC), row),
                  pl.BlockSpec((L, D_SSD), row),
                  pl.BlockSpec((L, LANES), row),
                  pl.BlockSpec((SSD_HEADS, L), lambda b, c: (0, b * nc + c)),
                  full(convw), full(convb), full(colb), full(cola), full(rowb), full(rowa),
                  full(dexp), full(gssd), full(expand)],
        out_specs=pl.BlockSpec((L, D_SSD), row),
        out_shape=jax.ShapeDtypeStruct((T, D_SSD), BF16),
        scratch_shapes=[pltpu.VMEM((L + 8, D_XBC), F32),
                        pltpu.VMEM((SSD_GROUPS, (SSD_HEADS // SSD_GROUPS) * SSD_HEAD_DIM, SSD_STATE), F32)],
        compiler_params=pltpu.CompilerParams(dimension_semantics=("arbitrary", "arbitrary"),
                                             vmem_limit_bytes=VMEM_LIMIT),
        name="ssd",
    )(xbc, z, sm, smt, convw, convb, colb, cola, rowb, rowa, dexp, gssd, expand)


def _fox_kernel(q_ref, k_ref, vt_ref, o_ref, sa_ref, sb_ref):
    TQ = q_ref.shape[0]
    TK = TK_FOX
    assert TQ == 2 * TK
    n_pairs = pl.program_id(1)
    n_full = 2 * n_pairs
    key_idx = lax.broadcasted_iota(I32, (TK, TQ), 0)
    qry_idx = lax.broadcasted_iota(I32, (TK, TQ), 1)
    heads = range(FOX_HEADS)
    qs = [q_ref[:, h * LANES:(h + 1) * LANES] for h in heads]

    def scores(j, s_ref):
        ks = pl.multiple_of(j * TK, TK)
        for h in heads:
            s_ref[h] = lax.dot_general(k_ref[pl.ds(ks, TK), h * LANES:(h + 1) * LANES], qs[h], _NT,
                                       preferred_element_type=F32)

    def softmax_pv(j, s_ref, carry, diag):
        ks = pl.multiple_of(j * TK, TK)
        new = []
        for h in heads:
            m, acc = carry[h]
            s = s_ref[h]
            if diag is not None:
                s = jnp.where(key_idx + diag * TK <= qry_idx, s, -jnp.inf)
            mn = jnp.maximum(m, jnp.max(s, axis=0, keepdims=True))
            p = jnp.exp2(s - mn).astype(BF16)
            acc = jnp.exp2(m - mn) * acc + _dot(vt_ref[h * VT_ROWS:(h + 1) * VT_ROWS, pl.ds(ks, TK)], p)
            new.append((mn, acc))
        return tuple(new)

    def pair(t, carry):
        j = 2 * t
        scores(j + 1, sb_ref)
        carry = softmax_pv(j, sa_ref, carry, None)
        scores(j + 2, sa_ref)
        return softmax_pv(j + 1, sb_ref, carry, None)

    carry = tuple((jnp.full((1, TQ), -1e30, F32), jnp.zeros((VT_ROWS, TQ), F32)) for _ in heads)
    scores(0, sa_ref)
    carry = lax.fori_loop(0, n_pairs, pair, carry)
    scores(n_full + 1, sb_ref)
    carry = softmax_pv(n_full, sa_ref, carry, 0)
    carry = softmax_pv(n_full + 1, sb_ref, carry, 1)
    o_ref[...] = jnp.concatenate(
        [acc[0:FOX_HEAD_DIM] / acc[FOX_HEAD_DIM:FOX_HEAD_DIM + 1] for _, acc in carry], axis=0).astype(BF16)


def _fox(q, k, vt, B, S):
    TQ = TQ_FOX
    nq = S // TQ
    T = B * S
    return pl.pallas_call(
        _fox_kernel,
        grid=(B, nq),
        in_specs=[pl.BlockSpec((TQ, D_QK), lambda b, i: (b * nq + i, 0)),
                  pl.BlockSpec((S, D_QK), lambda b, i: (b, 0)),
                  pl.BlockSpec((FOX_HEADS * VT_ROWS, S), lambda b, i: (0, b))],
        out_specs=pl.BlockSpec((D_FOX, TQ), lambda b, i: (0, b * nq + i)),
        out_shape=jax.ShapeDtypeStruct((D_FOX, T), BF16),
        scratch_shapes=[pltpu.VMEM((FOX_HEADS, TK_FOX, TQ), F32), pltpu.VMEM((FOX_HEADS, TK_FOX, TQ), F32)],
        compiler_params=pltpu.CompilerParams(dimension_semantics=("arbitrary", "arbitrary"),
                                             vmem_limit_bytes=VMEM_LIMIT),
        name="fox",
    )(q, k, vt)


def _outproj_kernel(x_ref, ys_ref, yft_ref, wo1_ref, wo2_ref, g_ref, wr_ref, br_ref,
                    h1_ref, xrow_ref, route_ref, meta_ref, cnt_ref, carry_ref):
    TM = x_ref.shape[0]
    i = pl.program_id(0)

    @pl.when(i == 0)
    def _():
        carry_ref[...] = jnp.zeros(carry_ref.shape, F32)

    h1 = (x_ref[...] + _dot(ys_ref[...], wo1_ref[...])
          + lax.dot_general(yft_ref[...], wo2_ref[...], _TN, preferred_element_type=F32))
    h1_ref[...] = h1
    hn = _rms(h1, g_ref[...])
    h_hi, h_lo = _split_bf16(hn, 2)
    logits = (_dot(h_hi, wr_ref[0]) + _dot(h_lo, wr_ref[0]) + _dot(h_hi, wr_ref[1])) + br_ref[...]
    lane = lax.broadcasted_iota(I32, (TM, LANES), 1)
    ninf = -jnp.inf
    gl = jnp.where(lane < N_GROUPS, logits, ninf)
    gmax = jnp.max(gl, axis=-1, keepdims=True)
    p_grp = 1.0 / jnp.sum(jnp.exp(gl - gmax), axis=-1, keepdims=True)
    g_idx = jnp.min(jnp.where(gl == gmax, lane, LANES), axis=-1, keepdims=True)
    e0 = N_GROUPS + EXPERTS_PER_GROUP * g_idx
    el = jnp.where((lane >= e0) & (lane < e0 + EXPERTS_PER_GROUP), logits, ninf)
    v1 = jnp.max(el, axis=-1, keepdims=True)
    i1 = jnp.min(jnp.where(el == v1, lane, LANES), axis=-1, keepdims=True)
    el2 = jnp.where(lane == i1, ninf, el)
    v2 = jnp.max(el2, axis=-1, keepdims=True)
    i2 = jnp.min(jnp.where(el2 == v2, lane, LANES), axis=-1, keepdims=True)
    e2 = jnp.exp(v2 - v1)
    w1 = p_grp / (1.0 + e2)
    w2 = p_grp * e2 / (1.0 + e2)
    l1 = i1 - e0
    l2 = i2 - e0
    first = l1 < l2
    a = jnp.where(first, l1, l2)
    b = jnp.where(first, l2, l1)
    wa = jnp.where(first, w1, w2)
    wb = jnp.where(first, w2, w1)
    pair = jnp.right_shift(a * (5 - a), 1) + b - 1
    bucket = g_idx * N_PAIRS + pair

    onehot = (lane == bucket).astype(BF16)
    ri = lax.broadcasted_iota(I32, (TM, TM), 0)
    ci = lax.broadcasted_iota(I32, (TM, TM), 1)
    incl = _dot((ri >= ci).astype(BF16), onehot)
    oh = onehot.astype(F32)
    rank = jnp.sum((incl - oh + carry_ref[0:1, :]) * oh, axis=-1, keepdims=True)
    carry_ref[...] = carry_ref[...] + jnp.broadcast_to(incl[TM - 1:TM, :], carry_ref.shape)
    cnt_ref[...] = carry_ref[...]

    route = jnp.where(lane == 0, wa, jnp.where(lane == 1, wb, jnp.where(
        lane == 2, bucket.astype(F32), jnp.where(lane == 3, rank, 0.0))))
    route_ref[...] = route
    sel = (lax.broadcasted_iota(I32, (SUBLANES, LANES), 0) + 2
           == lax.broadcasted_iota(I32, (SUBLANES, LANES), 1)).astype(BF16)
    meta = None
    for piece in _split_bf16(route, 3):
        t = lax.dot_general(sel, piece, _NT, preferred_element_type=F32)
        meta = t if meta is None else meta + t
    meta_ref[...] = meta
    _to_token_tiles(xrow_ref, hn, TOK_ROWS)


def _out_proj(x2, yssd, yfoxt, wo1, wo2, g_ffn, wr, br):
    T = x2.shape[0]
    tm = TM_PROJ
    row = lambda i: (i, 0)
    const = lambda i: (0, 0)
    return pl.pallas_call(
        _outproj_kernel,
        grid=(T // tm,),
        in_specs=[pl.BlockSpec((tm, D_MODEL), row),
                  pl.BlockSpec((tm, D_SSD), row),
                  pl.BlockSpec((D_FOX, tm), lambda i: (0, i)),
                  pl.BlockSpec(wo1.shape, const),
                  pl.BlockSpec(wo2.shape, const),
                  pl.BlockSpec((1, D_MODEL), const),
                  pl.BlockSpec(wr.shape, lambda i: (0, 0, 0)),
                  pl.BlockSpec((1, LANES), const)],
        out_specs=[pl.BlockSpec((tm, D_MODEL), row),
                   pl.BlockSpec((tm * TOK_ROWS, LANES), row),
                   pl.BlockSpec((tm, LANES), row),
                   pl.BlockSpec((SUBLANES, tm), lambda i: (0, i)),
                   pl.BlockSpec((SUBLANES, LANES), const)],
        out_shape=[jax.ShapeDtypeStruct((T, D_MODEL), F32),
                   jax.ShapeDtypeStruct((T * TOK_ROWS, LANES), F32),
                   jax.ShapeDtypeStruct((T, LANES), F32),
                   jax.ShapeDtypeStruct((SUBLANES, T), F32),
                   jax.ShapeDtypeStruct((SUBLANES, LANES), F32)],
        scratch_shapes=[pltpu.VMEM((SUBLANES, LANES), F32)],
        compiler_params=pltpu.CompilerParams(dimension_semantics=("arbitrary",),
                                             vmem_limit_bytes=VMEM_LIMIT),
        name="out_proj_router",
    )(x2, yssd, yfoxt, wo1, wo2, g_ffn, wr, br)


def _dispatch_kernel(pos_ref, src_ref, dst_init_hbm, dst_hbm, sem):
    del dst_init_hbm
    TD = pos_ref.shape[-1]

    def copy(r):
        dst = pl.multiple_of(pos_ref[0, r] * TOK_ROWS, TOK_ROWS)
        src = pl.multiple_of(r * TOK_ROWS, TOK_ROWS)
        return pltpu.make_async_copy(src_ref.at[pl.ds(src, TOK_ROWS)], dst_hbm.at[pl.ds(dst, TOK_ROWS)], sem)

    def start(r2, c):
        for u in range(2):
            copy(2 * r2 + u).start(priority=u)
        return c

    lax.fori_loop(0, TD // 2, start, 0)

    def wait(r, c):
        copy(r).wait()
        return c

    lax.fori_loop(0, TD, wait, 0, unroll=DMA_WAIT_UNROLL)


def _dispatch(pos, xrow, n_rows):
    T = pos.shape[0]
    td = TD_DISP
    pos3 = pos.reshape(T // td, 1, td)
    dst0 = jnp.zeros((n_rows * TOK_ROWS, LANES), F32)
    return pl.pallas_call(
        _dispatch_kernel,
        grid=(T // td,),
        in_specs=[pl.BlockSpec((None, 1, td), lambda i: (i, 0, 0), memory_space=pltpu.SMEM),
                  pl.BlockSpec((td * TOK_ROWS, LANES), lambda i: (i, 0)),
                  pl.BlockSpec(memory_space=pl.ANY)],
        out_specs=pl.BlockSpec(memory_space=pl.ANY),
        out_shape=jax.ShapeDtypeStruct((n_rows * TOK_ROWS, LANES), F32),
        scratch_shapes=[pltpu.SemaphoreType.DMA(())],
        input_output_aliases={2: 0},
        compiler_params=pltpu.CompilerParams(dimension_semantics=("arbitrary",)),
        name="dispatch",
    )(pos3, xrow, dst0)


def _experts_kernel(ea_ref, eb_ref, blk_ref, valid_ref,
                    xs_ref, wga_ref, wua_ref, wda_ref, wgb_ref, wub_ref, wdb_ref, ys_ref):
    del ea_ref, eb_ref, blk_ref
    i = pl.program_id(0)
    TM = xs_ref.shape[0] // TOK_ROWS

    @pl.when(valid_ref[i] > 0)
    def _():
        x = _from_token_tiles(xs_ref, TM, TOK_ROWS).astype(BF16)

        def expert(wg_ref, wu_ref, wd_ref):
            g = _dot(x, wg_ref[...])
            u = _dot(x, wu_ref[...])
            return _dot((g * _sigmoid(g) * u).astype(BF16), wd_ref[...])

        _to_token_tiles(ys_ref, expert(wga_ref, wua_ref, wda_ref), 2 * TOK_ROWS)
        _to_token_tiles(ys_ref, expert(wgb_ref, wub_ref, wdb_ref), 2 * TOK_ROWS, TOK_ROWS)

    @pl.when(valid_ref[i] == 0)
    def _():
        ys_ref[...] = jnp.zeros(ys_ref.shape, F32)


def _experts(ea, eb, blk, valid, xs, wg, wu, wd):
    tm = TM_MOE
    n_rows = xs.shape[0] // TOK_ROWS
    nt = n_rows // tm
    xmap = lambda i, ea, eb, blk, valid: (blk[i], 0)
    amap = lambda i, ea, eb, blk, valid: (ea[i], 0, 0)
    bmap = lambda i, ea, eb, blk, valid: (eb[i], 0, 0)
    up_spec = lambda m: pl.BlockSpec((None, D_MODEL, D_EXPERT), m)
    dn_spec = lambda m: pl.BlockSpec((None, D_EXPERT, D_MODEL), m)
    grid_spec = pltpu.PrefetchScalarGridSpec(
        num_scalar_prefetch=4,
        grid=(nt,),
        in_specs=[pl.BlockSpec((tm * TOK_ROWS, LANES), xmap),
                  up_spec(amap), up_spec(amap), dn_spec(amap),
                  up_spec(bmap), up_spec(bmap), dn_spec(bmap)],
        out_specs=pl.BlockSpec((tm * 2 * TOK_ROWS, LANES), lambda i, ea, eb, blk, valid: (i, 0)),
    )
    return pl.pallas_call(
        _experts_kernel,
        grid_spec=grid_spec,
        out_shape=jax.ShapeDtypeStruct((n_rows * 2 * TOK_ROWS, LANES), F32),
        compiler_params=pltpu.CompilerParams(dimension_semantics=("arbitrary",),
                                             vmem_limit_bytes=VMEM_LIMIT),
        name="experts",
    )(ea, eb, blk, valid, xs, wg, wu, wd, wg, wu, wd)


def _combine_kernel(pos_cur_ref, pos_nxt_ref, h1_ref, route_ref, p_ref, wpp_ref, wpg_ref, gple_ref, gfin_ref,
                    ys_hbm, o_ref, buf_ref, sem):
    TM = h1_ref.shape[0]
    ROWS = 2 * TOK_ROWS
    i = pl.program_id(0)
    n = pl.num_programs(0)
    slot = i % 2

    def copy(pos_ref, s, r):
        src = pl.multiple_of(pos_ref[0, r] * ROWS, ROWS)
        dst = pl.multiple_of(r * ROWS, ROWS)
        return pltpu.make_async_copy(ys_hbm.at[pl.ds(src, ROWS)], buf_ref.at[s, pl.ds(dst, ROWS)], sem.at[s])

    def gather(pos_ref, s):
        def start(r2, c):
            for u in range(2):
                copy(pos_ref, s, 2 * r2 + u).start(priority=u)
            return c
        lax.fori_loop(0, TM // 2, start, 0)

    @pl.when(i == 0)
    def _():
        gather(pos_cur_ref, 0)

    @pl.when(i + 1 < n)
    def _():
        gather(pos_nxt_ref, 1 - slot)

    def wait(r, c):
        copy(pos_cur_ref, slot, r).wait()
        return c

    lax.fori_loop(0, TM, wait, 0, unroll=DMA_WAIT_UNROLL)

    cur = buf_ref.at[slot]
    ya = _from_token_tiles(cur, TM, ROWS)
    yb = _from_token_tiles(cur, TM, ROWS, TOK_ROWS)
    h2 = h1_ref[...] + route_ref[:, 0:1] * ya + route_ref[:, 1:2] * yb
    n3 = _rms(h2, gple_ref[...]).astype(BF16)
    gate = _sigmoid(_dot(n3, wpg_ref[...]))
    pp = _dot(p_ref[...].astype(BF16), wpp_ref[...])
    o_ref[...] = _rms(h2 + pp * gate, gfin_ref[...])


def _combine(pos, h1, route, p2, wpp, wpg, g_ple, g_final, ys):
    T = h1.shape[0]
    tm = TM_COMB
    nt = T // tm
    pos3 = pos.reshape(nt, 1, tm)
    row = lambda i: (i, 0)
    const = lambda i: (0, 0)
    return pl.pallas_call(
        _combine_kernel,
        grid=(nt,),
        in_specs=[pl.BlockSpec((None, 1, tm), lambda i: (i, 0, 0), memory_space=pltpu.SMEM),
                  pl.BlockSpec((None, 1, tm), lambda i: (jnp.minimum(i + 1, nt - 1), 0, 0),
                               memory_space=pltpu.SMEM),
                  pl.BlockSpec((tm, D_MODEL), row),
                  pl.BlockSpec((tm, LANES), row),
                  pl.BlockSpec((tm, D_PLE), row),
                  pl.BlockSpec(wpp.shape, const),
                  pl.BlockSpec(wpg.shape, const),
                  pl.BlockSpec((1, D_MODEL), const),
                  pl.BlockSpec((1, D_MODEL), const),
                  pl.BlockSpec(memory_space=pl.ANY)],
        out_specs=pl.BlockSpec((tm, D_MODEL), row),
        out_shape=jax.ShapeDtypeStruct((T, D_MODEL), F32),
        scratch_shapes=[pltpu.VMEM((2, tm * 2 * TOK_ROWS, LANES), F32),
                        pltpu.SemaphoreType.DMA((2,))],
        compiler_params=pltpu.CompilerParams(dimension_semantics=("arbitrary",),
                                             vmem_limit_bytes=VMEM_LIMIT),
        name="combine_ple",
    )(pos3, pos3, h1, route, p2, wpp, wpg, g_ple, g_final, ys)


def _tile_tables(counts, n_tiles):
    nt = (counts + TM_MOE - 1) // TM_MOE
    cum = jnp.cumsum(nt)
    off = (cum - nt) * TM_MOE
    total = cum[-1]
    tile = jnp.arange(n_tiles, dtype=I32)
    valid = (tile < total).astype(I32)
    blk = jnp.maximum(jnp.minimum(tile, total - 1), 0)
    tb = jnp.minimum(jnp.sum((cum[None, :] <= blk[:, None]).astype(I32), axis=1), N_BUCKETS - 1)
    pair_a = jnp.array([0, 0, 0, 1, 1, 2], I32)
    pair_b = jnp.array([1, 2, 3, 2, 3, 3], I32)
    grp = tb // N_PAIRS
    ea = grp * EXPERTS_PER_GROUP + pair_a[tb % N_PAIRS]
    eb = grp * EXPERTS_PER_GROUP + pair_b[tb % N_PAIRS]
    return off, ea, eb, blk, valid


def kernel(x, p, w_in, conv_w, conv_b, dt_bias, a_log, d_skip, g_ssd, fox_fbias, w_out, g_mix, g_ffn,
           w_route_group, b_route_group, w_route_expert, b_route_expert, w_exp_gate, w_exp_up, w_exp_down,
           g_ple, w_ple_proj, w_ple_gate, g_final):
    B, S, _ = x.shape
    T = B * S
    assert S % L_SSD == 0 and S % TQ_FOX == 0 and S % TM_PROJ == 0 and T % TD_DISP == 0
    x2 = x.reshape(T, D_MODEL)
    p2 = p[0].reshape(T, D_PLE)

    wi = w_in[0]
    o_xbc = D_SSD
    o_dt = o_xbc + D_XBC
    o_q = o_dt + SSD_HEADS
    o_k = o_q + D_FOX
    o_v = o_k + D_FOX
    o_f = o_v + D_FOX
    wz = wi[:, 0:o_xbc].astype(BF16)
    wxbc = wi[:, o_xbc:o_dt].astype(BF16)
    wq = wi[:, o_q:o_k].astype(BF16)
    wk = wi[:, o_k:o_v].astype(BF16)
    wvt = wi[:, o_v:o_f].T.astype(BF16)
    w_dt = wi[:, o_dt:o_q]
    w_f = wi[:, o_f:o_f + FOX_HEADS]
    wsm = jnp.pad(jnp.concatenate([w_dt] + [w_f] * F_PARTS, axis=1),
                  ((0, 0), (0, LANES - SSD_HEADS - F_PARTS * FOX_HEADS))).astype(BF16)
    wsmt = w_dt.T.astype(BF16)
    lane_pad = lambda v: jnp.pad(v.reshape(1, -1), ((0, 0), (0, LANES - v.shape[-1])))
    colb = lane_pad(jnp.concatenate([dt_bias[0]] + [fox_fbias[0]] * F_PARTS))
    cola = lane_pad(a_log[0])
    rowb = jnp.broadcast_to(dt_bias[0][:, None], (SSD_HEADS, LANES))
    rowa = jnp.broadcast_to(a_log[0][:, None], (SSD_HEADS, LANES))
    dexp = jnp.repeat(d_skip[0], SSD_HEAD_DIM).reshape(1, D_SSD)
    expand = (jnp.arange(LANES)[:, None] == (jnp.arange(D_SSD) // SSD_HEAD_DIM)[None, :]).astype(BF16)
    col = jnp.arange(D_QK)
    head = col // LANES
    j_bias = col % LANES - jnp.where(head % 2 == 0, FOX_HEAD_DIM, 0)
    bias_col = (j_bias >= 0) & (j_bias < F_PARTS)
    qones = bias_col.astype(F32).reshape(1, D_QK)
    src_lane = SSD_HEADS + j_bias * FOX_HEADS + head
    place = (bias_col[None, :] & (jnp.arange(LANES)[:, None] == src_lane[None, :])).astype(BF16)
    wo = w_out[0].astype(BF16)
    wr = jnp.pad(jnp.concatenate([w_route_group[0], w_route_expert[0]], axis=1),
                 ((0, 0), (0, LANES - N_GROUPS - N_EXPERTS)))
    wr_hi = wr.astype(BF16)
    wr = jnp.stack([wr_hi, (wr - wr_hi.astype(F32)).astype(BF16)])
    br = lane_pad(jnp.concatenate([b_route_group[0], b_route_expert[0]]))
    wg = w_exp_gate[0].reshape(N_EXPERTS, D_MODEL, D_EXPERT).astype(BF16)
    wu = w_exp_up[0].reshape(N_EXPERTS, D_MODEL, D_EXPERT).astype(BF16)
    wd = w_exp_down[0].reshape(N_EXPERTS, D_EXPERT, D_MODEL).astype(BF16)

    z, xbc, q, k, vt, sm, smt = _in_proj(x2, g_mix[0].reshape(1, -1), wz, wxbc, wq, wk, wvt, wsm, wsmt,
                                         colb, qones, place, S)
    yssd = _ssd(xbc, z, sm, smt, conv_w[0].reshape(SSD_CONV, D_XBC), conv_b[0].reshape(1, -1),
                colb, cola, rowb, rowa, dexp, g_ssd[0].reshape(1, -1), expand, B, S)
    yfoxt = _fox(q, k, vt, B, S)
    h1, xrow, route, meta, cnt = _out_proj(x2, yssd, yfoxt, wo[0:D_SSD], wo[D_SSD:], g_ffn[0].reshape(1, -1), wr, br)

    n_rows = T + N_BUCKETS * TM_MOE
    bucket = meta[0].astype(I32)
    rank = meta[1].astype(I32)
    off, ea, eb, blk, valid = _tile_tables(cnt[0, 0:N_BUCKETS].astype(I32), n_rows // TM_MOE)
    pos = rank
    for b in range(N_BUCKETS):
        pos = pos + jnp.where(bucket == b, off[b], 0)

    xs = _dispatch(pos, xrow, n_rows)
    ys = _experts(ea, eb, blk, valid, xs, wg, wu, wd)
    out = _combine(pos, h1, route, p2, w_ple_proj[0].astype(BF16), w_ple_gate[0].astype(BF16),
                   g_ple[0].reshape(1, -1), g_final.reshape(1, -1), ys)
    return out.reshape(B, S, D_MODEL)
```

```python
import functools

import jax
import jax.numpy as jnp
from jax import lax
from jax.experimental import pallas as pl
from jax.experimental.pallas import tpu as pltpu

F32 = jnp.float32
BF16 = jnp.bfloat16
I32 = jnp.int32

D_MODEL = 1024
SSD_HEADS = 8
SSD_HEAD_DIM = 64
SSD_GROUPS = 2
SSD_STATE = 128
SSD_CONV = 4
D_SSD = SSD_HEADS * SSD_HEAD_DIM
D_XBC = D_SSD + 2 * SSD_GROUPS * SSD_STATE
FOX_HEADS = 8
FOX_HEAD_DIM = 64
D_FOX = FOX_HEADS * FOX_HEAD_DIM
N_GROUPS = 4
EXPERTS_PER_GROUP = 4
N_EXPERTS = N_GROUPS * EXPERTS_PER_GROUP
N_PAIRS = 6
N_BUCKETS = N_GROUPS * N_PAIRS
D_EXPERT = 512
D_PLE = 256
EPS = 1e-6
LOG2E = 1.4426950408889634
LANES = 128
SUBLANES = 8
TOK_ROWS = D_MODEL // LANES
TOK_PITCH = 12
D_QK = FOX_HEADS * LANES
F_PARTS = 3
VT_ROWS = FOX_HEAD_DIM + 16
VMEM_LIMIT = 52 * 1024 * 1024

TM_PROJ = 512
L_SSD = 256
TQ_FOX = 512
TK_FOX = 256
TM_MOE = 256
TD_DISP = 1024
TM_COMB = 256
DMA_WAIT_UNROLL = 16

_NT = (((1,), (1,)), ((), ()))
_TN = (((0,), (0,)), ((), ()))


def _rms(x, g):
    ms = jnp.mean(x * x, axis=-1, keepdims=True)
    return x * lax.rsqrt(ms + EPS) * g


def _sigmoid(x):
    return 1.0 / (1.0 + jnp.exp(-x))


def _softplus(x):
    return jnp.maximum(x, 0.0) + jnp.log(1.0 + jnp.exp(-jnp.abs(x)))


def _split_bf16(x, parts):
    out = []
    r = x
    for _ in range(parts):
        h = r.astype(BF16)
        out.append(h)
        r = r - h.astype(F32)
    return out


def _dot(a, b):
    return jnp.dot(a, b, preferred_element_type=F32)


def _dot_exact(a01, x, parts):
    acc = None
    for piece in _split_bf16(x, parts):
        t = _dot(a01, piece)
        acc = t if acc is None else acc + t
    return acc


def _to_token_tiles(ref, x, stride, offset=0):
    m = x.shape[0]
    for s in range(TOK_ROWS):
        ref[pl.ds(offset + s, m, stride=stride), :] = x[:, s * LANES:(s + 1) * LANES]


def _from_token_tiles(ref, m, stride, offset=0):
    return jnp.concatenate([ref[pl.ds(offset + s, m, stride=stride), :] for s in range(TOK_ROWS)], axis=1)


def _inproj_kernel(x_ref, g_ref, wz_ref, wxbc_ref, wq_ref, wk_ref, wvt_ref, wsm_ref, wsmt_ref,
                   colb_ref, qones_ref, place_ref,
                   z_ref, xbc_ref, q_ref, k_ref, vt_ref, sm_ref, smt_ref, fcar_ref, *, tiles_per_seq):
    TM = x_ref.shape[0]

    @pl.when(pl.program_id(0) % tiles_per_seq == 0)
    def _():
        fcar_ref[...] = jnp.zeros(fcar_ref.shape, F32)

    hn = _rms(x_ref[...], g_ref[...]).astype(BF16)
    z_ref[...] = _dot(hn, wz_ref[...]).astype(BF16)
    xbc_ref[...] = _dot(hn, wxbc_ref[...]).astype(BF16)
    vt = lax.dot_general(wvt_ref[...], hn, _NT, preferred_element_type=F32).astype(BF16)
    ones_rows = jnp.ones((VT_ROWS - FOX_HEAD_DIM, TM), BF16)
    for h in range(FOX_HEADS):
        vt_ref[h * VT_ROWS:h * VT_ROWS + FOX_HEAD_DIM, :] = vt[h * FOX_HEAD_DIM:(h + 1) * FOX_HEAD_DIM, :]
        vt_ref[h * VT_ROWS + FOX_HEAD_DIM:(h + 1) * VT_ROWS, :] = ones_rows
    sm = _dot(hn, wsm_ref[...])
    sm_ref[...] = sm
    smt_ref[...] = lax.dot_general(wsmt_ref[...], hn, _NT, preferred_element_type=F32)

    raw = sm + colb_ref[...]
    logf = -_softplus(-raw)
    ri = lax.broadcasted_iota(I32, (TM, TM), 0)
    ci = lax.broadcasted_iota(I32, (TM, TM), 1)
    fcum = _dot_exact((ri >= ci).astype(BF16), logf, 3) + fcar_ref[0:1, :]
    fcar_ref[...] = jnp.broadcast_to(fcum[TM - 1:TM, :], fcar_ref.shape)
    fs = fcum * (-LOG2E)
    hi = fs.astype(BF16)
    r1 = fs - hi.astype(F32)
    mid = r1.astype(BF16)
    lo = (r1 - mid.astype(F32)).astype(BF16)
    lane = lax.broadcasted_iota(I32, (TM, LANES), 1)
    pieces = jnp.where(lane < 16, hi, jnp.where(lane < 24, mid, lo))
    kbias = _dot(pieces, place_ref[...])
    qc = _dot(hn, wq_ref[...]) * (FOX_HEAD_DIM ** -0.5 * LOG2E)
    kc = _dot(hn, wk_ref[...])
    for h in range(FOX_HEADS):
        pair = slice((h // 2) * LANES, (h // 2 + 1) * LANES)
        tile = slice(h * LANES, (h + 1) * LANES)
        own = (lane < FOX_HEAD_DIM) if h % 2 == 0 else (lane >= FOX_HEAD_DIM)
        q_ref[:, tile] = jnp.where(own, qc[:, pair], qones_ref[:, tile]).astype(BF16)
        k_ref[:, tile] = jnp.where(own, kc[:, pair], kbias[:, tile]).astype(BF16)


def _in_proj(x2, g_mix, wz, wxbc, wq, wk, wvt, wsm, wsmt, colb, qones, place, S):
    T = x2.shape[0]
    tm = TM_PROJ
    row = lambda i: (i, 0)
    const = lambda i: (0, 0)
    full = lambda a: pl.BlockSpec(a.shape, const)
    return pl.pallas_call(
        functools.partial(_inproj_kernel, tiles_per_seq=S // tm),
        grid=(T // tm,),
        in_specs=[pl.BlockSpec((tm, D_MODEL), row), full(g_mix), full(wz), full(wxbc), full(wq), full(wk),
                  full(wvt), full(wsm), full(wsmt), full(colb), full(qones), full(place)],
        out_specs=[pl.BlockSpec((tm, D_SSD), row),
                   pl.BlockSpec((tm, D_XBC), row),
                   pl.BlockSpec((tm, D_QK), row),
                   pl.BlockSpec((tm, D_QK), row),
                   pl.BlockSpec((FOX_HEADS * VT_ROWS, tm), lambda i: (0, i)),
                   pl.BlockSpec((tm, LANES), row),
                   pl.BlockSpec((SSD_HEADS, tm), lambda i: (0, i))],
        out_shape=[jax.ShapeDtypeStruct((T, D_SSD), BF16),
                   jax.ShapeDtypeStruct((T, D_XBC), BF16),
                   jax.ShapeDtypeStruct((T, D_QK), BF16),
                   jax.ShapeDtypeStruct((T, D_QK), BF16),
                   jax.ShapeDtypeStruct((FOX_HEADS * VT_ROWS, T), BF16),
                   jax.ShapeDtypeStruct((T, LANES), F32),
                   jax.ShapeDtypeStruct((SSD_HEADS, T), F32)],
        scratch_shapes=[pltpu.VMEM((SUBLANES, LANES), F32)],
        compiler_params=pltpu.CompilerParams(dimension_semantics=("arbitrary",),
                                             vmem_limit_bytes=VMEM_LIMIT),
        name="in_proj",
    )(x2, g_mix, wz, wxbc, wq, wk, wvt, wsm, wsmt, colb, qones, place)


def _ssd_kernel(xbc_ref, z_ref, sm_ref, smt_ref, convw_ref, convb_ref, colb_ref, cola_ref,
                rowb_ref, rowa_ref, dexp_ref, gssd_ref, expand_ref,
                y_ref, ext_ref, st_ref):
    L = xbc_ref.shape[0]
    R = SSD_HEADS // SSD_GROUPS
    c = pl.program_id(1)

    @pl.when(c == 0)
    def _():
        ext_ref[0:8, :] = jnp.zeros((8, D_XBC), F32)
        st_ref[...] = jnp.zeros(st_ref.shape, F32)

    ext_ref[8:8 + L, :] = xbc_ref[...].astype(F32)
    conv = convb_ref[...] + convw_ref[0:1, :] * ext_ref[5:5 + L, :]
    for kk in range(1, SSD_CONV):
        conv = conv + convw_ref[kk:kk + 1, :] * ext_ref[5 + kk:5 + kk + L, :]
    ext_ref[0:8, :] = ext_ref[L:L + 8, :]
    xc = conv * _sigmoid(conv)
    xs = xc[:, 0:D_SSD]
    xs_bf = xs.astype(BF16)
    bm = xc[:, D_SSD:D_SSD + SSD_GROUPS * SSD_STATE].astype(BF16)
    cm = xc[:, D_SSD + SSD_GROUPS * SSD_STATE:].astype(BF16)

    ri = lax.broadcasted_iota(I32, (L, L), 0)
    ci = lax.broadcasted_iota(I32, (L, L), 1)
    causal = ri >= ci
    tril = causal.astype(BF16)
    triu = (ri <= ci).astype(BF16)

    dtr = _softplus(smt_ref[...] + rowb_ref[:, 0:1])
    adt_r = dtr * (-jnp.exp(rowa_ref[:, 0:1]))
    csr = None
    for part in _split_bf16(adt_r, 3):
        t = _dot(part, triu)
        csr = t if csr is None else csr + t

    dtc = _softplus(sm_ref[...] + colb_ref[...])
    adt_c = dtc * (-jnp.exp(cola_ref[...]))
    cs_c = _dot_exact(tril, adt_c, 3)
    cs_last = cs_c[L - 1:L, :]
    e1 = jnp.exp(cs_c)
    wst = dtc * jnp.exp(cs_last - cs_c)
    ex = expand_ref[...]
    p1 = _split_bf16(e1, 2)
    e1x = _dot(p1[0], ex) + _dot(p1[1], ex)
    p2 = _split_bf16(wst, 2)
    wstx = _dot(p2[0], ex) + _dot(p2[1], ex)

    xw = (xs * wstx).astype(BF16)
    dec8 = jnp.broadcast_to(jnp.exp(csr[:, L - 1:L]), (SSD_HEADS, SSD_STATE))

    ydiag = []
    yoff = []
    for g in range(SSD_GROUPS):
        bg = bm[:, g * SSD_STATE:(g + 1) * SSD_STATE]
        cg = cm[:, g * SSD_STATE:(g + 1) * SSD_STATE]
        gmat = lax.dot_general(cg, bg, _NT, preferred_element_type=F32)
        s_old = st_ref[g]
        yoff.append(lax.dot_general(cg, s_old.astype(BF16), _NT, preferred_element_type=F32))
        for j in range(R):
            h = g * R + j
            seg = cs_c[:, h:h + 1] - csr[h:h + 1, :]
            lm = jnp.exp(jnp.where(causal, seg, -jnp.inf))
            m = (gmat * lm * dtr[h:h + 1, :]).astype(BF16)
            ydiag.append(_dot(m, xs_bf[:, h * SSD_HEAD_DIM:(h + 1) * SSD_HEAD_DIM]))
        upd = lax.dot_general(xw[:, g * R * SSD_HEAD_DIM:(g + 1) * R * SSD_HEAD_DIM], bg, _TN,
                              preferred_element_type=F32)
        dec = jnp.concatenate(
            [jnp.broadcast_to(dec8[g * R + j:g * R + j + 1, :], (SSD_HEAD_DIM, SSD_STATE)) for j in range(R)],
            axis=0)
        st_ref[g] = dec * s_old + upd

    y = jnp.concatenate(ydiag, axis=1) + e1x * jnp.concatenate(yoff, axis=1) + dexp_ref[...] * xs
    zf = z_ref[...].astype(F32)
    y_ref[...] = _rms(y * (zf * _sigmoid(zf)), gssd_ref[...]).astype(BF16)


def _ssd(xbc, z, sm, smt, convw, convb, colb, cola, rowb, rowa, dexp, gssd, expand, B, S):
    L = L_SSD
    nc = S // L
    T = B * S
    row = lambda b, c: (b * nc + c, 0)
    const = lambda b, c: (0, 0)
    full = lambda a: pl.BlockSpec(a.shape, const)
    return pl.pallas_call(
        _ssd_kernel,
        grid=(B, nc),
        in_specs=[pl.BlockSpec((L, D_XBC), row),
                  pl.BlockSpec((L, D_SSD), row),
                  pl.BlockSpec((L, LANES), row),
                  pl.BlockSpec((SSD_HEADS, L), lambda b, c: (0, b * nc + c)),
                  full(convw), full(convb), full(colb), full(cola), full(rowb), full(rowa),
                  full(dexp), full(gssd), full(expand)],
        out_specs=pl.BlockSpec((L, D_SSD), row),
        out_shape=jax.ShapeDtypeStruct((T, D_SSD), BF16),
        scratch_shapes=[pltpu.VMEM((L + 8, D_XBC), F32),
                        pltpu.VMEM((SSD_GROUPS, (SSD_HEADS // SSD_GROUPS) * SSD_HEAD_DIM, SSD_STATE), F32)],
        compiler_params=pltpu.CompilerParams(dimension_semantics=("arbitrary", "arbitrary"),
                                             vmem_limit_bytes=VMEM_LIMIT),
        name="ssd",
    )(xbc, z, sm, smt, convw, convb, colb, cola, rowb, rowa, dexp, gssd, expand)


def _fox_kernel(q_ref, k_ref, vt_ref, o_ref, sa_ref, sb_ref):
    TQ = q_ref.shape[0]
    TK = TK_FOX
    assert TQ == 2 * TK
    n_pairs = pl.program_id(1)
    n_full = 2 * n_pairs
    key_idx = lax.broadcasted_iota(I32, (TK, TQ), 0)
    qry_idx = lax.broadcasted_iota(I32, (TK, TQ), 1)
    heads = range(FOX_HEADS)
    qs = [q_ref[:, h * LANES:(h + 1) * LANES] for h in heads]

    def scores(j, s_ref):
        ks = pl.multiple_of(j * TK, TK)
        for h in heads:
            s_ref[h] = lax.dot_general(k_ref[pl.ds(ks, TK), h * LANES:(h + 1) * LANES], qs[h], _NT,
                                       preferred_element_type=F32)

    def softmax_pv(j, s_ref, carry, diag):
        ks = pl.multiple_of(j * TK, TK)
        new = []
        for h in heads:
            m, acc = carry[h]
            s = s_ref[h]
            if diag is not None:
                s = jnp.where(key_idx + diag * TK <= qry_idx, s, -jnp.inf)
            mn = jnp.maximum(m, jnp.max(s, axis=0, keepdims=True))
            p = jnp.exp2(s - mn).astype(BF16)
            acc = jnp.exp2(m - mn) * acc + _dot(vt_ref[h * VT_ROWS:(h + 1) * VT_ROWS, pl.ds(ks, TK)], p)
            new.append((mn, acc))
        return tuple(new)

    def pair(t, carry):
        j = 2 * t
        scores(j + 1, sb_ref)
        carry = softmax_pv(j, sa_ref, carry, None)
        scores(j + 2, sa_ref)
        return softmax_pv(j + 1, sb_ref, carry, None)

    carry = tuple((jnp.full((1, TQ), -1e30, F32), jnp.zeros((VT_ROWS, TQ), F32)) for _ in heads)
    scores(0, sa_ref)
    carry = lax.fori_loop(0, n_pairs, pair, carry)
    scores(n_full + 1, sb_ref)
    carry = softmax_pv(n_full, sa_ref, carry, 0)
    carry = softmax_pv(n_full + 1, sb_ref, carry, 1)
    o_ref[...] = jnp.concatenate(
        [acc[0:FOX_HEAD_DIM] / acc[FOX_HEAD_DIM:FOX_HEAD_DIM + 1] for _, acc in carry], axis=0).astype(BF16)


def _fox(q, k, vt, B, S):
    TQ = TQ_FOX
    nq = S // TQ
    T = B * S
    return pl.pallas_call(
        _fox_kernel,
        grid=(B, nq),
        in_specs=[pl.BlockSpec((TQ, D_QK), lambda b, i: (b * nq + i, 0)),
                  pl.BlockSpec((S, D_QK), lambda b, i: (b, 0)),
                  pl.BlockSpec((FOX_HEADS * VT_ROWS, S), lambda b, i: (0, b))],
        out_specs=pl.BlockSpec((D_FOX, TQ), lambda b, i: (0, b * nq + i)),
        out_shape=jax.ShapeDtypeStruct((D_FOX, T), BF16),
        scratch_shapes=[pltpu.VMEM((FOX_HEADS, TK_FOX, TQ), F32), pltpu.VMEM((FOX_HEADS, TK_FOX, TQ), F32)],
        compiler_params=pltpu.CompilerParams(dimension_semantics=("arbitrary", "arbitrary"),
                                             vmem_limit_bytes=VMEM_LIMIT),
        name="fox",
    )(q, k, vt)


def _outproj_kernel(x_ref, ys_ref, yft_ref, wo1_ref, wo2_ref, g_ref, wr_ref, br_ref,
                    h1_ref, xrow_ref, meta_ref, cnt_ref, carry_ref):
    TM = x_ref.shape[0]
    i = pl.program_id(0)

    @pl.when(i == 0)
    def _():
        carry_ref[...] = jnp.zeros(carry_ref.shape, F32)

    h1 = (x_ref[...] + _dot(ys_ref[...], wo1_ref[...])
          + lax.dot_general(yft_ref[...], wo2_ref[...], _TN, preferred_element_type=F32))
    h1_ref[...] = h1
    hn = _rms(h1, g_ref[...])
    h_hi, h_lo = _split_bf16(hn, 2)
    logits = (_dot(h_hi, wr_ref[0]) + _dot(h_lo, wr_ref[0]) + _dot(h_hi, wr_ref[1])) + br_ref[...]
    lane = lax.broadcasted_iota(I32, (TM, LANES), 1)
    ninf = -jnp.inf
    gl = jnp.where(lane < N_GROUPS, logits, ninf)
    gmax = jnp.max(gl, axis=-1, keepdims=True)
    p_grp = 1.0 / jnp.sum(jnp.exp(gl - gmax), axis=-1, keepdims=True)
    g_idx = jnp.min(jnp.where(gl == gmax, lane, LANES), axis=-1, keepdims=True)
    e0 = N_GROUPS + EXPERTS_PER_GROUP * g_idx
    el = jnp.where((lane >= e0) & (lane < e0 + EXPERTS_PER_GROUP), logits, ninf)
    v1 = jnp.max(el, axis=-1, keepdims=True)
    i1 = jnp.min(jnp.where(el == v1, lane, LANES), axis=-1, keepdims=True)
    el2 = jnp.where(lane == i1, ninf, el)
    v2 = jnp.max(el2, axis=-1, keepdims=True)
    i2 = jnp.min(jnp.where(el2 == v2, lane, LANES), axis=-1, keepdims=True)
    e2 = jnp.exp(v2 - v1)
    w1 = p_grp / (1.0 + e2)
    w2 = p_grp * e2 / (1.0 + e2)
    l1 = i1 - e0
    l2 = i2 - e0
    first = l1 < l2
    a = jnp.where(first, l1, l2)
    b = jnp.where(first, l2, l1)
    wa = jnp.where(first, w1, w2)
    wb = jnp.where(first, w2, w1)
    pair = jnp.right_shift(a * (5 - a), 1) + b - 1
    bucket = g_idx * N_PAIRS + pair

    onehot = (lane == bucket).astype(BF16)
    ri = lax.broadcasted_iota(I32, (TM, TM), 0)
    ci = lax.broadcasted_iota(I32, (TM, TM), 1)
    incl = _dot((ri >= ci).astype(BF16), onehot)
    oh = onehot.astype(F32)
    rank = jnp.sum((incl - oh + carry_ref[0:1, :]) * oh, axis=-1, keepdims=True)
    carry_ref[...] = carry_ref[...] + jnp.broadcast_to(incl[TM - 1:TM, :], carry_ref.shape)
    cnt_ref[...] = carry_ref[...]

    route = jnp.where(lane == 0, wa, jnp.where(lane == 1, wb, jnp.where(
        lane == 2, bucket.astype(F32), jnp.where(lane == 3, rank, 0.0))))
    sel = (lax.broadcasted_iota(I32, (SUBLANES, LANES), 0) + 2
           == lax.broadcasted_iota(I32, (SUBLANES, LANES), 1)).astype(BF16)
    meta = None
    for piece in _split_bf16(route, 3):
        t = lax.dot_general(sel, piece, _NT, preferred_element_type=F32)
        meta = t if meta is None else meta + t
    meta_ref[...] = meta
    _to_token_tiles(xrow_ref, hn, TOK_PITCH)
    xrow_ref[pl.ds(TOK_ROWS, TM, stride=TOK_PITCH), :] = jnp.where(lane < 2, route, 0.0)
    for s in range(TOK_ROWS + 1, TOK_PITCH):
        xrow_ref[pl.ds(s, TM, stride=TOK_PITCH), :] = jnp.zeros((TM, LANES), F32)


def _out_proj(x2, yssd, yfoxt, wo1, wo2, g_ffn, wr, br):
    T = x2.shape[0]
    tm = TM_PROJ
    row = lambda i: (i, 0)
    const = lambda i: (0, 0)
    return pl.pallas_call(
        _outproj_kernel,
        grid=(T // tm,),
        in_specs=[pl.BlockSpec((tm, D_MODEL), row),
                  pl.BlockSpec((tm, D_SSD), row),
                  pl.BlockSpec((D_FOX, tm), lambda i: (0, i)),
                  pl.BlockSpec(wo1.shape, const),
                  pl.BlockSpec(wo2.shape, const),
                  pl.BlockSpec((1, D_MODEL), const),
                  pl.BlockSpec(wr.shape, lambda i: (0, 0, 0)),
                  pl.BlockSpec((1, LANES), const)],
        out_specs=[pl.BlockSpec((tm, D_MODEL), row),
                   pl.BlockSpec((tm * TOK_PITCH, LANES), row),
                   pl.BlockSpec((SUBLANES, tm), lambda i: (0, i)),
                   pl.BlockSpec((SUBLANES, LANES), const)],
        out_shape=[jax.ShapeDtypeStruct((T, D_MODEL), F32),
                   jax.ShapeDtypeStruct((T * TOK_PITCH, LANES), F32),
                   jax.ShapeDtypeStruct((SUBLANES, T), F32),
                   jax.ShapeDtypeStruct((SUBLANES, LANES), F32)],
        scratch_shapes=[pltpu.VMEM((SUBLANES, LANES), F32)],
        compiler_params=pltpu.CompilerParams(dimension_semantics=("arbitrary",),
                                             vmem_limit_bytes=VMEM_LIMIT),
        name="out_proj_router",
    )(x2, yssd, yfoxt, wo1, wo2, g_ffn, wr, br)


def _dispatch_kernel(ztile_ref, pos_ref, src_ref, dst_hbm, zero_ref, sem, zsem):
    TD = pos_ref.shape[-1]
    tile_rows = TM_MOE * TOK_PITCH

    @pl.when(pl.program_id(0) == 0)
    def _():
        zero_ref[...] = jnp.zeros(zero_ref.shape, F32)

        def zcopy(e):
            dst = pl.multiple_of(ztile_ref[e] * tile_rows, tile_rows)
            return pltpu.make_async_copy(zero_ref, dst_hbm.at[pl.ds(dst, tile_rows)], zsem)

        def zstart(e, c):
            @pl.when(ztile_ref[e] >= 0)
            def _():
                zcopy(e).start()
            return c

        def zwait(e, c):
            @pl.when(ztile_ref[e] >= 0)
            def _():
                zcopy(e).wait()
            return c

        lax.fori_loop(0, ztile_ref.shape[0], zstart, 0)
        lax.fori_loop(0, ztile_ref.shape[0], zwait, 0)

    def copy(r):
        dst = pl.multiple_of(pos_ref[0, r] * TOK_PITCH, 4)
        src = pl.multiple_of(r * TOK_PITCH, 4)
        return pltpu.make_async_copy(src_ref.at[pl.ds(src, TOK_PITCH)], dst_hbm.at[pl.ds(dst, TOK_PITCH)], sem)

    def start(r2, c):
        for u in range(2):
            copy(2 * r2 + u).start(priority=u)
        return c

    lax.fori_loop(0, TD // 2, start, 0)

    def wait(r, c):
        copy(r).wait()
        return c

    lax.fori_loop(0, TD, wait, 0, unroll=DMA_WAIT_UNROLL)


def _dispatch(ztiles, pos, xrow, n_rows):
    T = pos.shape[0]
    td = TD_DISP
    pos3 = pos.reshape(T // td, 1, td)
    grid_spec = pltpu.PrefetchScalarGridSpec(
        num_scalar_prefetch=1,
        grid=(T // td,),
        in_specs=[pl.BlockSpec((None, 1, td), lambda i, zt: (i, 0, 0), memory_space=pltpu.SMEM),
                  pl.BlockSpec((td * TOK_PITCH, LANES), lambda i, zt: (i, 0))],
        out_specs=pl.BlockSpec(memory_space=pl.ANY),
        scratch_shapes=[pltpu.VMEM((TM_MOE * TOK_PITCH, LANES), F32),
                        pltpu.SemaphoreType.DMA(()), pltpu.SemaphoreType.DMA(())],
    )
    return pl.pallas_call(
        _dispatch_kernel,
        grid_spec=grid_spec,
        out_shape=jax.ShapeDtypeStruct((n_rows * TOK_PITCH, LANES), F32),
        compiler_params=pltpu.CompilerParams(dimension_semantics=("arbitrary",),
                                             vmem_limit_bytes=VMEM_LIMIT),
        name="dispatch",
    )(ztiles, pos3, xrow)


def _experts_kernel(ea_ref, eb_ref, blk_ref, valid_ref,
                    xs_ref, wga_ref, wua_ref, wda_ref, wgb_ref, wub_ref, wdb_ref, ys_ref):
    del ea_ref, eb_ref, blk_ref
    i = pl.program_id(0)
    TM = xs_ref.shape[0] // TOK_PITCH

    @pl.when(valid_ref[i] > 0)
    def _():
        x = _from_token_tiles(xs_ref, TM, TOK_PITCH).astype(BF16)
        w = xs_ref[pl.ds(TOK_ROWS, TM, stride=TOK_PITCH), :]

        def expert(wg_ref, wu_ref, wd_ref, wt):
            g = _dot(x, wg_ref[...])
            u = _dot(x, wu_ref[...])
            return _dot((g * _sigmoid(g) * u * wt).astype(BF16), wd_ref[...])

        y = expert(wga_ref, wua_ref, wda_ref, w[:, 0:1]) + expert(wgb_ref, wub_ref, wdb_ref, w[:, 1:2])
        _to_token_tiles(ys_ref, y, TOK_ROWS)

    @pl.when(valid_ref[i] == 0)
    def _():
        ys_ref[...] = jnp.zeros(ys_ref.shape, F32)


def _experts(ea, eb, blk, valid, xs, wg, wu, wd):
    tm = TM_MOE
    n_rows = xs.shape[0] // TOK_PITCH
    nt = n_rows // tm
    xmap = lambda i, ea, eb, blk, valid: (blk[i], 0)
    amap = lambda i, ea, eb, blk, valid: (ea[i], 0, 0)
    bmap = lambda i, ea, eb, blk, valid: (eb[i], 0, 0)
    up_spec = lambda m: pl.BlockSpec((None, D_MODEL, D_EXPERT), m)
    dn_spec = lambda m: pl.BlockSpec((None, D_EXPERT, D_MODEL), m)
    grid_spec = pltpu.PrefetchScalarGridSpec(
        num_scalar_prefetch=4,
        grid=(nt,),
        in_specs=[pl.BlockSpec((tm * TOK_PITCH, LANES), xmap),
                  up_spec(amap), up_spec(amap), dn_spec(amap),
                  up_spec(bmap), up_spec(bmap), dn_spec(bmap)],
        out_specs=pl.BlockSpec((tm * TOK_ROWS, LANES), lambda i, ea, eb, blk, valid: (i, 0)),
    )
    return pl.pallas_call(
        _experts_kernel,
        grid_spec=grid_spec,
        out_shape=jax.ShapeDtypeStruct((n_rows * TOK_ROWS, LANES), F32),
        compiler_params=pltpu.CompilerParams(dimension_semantics=("arbitrary",),
                                             vmem_limit_bytes=VMEM_LIMIT),
        name="experts",
    )(ea, eb, blk, valid, xs, wg, wu, wd, wg, wu, wd)


def _combine_kernel(pos_cur_ref, pos_nxt_ref, h1_ref, p_ref, wpp_ref, wpg_ref, gple_ref, gfin_ref,
                    ys_hbm, o_ref, buf_ref, sem):
    TM = h1_ref.shape[0]
    i = pl.program_id(0)
    n = pl.num_programs(0)
    slot = i % 2

    def copy(pos_ref, s, r):
        src = pl.multiple_of(pos_ref[0, r] * TOK_ROWS, TOK_ROWS)
        dst = pl.multiple_of(r * TOK_PITCH, 4)
        return pltpu.make_async_copy(ys_hbm.at[pl.ds(src, TOK_ROWS)], buf_ref.at[s, pl.ds(dst, TOK_ROWS)],
                                     sem.at[s])

    def gather(pos_ref, s):
        def start(r2, c):
            for u in range(2):
                copy(pos_ref, s, 2 * r2 + u).start(priority=u)
            return c
        lax.fori_loop(0, TM // 2, start, 0)

    @pl.when(i == 0)
    def _():
        gather(pos_cur_ref, 0)

    @pl.when(i + 1 < n)
    def _():
        gather(pos_nxt_ref, 1 - slot)

    def wait(r, c):
        copy(pos_cur_ref, slot, r).wait()
        return c

    lax.fori_loop(0, TM, wait, 0, unroll=DMA_WAIT_UNROLL)

    h2 = h1_ref[...] + _from_token_tiles(buf_ref.at[slot], TM, TOK_PITCH)
    n3 = _rms(h2, gple_ref[...]).astype(BF16)
    gate = _sigmoid(_dot(n3, wpg_ref[...]))
    pp = _dot(p_ref[...].astype(BF16), wpp_ref[...])
    o_ref[...] = _rms(h2 + pp * gate, gfin_ref[...])


def _combine(pos, h1, p2, wpp, wpg, g_ple, g_final, ys):
    T = h1.shape[0]
    tm = TM_COMB
    nt = T // tm
    pos3 = pos.reshape(nt, 1, tm)
    row = lambda i: (i, 0)
    const = lambda i: (0, 0)
    return pl.pallas_call(
        _combine_kernel,
        grid=(nt,),
        in_specs=[pl.BlockSpec((None, 1, tm), lambda i: (i, 0, 0), memory_space=pltpu.SMEM),
                  pl.BlockSpec((None, 1, tm), lambda i: (jnp.minimum(i + 1, nt - 1), 0, 0),
                               memory_space=pltpu.SMEM),
                  pl.BlockSpec((tm, D_MODEL), row),
                  pl.BlockSpec((tm, D_PLE), row),
                  pl.BlockSpec(wpp.shape, const),
                  pl.BlockSpec(wpg.shape, const),
                  pl.BlockSpec((1, D_MODEL), const),
                  pl.BlockSpec((1, D_MODEL), const),
                  pl.BlockSpec(memory_space=pl.ANY)],
        out_specs=pl.BlockSpec((tm, D_MODEL), row),
        out_shape=jax.ShapeDtypeStruct((T, D_MODEL), F32),
        scratch_shapes=[pltpu.VMEM((2, tm * TOK_PITCH, LANES), F32),
                        pltpu.SemaphoreType.DMA((2,))],
        compiler_params=pltpu.CompilerParams(dimension_semantics=("arbitrary",),
                                             vmem_limit_bytes=VMEM_LIMIT),
        name="combine_ple",
    )(pos3, pos3, h1, p2, wpp, wpg, g_ple, g_final, ys)


def _tile_tables(counts, n_tiles):
    nt = (counts + TM_MOE - 1) // TM_MOE
    cum = jnp.cumsum(nt)
    off = (cum - nt) * TM_MOE
    total = cum[-1]
    tile = jnp.arange(n_tiles, dtype=I32)
    valid = (tile < total).astype(I32)
    blk = jnp.maximum(jnp.minimum(tile, total - 1), 0)
    tb = jnp.minimum(jnp.sum((cum[None, :] <= blk[:, None]).astype(I32), axis=1), N_BUCKETS - 1)
    pair_a = jnp.array([0, 0, 0, 1, 1, 2], I32)
    pair_b = jnp.array([1, 2, 3, 2, 3, 3], I32)
    grp = tb // N_PAIRS
    ea = grp * EXPERTS_PER_GROUP + pair_a[tb % N_PAIRS]
    eb = grp * EXPERTS_PER_GROUP + pair_b[tb % N_PAIRS]
    last = jnp.where(nt > 0, cum - 1, -1)
    tail = total + jnp.arange(N_BUCKETS, dtype=I32)
    ztiles = jnp.concatenate([last, jnp.where(tail < n_tiles, tail, -1)]).astype(I32)
    return off, ea, eb, blk, valid, ztiles


def kernel(x, p, w_in, conv_w, conv_b, dt_bias, a_log, d_skip, g_ssd, fox_fbias, w_out, g_mix, g_ffn,
           w_route_group, b_route_group, w_route_expert, b_route_expert, w_exp_gate, w_exp_up, w_exp_down,
           g_ple, w_ple_proj, w_ple_gate, g_final):
    B, S, _ = x.shape
    T = B * S
    assert S % L_SSD == 0 and S % TQ_FOX == 0 and S % TM_PROJ == 0 and T % TD_DISP == 0
    x2 = x.reshape(T, D_MODEL)
    p2 = p[0].reshape(T, D_PLE)

    wi = w_in[0]
    o_xbc = D_SSD
    o_dt = o_xbc + D_XBC
    o_q = o_dt + SSD_HEADS
    o_k = o_q + D_FOX
    o_v = o_k + D_FOX
    o_f = o_v + D_FOX
    wz = wi[:, 0:o_xbc].astype(BF16)
    wxbc = wi[:, o_xbc:o_dt].astype(BF16)
    wq = wi[:, o_q:o_k].astype(BF16)
    wk = wi[:, o_k:o_v].astype(BF16)
    wvt = wi[:, o_v:o_f].T.astype(BF16)
    w_dt = wi[:, o_dt:o_q]
    w_f = wi[:, o_f:o_f + FOX_HEADS]
    wsm = jnp.pad(jnp.concatenate([w_dt] + [w_f] * F_PARTS, axis=1),
                  ((0, 0), (0, LANES - SSD_HEADS - F_PARTS * FOX_HEADS))).astype(BF16)
    wsmt = w_dt.T.astype(BF16)
    lane_pad = lambda v: jnp.pad(v.reshape(1, -1), ((0, 0), (0, LANES - v.shape[-1])))
    colb = lane_pad(jnp.concatenate([dt_bias[0]] + [fox_fbias[0]] * F_PARTS))
    cola = lane_pad(a_log[0])
    rowb = jnp.broadcast_to(dt_bias[0][:, None], (SSD_HEADS, LANES))
    rowa = jnp.broadcast_to(a_log[0][:, None], (SSD_HEADS, LANES))
    dexp = jnp.repeat(d_skip[0], SSD_HEAD_DIM).reshape(1, D_SSD)
    expand = (jnp.arange(LANES)[:, None] == (jnp.arange(D_SSD) // SSD_HEAD_DIM)[None, :]).astype(BF16)
    col = jnp.arange(D_QK)
    head = col // LANES
    j_bias = col % LANES - jnp.where(head % 2 == 0, FOX_HEAD_DIM, 0)
    bias_col = (j_bias >= 0) & (j_bias < F_PARTS)
    qones = bias_col.astype(F32).reshape(1, D_QK)
    src_lane = SSD_HEADS + j_bias * FOX_HEADS + head
    place = (bias_col[None, :] & (jnp.arange(LANES)[:, None] == src_lane[None, :])).astype(BF16)
    wo = w_out[0].astype(BF16)
    wr = jnp.pad(jnp.concatenate([w_route_group[0], w_route_expert[0]], axis=1),
                 ((0, 0), (0, LANES - N_GROUPS - N_EXPERTS)))
    wr_hi = wr.astype(BF16)
    wr = jnp.stack([wr_hi, (wr - wr_hi.astype(F32)).astype(BF16)])
    br = lane_pad(jnp.concatenate([b_route_group[0], b_route_expert[0]]))
    wg = w_exp_gate[0].reshape(N_EXPERTS, D_MODEL, D_EXPERT).astype(BF16)
    wu = w_exp_up[0].reshape(N_EXPERTS, D_MODEL, D_EXPERT).astype(BF16)
    wd = w_exp_down[0].reshape(N_EXPERTS, D_EXPERT, D_MODEL).astype(BF16)

    z, xbc, q, k, vt, sm, smt = _in_proj(x2, g_mix[0].reshape(1, -1), wz, wxbc, wq, wk, wvt, wsm, wsmt,
                                         colb, qones, place, S)
    yssd = _ssd(xbc, z, sm, smt, conv_w[0].reshape(SSD_CONV, D_XBC), conv_b[0].reshape(1, -1),
                colb, cola, rowb, rowa, dexp, g_ssd[0].reshape(1, -1), expand, B, S)
    yfoxt = _fox(q, k, vt, B, S)
    h1, xrow, meta, cnt = _out_proj(x2, yssd, yfoxt, wo[0:D_SSD], wo[D_SSD:], g_ffn[0].reshape(1, -1), wr, br)

    n_rows = T + N_BUCKETS * TM_MOE
    bucket = meta[0].astype(I32)
    rank = meta[1].astype(I32)
    off, ea, eb, blk, valid, ztiles = _tile_tables(cnt[0, 0:N_BUCKETS].astype(I32), n_rows // TM_MOE)
    pos = rank
    for b in range(N_BUCKETS):
        pos = pos + jnp.where(bucket == b, off[b], 0)

    xs = _dispatch(ztiles, pos, xrow, n_rows)
    ys = _experts(ea, eb, blk, valid, xs, wg, wu, wd)
    out = _combine(pos, h1, p2, w_ple_proj[0].astype(BF16), w_ple_gate[0].astype(BF16),
                   g_ple[0].reshape(1, -1), g_final.reshape(1, -1), ys)
    return out.reshape(B, S, D_MODEL)
```

```python
import functools

import jax
import jax.numpy as jnp
from jax import lax
from jax.experimental import pallas as pl
from jax.experimental.pallas import tpu as pltpu

F32 = jnp.float32
BF16 = jnp.bfloat16
I32 = jnp.int32

D_MODEL = 1024
SSD_HEADS = 8
SSD_HEAD_DIM = 64
SSD_GROUPS = 2
SSD_STATE = 128
SSD_CONV = 4
D_SSD = SSD_HEADS * SSD_HEAD_DIM
D_XBC = D_SSD + 2 * SSD_GROUPS * SSD_STATE
FOX_HEADS = 8
FOX_HEAD_DIM = 64
D_FOX = FOX_HEADS * FOX_HEAD_DIM
N_GROUPS = 4
EXPERTS_PER_GROUP = 4
N_EXPERTS = N_GROUPS * EXPERTS_PER_GROUP
N_PAIRS = 6
N_BUCKETS = N_GROUPS * N_PAIRS
D_EXPERT = 512
D_PLE = 256
EPS = 1e-6
LOG2E = 1.4426950408889634
LANES = 128
SUBLANES = 8
TOK_ROWS = D_MODEL // LANES
TOK_PITCH = 12
D_QK = FOX_HEADS * LANES
F_PARTS = 3
VT_ROWS = FOX_HEAD_DIM + 16
VMEM_LIMIT = 52 * 1024 * 1024

TM_PROJ = 512
L_SSD = 256
TQ_FOX = 512
TK_FOX = 256
TM_MOE = 256
TD_DISP = 1024
TM_COMB = 256
DMA_WAIT_UNROLL = 16
DMA_ISSUE_UNROLL = 8

_NT = (((1,), (1,)), ((), ()))
_TN = (((0,), (0,)), ((), ()))


def _rms(x, g):
    ms = jnp.mean(x * x, axis=-1, keepdims=True)
    return x * lax.rsqrt(ms + EPS) * g


def _sigmoid(x):
    return 1.0 / (1.0 + jnp.exp(-x))


def _softplus(x):
    return jnp.maximum(x, 0.0) + jnp.log(1.0 + jnp.exp(-jnp.abs(x)))


def _split_bf16(x, parts):
    out = []
    r = x
    for _ in range(parts):
        h = r.astype(BF16)
        out.append(h)
        r = r - h.astype(F32)
    return out


def _dot(a, b):
    return jnp.dot(a, b, preferred_element_type=F32)


def _dot_exact(a01, x, parts):
    acc = None
    for piece in _split_bf16(x, parts):
        t = _dot(a01, piece)
        acc = t if acc is None else acc + t
    return acc


def _to_token_tiles(ref, x, stride, offset=0):
    m = x.shape[0]
    for s in range(TOK_ROWS):
        ref[pl.ds(offset + s, m, stride=stride), :] = x[:, s * LANES:(s + 1) * LANES]


def _from_token_tiles(ref, m, stride, offset=0):
    return jnp.concatenate([ref[pl.ds(offset + s, m, stride=stride), :] for s in range(TOK_ROWS)], axis=1)


def _inproj_kernel(x_ref, g_ref, wz_ref, wxbc_ref, wq_ref, wk_ref, wvt_ref, wsm_ref, wsmt_ref,
                   colb_ref, qones_ref, place_ref,
                   z_ref, xbc_ref, q_ref, k_ref, vt_ref, sm_ref, smt_ref, fcar_ref, *, tiles_per_seq):
    TM = x_ref.shape[0]

    @pl.when(pl.program_id(0) % tiles_per_seq == 0)
    def _():
        fcar_ref[...] = jnp.zeros(fcar_ref.shape, F32)

    hn = _rms(x_ref[...], g_ref[...]).astype(BF16)
    z_ref[...] = _dot(hn, wz_ref[...]).astype(BF16)
    xbc_ref[...] = _dot(hn, wxbc_ref[...]).astype(BF16)
    vt = lax.dot_general(wvt_ref[...], hn, _NT, preferred_element_type=F32).astype(BF16)
    ones_rows = jnp.ones((VT_ROWS - FOX_HEAD_DIM, TM), BF16)
    for h in range(FOX_HEADS):
        vt_ref[h * VT_ROWS:h * VT_ROWS + FOX_HEAD_DIM, :] = vt[h * FOX_HEAD_DIM:(h + 1) * FOX_HEAD_DIM, :]
        vt_ref[h * VT_ROWS + FOX_HEAD_DIM:(h + 1) * VT_ROWS, :] = ones_rows
    sm = _dot(hn, wsm_ref[...])
    sm_ref[...] = sm
    smt_ref[...] = lax.dot_general(wsmt_ref[...], hn, _NT, preferred_element_type=F32)

    raw = sm + colb_ref[...]
    logf = -_softplus(-raw)
    ri = lax.broadcasted_iota(I32, (TM, TM), 0)
    ci = lax.broadcasted_iota(I32, (TM, TM), 1)
    fcum = _dot_exact((ri >= ci).astype(BF16), logf, 3) + fcar_ref[0:1, :]
    fcar_ref[...] = jnp.broadcast_to(fcum[TM - 1:TM, :], fcar_ref.shape)
    fs = fcum * (-LOG2E)
    hi = fs.astype(BF16)
    r1 = fs - hi.astype(F32)
    mid = r1.astype(BF16)
    lo = (r1 - mid.astype(F32)).astype(BF16)
    lane = lax.broadcasted_iota(I32, (TM, LANES), 1)
    pieces = jnp.where(lane < 16, hi, jnp.where(lane < 24, mid, lo))
    kbias = _dot(pieces, place_ref[...])
    qc = _dot(hn, wq_ref[...]) * (FOX_HEAD_DIM ** -0.5 * LOG2E)
    kc = _dot(hn, wk_ref[...])
    for h in range(FOX_HEADS):
        pair = slice((h // 2) * LANES, (h // 2 + 1) * LANES)
        tile = slice(h * LANES, (h + 1) * LANES)
        own = (lane < FOX_HEAD_DIM) if h % 2 == 0 else (lane >= FOX_HEAD_DIM)
        q_ref[:, tile] = jnp.where(own, qc[:, pair], qones_ref[:, tile]).astype(BF16)
        k_ref[:, tile] = jnp.where(own, kc[:, pair], kbias[:, tile]).astype(BF16)


def _in_proj(x2, g_mix, wz, wxbc, wq, wk, wvt, wsm, wsmt, colb, qones, place, S):
    T = x2.shape[0]
    tm = TM_PROJ
    row = lambda i: (i, 0)
    const = lambda i: (0, 0)
    full = lambda a: pl.BlockSpec(a.shape, const)
    return pl.pallas_call(
        functools.partial(_inproj_kernel, tiles_per_seq=S // tm),
        grid=(T // tm,),
        in_specs=[pl.BlockSpec((tm, D_MODEL), row), full(g_mix), full(wz), full(wxbc), full(wq), full(wk),
                  full(wvt), full(wsm), full(wsmt), full(colb), full(qones), full(place)],
        out_specs=[pl.BlockSpec((tm, D_SSD), row),
                   pl.BlockSpec((tm, D_XBC), row),
                   pl.BlockSpec((tm, D_QK), row),
                   pl.BlockSpec((tm, D_QK), row),
                   pl.BlockSpec((FOX_HEADS * VT_ROWS, tm), lambda i: (0, i)),
                   pl.BlockSpec((tm, LANES), row),
                   pl.BlockSpec((SSD_HEADS, tm), lambda i: (0, i))],
        out_shape=[jax.ShapeDtypeStruct((T, D_SSD), BF16),
                   jax.ShapeDtypeStruct((T, D_XBC), BF16),
                   jax.ShapeDtypeStruct((T, D_QK), BF16),
                   jax.ShapeDtypeStruct((T, D_QK), BF16),
                   jax.ShapeDtypeStruct((FOX_HEADS * VT_ROWS, T), BF16),
                   jax.ShapeDtypeStruct((T, LANES), F32),
                   jax.ShapeDtypeStruct((SSD_HEADS, T), F32)],
        scratch_shapes=[pltpu.VMEM((SUBLANES, LANES), F32)],
        compiler_params=pltpu.CompilerParams(dimension_semantics=("arbitrary",),
                                             vmem_limit_bytes=VMEM_LIMIT),
        name="in_proj",
    )(x2, g_mix, wz, wxbc, wq, wk, wvt, wsm, wsmt, colb, qones, place)


def _ssd_kernel(xbc_ref, z_ref, sm_ref, smt_ref, convw_ref, convb_ref, colb_ref, cola_ref,
                rowb_ref, rowa_ref, dexp_ref, gssd_ref, expand_ref,
                y_ref, tail_ref, st_ref):
    L = xbc_ref.shape[0]
    R = SSD_HEADS // SSD_GROUPS
    TAIL = tail_ref.shape[0]
    c = pl.program_id(1)

    @pl.when(c == 0)
    def _():
        tail_ref[...] = jnp.zeros(tail_ref.shape, BF16)
        st_ref[...] = jnp.zeros(st_ref.shape, F32)

    ri = lax.broadcasted_iota(I32, (L, L), 0)
    ci = lax.broadcasted_iota(I32, (L, L), 1)
    causal = ri >= ci
    tril = causal.astype(BF16)
    triu = (ri <= ci).astype(BF16)

    zf = z_ref[...].astype(F32)
    zgate = zf * _sigmoid(zf)

    x_bf = xbc_ref[...]
    tail = tail_ref[...]
    hr = lax.broadcasted_iota(I32, (SUBLANES, TAIL), 0)
    hc = lax.broadcasted_iota(I32, (SUBLANES, TAIL), 1)
    conv = convb_ref[...] + convw_ref[SSD_CONV - 1:SSD_CONV, :] * x_bf.astype(F32)
    head = None
    for d in range(1, SSD_CONV):
        w_d = convw_ref[SSD_CONV - 1 - d:SSD_CONV - d, :]
        conv = conv + w_d * _dot((ri - d == ci).astype(BF16), x_bf)
        t = w_d * _dot((hc == hr + (TAIL - d)).astype(BF16), tail)
        head = t if head is None else head + t
    conv = jnp.concatenate([conv[0:SUBLANES] + head, conv[SUBLANES:]], axis=0)
    tail_ref[...] = xbc_ref[L - TAIL:L, :]
    xc = conv * _sigmoid(conv)
    xs = xc[:, 0:D_SSD]
    xs_bf = xs.astype(BF16)
    bm = xc[:, D_SSD:D_SSD + SSD_GROUPS * SSD_STATE].astype(BF16)
    cm = xc[:, D_SSD + SSD_GROUPS * SSD_STATE:].astype(BF16)

    dtr = _softplus(smt_ref[...] + rowb_ref[:, 0:1])
    adt_r = dtr * (-LOG2E * jnp.exp(rowa_ref[:, 0:1]))
    csr = None
    for part in _split_bf16(adt_r, 3):
        t = _dot(part, triu)
        csr = t if csr is None else csr + t

    dtc = _softplus(sm_ref[...] + colb_ref[...])
    adt_c = dtc * (-LOG2E * jnp.exp(cola_ref[...]))
    cs_c = _dot_exact(tril, adt_c, 3)
    cs_last = cs_c[L - 1:L, :]
    e1 = jnp.exp2(cs_c)
    wst = dtc * jnp.exp2(cs_last - cs_c)
    ex = expand_ref[...]
    p1 = _split_bf16(e1, 2)
    e1x = _dot(p1[0], ex) + _dot(p1[1], ex)
    p2 = _split_bf16(wst, 2)
    wstx = _dot(p2[0], ex) + _dot(p2[1], ex)

    xw = (xs * wstx).astype(BF16)
    dec8 = jnp.broadcast_to(jnp.exp2(csr[:, L - 1:L]), (SSD_HEADS, SSD_STATE))

    ydiag = []
    yoff = []
    for g in range(SSD_GROUPS):
        bg = bm[:, g * SSD_STATE:(g + 1) * SSD_STATE]
        cg = cm[:, g * SSD_STATE:(g + 1) * SSD_STATE]
        gmat = lax.dot_general(cg, bg, _NT, preferred_element_type=F32)
        s_old = st_ref[g]
        yoff.append(lax.dot_general(cg, s_old.astype(BF16), _NT, preferred_element_type=F32))
        for j in range(R):
            h = g * R + j
            seg = cs_c[:, h:h + 1] - csr[h:h + 1, :]
            lm = jnp.exp2(jnp.where(causal, seg, -jnp.inf))
            m = (gmat * lm * dtr[h:h + 1, :]).astype(BF16)
            ydiag.append(_dot(m, xs_bf[:, h * SSD_HEAD_DIM:(h + 1) * SSD_HEAD_DIM]))
        upd = lax.dot_general(xw[:, g * R * SSD_HEAD_DIM:(g + 1) * R * SSD_HEAD_DIM], bg, _TN,
                              preferred_element_type=F32)
        dec = jnp.concatenate(
            [jnp.broadcast_to(dec8[g * R + j:g * R + j + 1, :], (SSD_HEAD_DIM, SSD_STATE)) for j in range(R)],
            axis=0)
        st_ref[g] = dec * s_old + upd

    y = jnp.concatenate(ydiag, axis=1) + e1x * jnp.concatenate(yoff, axis=1) + dexp_ref[...] * xs
    y_ref[...] = _rms(y * zgate, gssd_ref[...]).astype(BF16)


def _ssd(xbc, z, sm, smt, convw, convb, colb, cola, rowb, rowa, dexp, gssd, expand, B, S):
    L = L_SSD
    nc = S // L
    T = B * S
    row = lambda b, c: (b * nc + c, 0)
    const = lambda b, c: (0, 0)
    full = lambda a: pl.BlockSpec(a.shape, const)
    return pl.pallas_call(
        _ssd_kernel,
        grid=(B, nc),
        in_specs=[pl.BlockSpec((L, D_XBC), row),
                  pl.BlockSpec((L, D_SSD), row),
                  pl.BlockSpec((L, LANES), row),
                  pl.BlockSpec((SSD_HEADS, L), lambda b, c: (0, b * nc + c)),
                  full(convw), full(convb), full(colb), full(cola), full(rowb), full(rowa),
                  full(dexp), full(gssd), full(expand)],
        out_specs=pl.BlockSpec((L, D_SSD), row),
        out_shape=jax.ShapeDtypeStruct((T, D_SSD), BF16),
        scratch_shapes=[pltpu.VMEM((16, D_XBC), BF16),
                        pltpu.VMEM((SSD_GROUPS, (SSD_HEADS // SSD_GROUPS) * SSD_HEAD_DIM, SSD_STATE), F32)],
        compiler_params=pltpu.CompilerParams(dimension_semantics=("arbitrary", "arbitrary"),
                                             vmem_limit_bytes=VMEM_LIMIT),
        name="ssd",
    )(xbc, z, sm, smt, convw, convb, colb, cola, rowb, rowa, dexp, gssd, expand)


def _fox_kernel(q_ref, k_ref, vt_ref, o_ref, sa_ref, sb_ref):
    TQ = q_ref.shape[0]
    TK = TK_FOX
    assert TQ == 2 * TK
    n_pairs = pl.program_id(1)
    n_full = 2 * n_pairs
    key_idx = lax.broadcasted_iota(I32, (TK, TQ), 0)
    qry_idx = lax.broadcasted_iota(I32, (TK, TQ), 1)
    heads = range(FOX_HEADS)
    qs = [q_ref[:, h * LANES:(h + 1) * LANES] for h in heads]

    def scores(j, s_ref, h):
        ks = pl.multiple_of(j * TK, TK)
        s_ref[h] = lax.dot_general(k_ref[pl.ds(ks, TK), h * LANES:(h + 1) * LANES], qs[h], _NT,
                                   preferred_element_type=F32)

    def softmax_pv(j, s_ref, h, state, diag):
        ks = pl.multiple_of(j * TK, TK)
        m, acc = state
        s = s_ref[h]
        if diag is not None:
            s = jnp.where(key_idx + diag * TK <= qry_idx, s, -jnp.inf)
        mn = jnp.maximum(m, jnp.max(s, axis=0, keepdims=True))
        p = jnp.exp2(s - mn).astype(BF16)
        acc = jnp.exp2(m - mn) * acc + _dot(vt_ref[h * VT_ROWS:(h + 1) * VT_ROWS, pl.ds(ks, TK)], p)
        return mn, acc

    def step(j, cur_ref, nxt_ref, carry, diag):
        new = []
        for h in heads:
            if nxt_ref is not None:
                scores(j + 1, nxt_ref, h)
            new.append(softmax_pv(j, cur_ref, h, carry[h], diag))
        return tuple(new)

    def pair(t, carry):
        carry = step(2 * t, sa_ref, sb_ref, carry, None)
        return step(2 * t + 1, sb_ref, sa_ref, carry, None)

    carry = tuple((jnp.full((1, TQ), -1e30, F32), jnp.zeros((VT_ROWS, TQ), F32)) for _ in heads)
    for h in heads:
        scores(0, sa_ref, h)
    carry = lax.fori_loop(0, n_pairs, pair, carry)
    carry = step(n_full, sa_ref, sb_ref, carry, 0)
    carry = step(n_full + 1, sb_ref, None, carry, 1)
    o_ref[...] = jnp.concatenate(
        [acc[0:FOX_HEAD_DIM] / acc[FOX_HEAD_DIM:FOX_HEAD_DIM + 1] for _, acc in carry], axis=0).astype(BF16)


def _fox(q, k, vt, B, S):
    TQ = TQ_FOX
    nq = S // TQ
    T = B * S
    return pl.pallas_call(
        _fox_kernel,
        grid=(B, nq),
        in_specs=[pl.BlockSpec((TQ, D_QK), lambda b, i: (b * nq + i, 0)),
                  pl.BlockSpec((S, D_QK), lambda b, i: (b, 0)),
                  pl.BlockSpec((FOX_HEADS * VT_ROWS, S), lambda b, i: (0, b))],
        out_specs=pl.BlockSpec((D_FOX, TQ), lambda b, i: (0, b * nq + i)),
        out_shape=jax.ShapeDtypeStruct((D_FOX, T), BF16),
        scratch_shapes=[pltpu.VMEM((FOX_HEADS, TK_FOX, TQ), F32), pltpu.VMEM((FOX_HEADS, TK_FOX, TQ), F32)],
        compiler_params=pltpu.CompilerParams(dimension_semantics=("arbitrary", "arbitrary"),
                                             vmem_limit_bytes=VMEM_LIMIT),
        name="fox",
    )(q, k, vt)


def _outproj_kernel(x_ref, ys_ref, yft_ref, wo1_ref, wo2_ref, g_ref, wr_ref, br_ref,
                    h1_ref, xrow_ref, meta_ref, cnt_ref, carry_ref):
    TM = x_ref.shape[0]
    i = pl.program_id(0)

    @pl.when(i == 0)
    def _():
        carry_ref[...] = jnp.zeros(carry_ref.shape, F32)

    h1 = (x_ref[...] + _dot(ys_ref[...], wo1_ref[...])
          + lax.dot_general(yft_ref[...], wo2_ref[...], _TN, preferred_element_type=F32))
    h1_ref[...] = h1
    hn = _rms(h1, g_ref[...])
    h_hi, h_lo = _split_bf16(hn, 2)
    logits = (_dot(h_hi, wr_ref[0]) + _dot(h_lo, wr_ref[0]) + _dot(h_hi, wr_ref[1])) + br_ref[...]
    lane = lax.broadcasted_iota(I32, (TM, LANES), 1)
    ninf = -jnp.inf
    gl = jnp.where(lane < N_GROUPS, logits, ninf)
    gmax = jnp.max(gl, axis=-1, keepdims=True)
    p_grp = 1.0 / jnp.sum(jnp.exp(gl - gmax), axis=-1, keepdims=True)
    g_idx = jnp.min(jnp.where(gl == gmax, lane, LANES), axis=-1, keepdims=True)
    e0 = N_GROUPS + EXPERTS_PER_GROUP * g_idx
    el = jnp.where((lane >= e0) & (lane < e0 + EXPERTS_PER_GROUP), logits, ninf)
    v1 = jnp.max(el, axis=-1, keepdims=True)
    i1 = jnp.min(jnp.where(el == v1, lane, LANES), axis=-1, keepdims=True)
    el2 = jnp.where(lane == i1, ninf, el)
    v2 = jnp.max(el2, axis=-1, keepdims=True)
    i2 = jnp.min(jnp.where(el2 == v2, lane, LANES), axis=-1, keepdims=True)
    e2 = jnp.exp(v2 - v1)
    w1 = p_grp / (1.0 + e2)
    w2 = p_grp * e2 / (1.0 + e2)
    l1 = i1 - e0
    l2 = i2 - e0
    first = l1 < l2
    a = jnp.where(first, l1, l2)
    b = jnp.where(first, l2, l1)
    wa = jnp.where(first, w1, w2)
    wb = jnp.where(first, w2, w1)
    pair = jnp.right_shift(a * (5 - a), 1) + b - 1
    bucket = g_idx * N_PAIRS + pair

    onehot = (lane == bucket).astype(BF16)
    ri = lax.broadcasted_iota(I32, (TM, TM), 0)
    ci = lax.broadcasted_iota(I32, (TM, TM), 1)
    incl = _dot((ri >= ci).astype(BF16), onehot)
    oh = onehot.astype(F32)
    rank = jnp.sum((incl - oh + carry_ref[0:1, :]) * oh, axis=-1, keepdims=True)
    carry_ref[...] = carry_ref[...] + jnp.broadcast_to(incl[TM - 1:TM, :], carry_ref.shape)
    cnt_ref[...] = carry_ref[...]

    route = jnp.where(lane == 0, wa, jnp.where(lane == 1, wb, jnp.where(
        lane == 2, bucket.astype(F32), jnp.where(lane == 3, rank, 0.0))))
    sel = (lax.broadcasted_iota(I32, (SUBLANES, LANES), 0) + 2
           == lax.broadcasted_iota(I32, (SUBLANES, LANES), 1)).astype(BF16)
    meta = None
    for piece in _split_bf16(route, 3):
        t = lax.dot_general(sel, piece, _NT, preferred_element_type=F32)
        meta = t if meta is None else meta + t
    meta_ref[...] = meta
    _to_token_tiles(xrow_ref, hn, TOK_PITCH)
    xrow_ref[pl.ds(TOK_ROWS, TM, stride=TOK_PITCH), :] = jnp.where(lane < 2, route, 0.0)
    for s in range(TOK_ROWS + 1, TOK_PITCH):
        xrow_ref[pl.ds(s, TM, stride=TOK_PITCH), :] = jnp.zeros((TM, LANES), F32)


def _out_proj(x2, yssd, yfoxt, wo1, wo2, g_ffn, wr, br):
    T = x2.shape[0]
    tm = TM_PROJ
    row = lambda i: (i, 0)
    const = lambda i: (0, 0)
    return pl.pallas_call(
        _outproj_kernel,
        grid=(T // tm,),
        in_specs=[pl.BlockSpec((tm, D_MODEL), row),
                  pl.BlockSpec((tm, D_SSD), row),
                  pl.BlockSpec((D_FOX, tm), lambda i: (0, i)),
                  pl.BlockSpec(wo1.shape, const),
                  pl.BlockSpec(wo2.shape, const),
                  pl.BlockSpec((1, D_MODEL), const),
                  pl.BlockSpec(wr.shape, lambda i: (0, 0, 0)),
                  pl.BlockSpec((1, LANES), const)],
        out_specs=[pl.BlockSpec((tm, D_MODEL), row),
                   pl.BlockSpec((tm * TOK_PITCH, LANES), row),
                   pl.BlockSpec((SUBLANES, tm), lambda i: (0, i)),
                   pl.BlockSpec((SUBLANES, LANES), const)],
        out_shape=[jax.ShapeDtypeStruct((T, D_MODEL), F32),
                   jax.ShapeDtypeStruct((T * TOK_PITCH, LANES), F32),
                   jax.ShapeDtypeStruct((SUBLANES, T), F32),
                   jax.ShapeDtypeStruct((SUBLANES, LANES), F32)],
        scratch_shapes=[pltpu.VMEM((SUBLANES, LANES), F32)],
        compiler_params=pltpu.CompilerParams(dimension_semantics=("arbitrary",),
                                             vmem_limit_bytes=VMEM_LIMIT),
        name="out_proj_router",
    )(x2, yssd, yfoxt, wo1, wo2, g_ffn, wr, br)


def _dispatch_kernel(ztile_ref, pos_ref, src_ref, dst_hbm, zero_ref, sem, zsem):
    TD = pos_ref.shape[-1]
    tile_rows = TM_MOE * TOK_PITCH

    @pl.when(pl.program_id(0) == 0)
    def _():
        zero_ref[...] = jnp.zeros(zero_ref.shape, F32)

        def zcopy(e):
            dst = pl.multiple_of(ztile_ref[e] * tile_rows, tile_rows)
            return pltpu.make_async_copy(zero_ref, dst_hbm.at[pl.ds(dst, tile_rows)], zsem)

        def zstart(e, c):
            @pl.when(ztile_ref[e] >= 0)
            def _():
                zcopy(e).start()
            return c

        def zwait(e, c):
            @pl.when(ztile_ref[e] >= 0)
            def _():
                zcopy(e).wait()
            return c

        lax.fori_loop(0, ztile_ref.shape[0], zstart, 0)
        lax.fori_loop(0, ztile_ref.shape[0], zwait, 0)

    def copy(r):
        dst = pl.multiple_of(pos_ref[0, r] * TOK_PITCH, 4)
        src = pl.multiple_of(r * TOK_PITCH, 4)
        return pltpu.make_async_copy(src_ref.at[pl.ds(src, TOK_PITCH)], dst_hbm.at[pl.ds(dst, TOK_PITCH)], sem)

    def start(rb, c):
        for u in range(DMA_ISSUE_UNROLL):
            copy(DMA_ISSUE_UNROLL * rb + u).start(priority=u % 2)
        return c

    lax.fori_loop(0, TD // DMA_ISSUE_UNROLL, start, 0)

    def wait(r, c):
        copy(r).wait()
        return c

    lax.fori_loop(0, TD, wait, 0, unroll=DMA_WAIT_UNROLL)


def _dispatch(ztiles, pos, xrow, n_rows):
    T = pos.shape[0]
    td = TD_DISP
    pos3 = pos.reshape(T // td, 1, td)
    grid_spec = pltpu.PrefetchScalarGridSpec(
        num_scalar_prefetch=1,
        grid=(T // td,),
        in_specs=[pl.BlockSpec((None, 1, td), lambda i, zt: (i, 0, 0), memory_space=pltpu.SMEM),
                  pl.BlockSpec((td * TOK_PITCH, LANES), lambda i, zt: (i, 0))],
        out_specs=pl.BlockSpec(memory_space=pl.ANY),
        scratch_shapes=[pltpu.VMEM((TM_MOE * TOK_PITCH, LANES), F32),
                        pltpu.SemaphoreType.DMA(()), pltpu.SemaphoreType.DMA(())],
    )
    return pl.pallas_call(
        _dispatch_kernel,
        grid_spec=grid_spec,
        out_shape=jax.ShapeDtypeStruct((n_rows * TOK_PITCH, LANES), F32),
        compiler_params=pltpu.CompilerParams(dimension_semantics=("arbitrary",),
                                             vmem_limit_bytes=VMEM_LIMIT),
        name="dispatch",
    )(ztiles, pos3, xrow)


def _experts_kernel(ea_ref, eb_ref, blk_ref, valid_ref,
                    xs_ref, wga_ref, wua_ref, wda_ref, wgb_ref, wub_ref, wdb_ref, ys_ref):
    del ea_ref, eb_ref, blk_ref
    i = pl.program_id(0)
    TM = xs_ref.shape[0] // TOK_PITCH

    @pl.when(valid_ref[i] > 0)
    def _():
        x = _from_token_tiles(xs_ref, TM, TOK_PITCH).astype(BF16)
        w = xs_ref[pl.ds(TOK_ROWS, TM, stride=TOK_PITCH), :]

        def expert(wg_ref, wu_ref, wd_ref, wt):
            g = _dot(x, wg_ref[...])
            u = _dot(x, wu_ref[...])
            return _dot((g * _sigmoid(g) * u * wt).astype(BF16), wd_ref[...])

        y = expert(wga_ref, wua_ref, wda_ref, w[:, 0:1]) + expert(wgb_ref, wub_ref, wdb_ref, w[:, 1:2])
        _to_token_tiles(ys_ref, y, TOK_ROWS)

    @pl.when(valid_ref[i] == 0)
    def _():
        ys_ref[...] = jnp.zeros(ys_ref.shape, F32)


def _experts(ea, eb, blk, valid, xs, wg, wu, wd):
    tm = TM_MOE
    n_rows = xs.shape[0] // TOK_PITCH
    nt = n_rows // tm
    xmap = lambda i, ea, eb, blk, valid: (blk[i], 0)
    amap = lambda i, ea, eb, blk, valid: (ea[i], 0, 0)
    bmap = lambda i, ea, eb, blk, valid: (eb[i], 0, 0)
    up_spec = lambda m: pl.BlockSpec((None, D_MODEL, D_EXPERT), m)
    dn_spec = lambda m: pl.BlockSpec((None, D_EXPERT, D_MODEL), m)
    grid_spec = pltpu.PrefetchScalarGridSpec(
        num_scalar_prefetch=4,
        grid=(nt,),
        in_specs=[pl.BlockSpec((tm * TOK_PITCH, LANES), xmap),
                  up_spec(amap), up_spec(amap), dn_spec(amap),
                  up_spec(bmap), up_spec(bmap), dn_spec(bmap)],
        out_specs=pl.BlockSpec((tm * TOK_ROWS, LANES), lambda i, ea, eb, blk, valid: (i, 0)),
    )
    return pl.pallas_call(
        _experts_kernel,
        grid_spec=grid_spec,
        out_shape=jax.ShapeDtypeStruct((n_rows * TOK_ROWS, LANES), F32),
        compiler_params=pltpu.CompilerParams(dimension_semantics=("arbitrary",),
                                             vmem_limit_bytes=VMEM_LIMIT),
        name="experts",
    )(ea, eb, blk, valid, xs, wg, wu, wd, wg, wu, wd)


def _combine_kernel(pos_cur_ref, pos_nxt_ref, h1_ref, p_ref, wpp_ref, wpg_ref, gple_ref, gfin_ref,
                    ys_hbm, o_ref, buf_ref, sem):
    TM = h1_ref.shape[0]
    i = pl.program_id(0)
    n = pl.num_programs(0)
    slot = i % 2

    def copy(pos_ref, s, r):
        src = pl.multiple_of(pos_ref[0, r] * TOK_ROWS, TOK_ROWS)
        dst = pl.multiple_of(r * TOK_PITCH, 4)
        return pltpu.make_async_copy(ys_hbm.at[pl.ds(src, TOK_ROWS)], buf_ref.at[s, pl.ds(dst, TOK_ROWS)],
                                     sem.at[s])

    def gather(pos_ref, s):
        def start(rb, c):
            for u in range(DMA_ISSUE_UNROLL):
                copy(pos_ref, s, DMA_ISSUE_UNROLL * rb + u).start(priority=u % 2)
            return c
        lax.fori_loop(0, TM // DMA_ISSUE_UNROLL, start, 0)

    @pl.when(i == 0)
    def _():
        gather(pos_cur_ref, 0)

    @pl.when(i + 1 < n)
    def _():
        gather(pos_nxt_ref, 1 - slot)

    def wait(r, c):
        copy(pos_cur_ref, slot, r).wait()
        return c

    lax.fori_loop(0, TM, wait, 0, unroll=DMA_WAIT_UNROLL)

    h2 = h1_ref[...] + _from_token_tiles(buf_ref.at[slot], TM, TOK_PITCH)
    n3 = _rms(h2, gple_ref[...]).astype(BF16)
    gate = _sigmoid(_dot(n3, wpg_ref[...]))
    pp = _dot(p_ref[...].astype(BF16), wpp_ref[...])
    o_ref[...] = _rms(h2 + pp * gate, gfin_ref[...])


def _combine(pos, h1, p2, wpp, wpg, g_ple, g_final, ys):
    T = h1.shape[0]
    tm = TM_COMB
    nt = T // tm
    pos3 = pos.reshape(nt, 1, tm)
    row = lambda i: (i, 0)
    const = lambda i: (0, 0)
    return pl.pallas_call(
        _combine_kernel,
        grid=(nt,),
        in_specs=[pl.BlockSpec((None, 1, tm), lambda i: (i, 0, 0), memory_space=pltpu.SMEM),
                  pl.BlockSpec((None, 1, tm), lambda i: (jnp.minimum(i + 1, nt - 1), 0, 0),
                               memory_space=pltpu.SMEM),
                  pl.BlockSpec((tm, D_MODEL), row),
                  pl.BlockSpec((tm, D_PLE), row),
                  pl.BlockSpec(wpp.shape, const),
                  pl.BlockSpec(wpg.shape, const),
                  pl.BlockSpec((1, D_MODEL), const),
                  pl.BlockSpec((1, D_MODEL), const),
                  pl.BlockSpec(memory_space=pl.ANY)],
        out_specs=pl.BlockSpec((tm, D_MODEL), row),
        out_shape=jax.ShapeDtypeStruct((T, D_MODEL), F32),
        scratch_shapes=[pltpu.VMEM((2, tm * TOK_PITCH, LANES), F32),
                        pltpu.SemaphoreType.DMA((2,))],
        compiler_params=pltpu.CompilerParams(dimension_semantics=("arbitrary",),
                                             vmem_limit_bytes=VMEM_LIMIT),
        name="combine_ple",
    )(pos3, pos3, h1, p2, wpp, wpg, g_ple, g_final, ys)


def _tile_tables(counts, n_tiles):
    nt = (counts + TM_MOE - 1) // TM_MOE
    cum = jnp.cumsum(nt)
    off = (cum - nt) * TM_MOE
    total = cum[-1]
    tile = jnp.arange(n_tiles, dtype=I32)
    valid = (tile < total).astype(I32)
    blk = jnp.maximum(jnp.minimum(tile, total - 1), 0)
    tb = jnp.minimum(jnp.sum((cum[None, :] <= blk[:, None]).astype(I32), axis=1), N_BUCKETS - 1)
    pair_a = jnp.array([0, 0, 0, 1, 1, 2], I32)
    pair_b = jnp.array([1, 2, 3, 2, 3, 3], I32)
    grp = tb // N_PAIRS
    ea = grp * EXPERTS_PER_GROUP + pair_a[tb % N_PAIRS]
    eb = grp * EXPERTS_PER_GROUP + pair_b[tb % N_PAIRS]
    last = jnp.where(nt > 0, cum - 1, -1)
    tail = total + jnp.arange(N_BUCKETS, dtype=I32)
    ztiles = jnp.concatenate([last, jnp.where(tail < n_tiles, tail, -1)]).astype(I32)
    return off, ea, eb, blk, valid, ztiles


def kernel(x, p, w_in, conv_w, conv_b, dt_bias, a_log, d_skip, g_ssd, fox_fbias, w_out, g_mix, g_ffn,
           w_route_group, b_route_group, w_route_expert, b_route_expert, w_exp_gate, w_exp_up, w_exp_down,
           g_ple, w_ple_proj, w_ple_gate, g_final):
    B, S, _ = x.shape
    T = B * S
    assert S % L_SSD == 0 and S % TQ_FOX == 0 and S % TM_PROJ == 0 and T % TD_DISP == 0
    x2 = x.reshape(T, D_MODEL)
    p2 = p[0].reshape(T, D_PLE)

    wi = w_in[0]
    o_xbc = D_SSD
    o_dt = o_xbc + D_XBC
    o_q = o_dt + SSD_HEADS
    o_k = o_q + D_FOX
    o_v = o_k + D_FOX
    o_f = o_v + D_FOX
    wz = wi[:, 0:o_xbc].astype(BF16)
    wxbc = wi[:, o_xbc:o_dt].astype(BF16)
    wq = wi[:, o_q:o_k].astype(BF16)
    wk = wi[:, o_k:o_v].astype(BF16)
    wvt = wi[:, o_v:o_f].T.astype(BF16)
    w_dt = wi[:, o_dt:o_q]
    w_f = wi[:, o_f:o_f + FOX_HEADS]
    wsm = jnp.pad(jnp.concatenate([w_dt] + [w_f] * F_PARTS, axis=1),
                  ((0, 0), (0, LANES - SSD_HEADS - F_PARTS * FOX_HEADS))).astype(BF16)
    wsmt = w_dt.T.astype(BF16)
    lane_pad = lambda v: jnp.pad(v.reshape(1, -1), ((0, 0), (0, LANES - v.shape[-1])))
    colb = lane_pad(jnp.concatenate([dt_bias[0]] + [fox_fbias[0]] * F_PARTS))
    cola = lane_pad(a_log[0])
    rowb = jnp.broadcast_to(dt_bias[0][:, None], (SSD_HEADS, LANES))
    rowa = jnp.broadcast_to(a_log[0][:, None], (SSD_HEADS, LANES))
    dexp = jnp.repeat(d_skip[0], SSD_HEAD_DIM).reshape(1, D_SSD)
    expand = (jnp.arange(LANES)[:, None] == (jnp.arange(D_SSD) // SSD_HEAD_DIM)[None, :]).astype(BF16)
    col = jnp.arange(D_QK)
    head = col // LANES
    j_bias = col % LANES - jnp.where(head % 2 == 0, FOX_HEAD_DIM, 0)
    bias_col = (j_bias >= 0) & (j_bias < F_PARTS)
    qones = bias_col.astype(F32).reshape(1, D_QK)
    src_lane = SSD_HEADS + j_bias * FOX_HEADS + head
    place = (bias_col[None, :] & (jnp.arange(LANES)[:, None] == src_lane[None, :])).astype(BF16)
    wo = w_out[0].astype(BF16)
    wr = jnp.pad(jnp.concatenate([w_route_group[0], w_route_expert[0]], axis=1),
                 ((0, 0), (0, LANES - N_GROUPS - N_EXPERTS)))
    wr_hi = wr.astype(BF16)
    wr = jnp.stack([wr_hi, (wr - wr_hi.astype(F32)).astype(BF16)])
    br = lane_pad(jnp.concatenate([b_route_group[0], b_route_expert[0]]))
    wg = w_exp_gate[0].reshape(N_EXPERTS, D_MODEL, D_EXPERT).astype(BF16)
    wu = w_exp_up[0].reshape(N_EXPERTS, D_MODEL, D_EXPERT).astype(BF16)
    wd = w_exp_down[0].reshape(N_EXPERTS, D_EXPERT, D_MODEL).astype(BF16)

    z, xbc, q, k, vt, sm, smt = _in_proj(x2, g_mix[0].reshape(1, -1), wz, wxbc, wq, wk, wvt, wsm, wsmt,
                                         colb, qones, place, S)
    yssd = _ssd(xbc, z, sm, smt, conv_w[0].reshape(SSD_CONV, D_XBC), conv_b[0].reshape(1, -1),
                colb, cola, rowb, rowa, dexp, g_ssd[0].reshape(1, -1), expand, B, S)
    yfoxt = _fox(q, k, vt, B, S)
    h1, xrow, meta, cnt = _out_proj(x2, yssd, yfoxt, wo[0:D_SSD], wo[D_SSD:], g_ffn[0].reshape(1, -1), wr, br)

    n_rows = T + N_BUCKETS * TM_MOE
    bucket = meta[0].astype(I32)
    rank = meta[1].astype(I32)
    off, ea, eb, blk, valid, ztiles = _tile_tables(cnt[0, 0:N_BUCKETS].astype(I32), n_rows // TM_MOE)
    pos = rank
    for b in range(N_BUCKETS):
        pos = pos + jnp.where(bucket == b, off[b], 0)

    xs = _dispatch(ztiles, pos, xrow, n_rows)
    ys = _experts(ea, eb, blk, valid, xs, wg, wu, wd)
    out = _combine(pos, h1, p2, w_ple_proj[0].astype(BF16), w_ple_gate[0].astype(BF16),
                   g_ple[0].reshape(1, -1), g_final.reshape(1, -1), ys)
    return out.reshape(B, S, D_MODEL)
```

```python
import functools

import jax
import jax.numpy as jnp
from jax import lax
from jax.experimental import pallas as pl
from jax.experimental.pallas import tpu as pltpu

F32 = jnp.float32
BF16 = jnp.bfloat16
I32 = jnp.int32

D_MODEL = 1024
SSD_HEADS = 8
SSD_HEAD_DIM = 64
SSD_GROUPS = 2
SSD_STATE = 128
SSD_CONV = 4
D_SSD = SSD_HEADS * SSD_HEAD_DIM
D_XBC = D_SSD + 2 * SSD_GROUPS * SSD_STATE
FOX_HEADS = 8
FOX_HEAD_DIM = 64
D_FOX = FOX_HEADS * FOX_HEAD_DIM
N_GROUPS = 4
EXPERTS_PER_GROUP = 4
N_EXPERTS = N_GROUPS * EXPERTS_PER_GROUP
N_PAIRS = 6
N_BUCKETS = N_GROUPS * N_PAIRS
ROUTE_ROWS = 32
D_EXPERT = 512
D_PLE = 256
EPS = 1e-6
LOG2E = 1.4426950408889634
LANES = 128
SUBLANES = 8
TOK_ROWS = D_MODEL // LANES
TOK_PITCH = 12
D_QK = FOX_HEADS * LANES
F_PARTS = 3
VT_ROWS = FOX_HEAD_DIM + 16
VMEM_LIMIT = 52 * 1024 * 1024

TM_PROJ = 512
L_SSD = 256
TQ_FOX = 512
TK_FOX = 256
TM_MOE = 256
TD_DISP = 1024
TM_COMB = 256
DMA_WAIT_UNROLL = 16
DMA_ISSUE_UNROLL = 8

_NT = (((1,), (1,)), ((), ()))
_TN = (((0,), (0,)), ((), ()))


def _rms(x, g):
    ms = jnp.mean(x * x, axis=-1, keepdims=True)
    return x * lax.rsqrt(ms + EPS) * g


def _sigmoid(x):
    return 1.0 / (1.0 + jnp.exp(-x))


def _softplus(x):
    return jnp.maximum(x, 0.0) + jnp.log(1.0 + jnp.exp(-jnp.abs(x)))


def _split_bf16(x, parts):
    out = []
    r = x
    for _ in range(parts):
        h = r.astype(BF16)
        out.append(h)
        r = r - h.astype(F32)
    return out


def _dot(a, b):
    return jnp.dot(a, b, preferred_element_type=F32)


def _dot_exact(a01, x, parts):
    acc = None
    for piece in _split_bf16(x, parts):
        t = _dot(a01, piece)
        acc = t if acc is None else acc + t
    return acc


def _to_token_tiles(ref, x, stride, offset=0):
    m = x.shape[0]
    for s in range(TOK_ROWS):
        ref[pl.ds(offset + s, m, stride=stride), :] = x[:, s * LANES:(s + 1) * LANES]


def _from_token_tiles(ref, m, stride, offset=0):
    return jnp.concatenate([ref[pl.ds(offset + s, m, stride=stride), :] for s in range(TOK_ROWS)], axis=1)


def _inproj_kernel(x_ref, g_ref, wz_ref, wxbc_ref, wq_ref, wk_ref, wvt_ref, wsm_ref, wsmt_ref,
                   colb_ref, qones_ref, place_ref,
                   z_ref, xbc_ref, q_ref, k_ref, vt_ref, sm_ref, smt_ref, fcar_ref, *, tiles_per_seq):
    TM = x_ref.shape[0]

    @pl.when(pl.program_id(0) % tiles_per_seq == 0)
    def _():
        fcar_ref[...] = jnp.zeros(fcar_ref.shape, F32)

    hn = _rms(x_ref[...], g_ref[...]).astype(BF16)
    z_ref[...] = _dot(hn, wz_ref[...]).astype(BF16)
    xbc_ref[...] = _dot(hn, wxbc_ref[...]).astype(BF16)
    vt = lax.dot_general(wvt_ref[...], hn, _NT, preferred_element_type=F32).astype(BF16)
    ones_rows = jnp.ones((VT_ROWS - FOX_HEAD_DIM, TM), BF16)
    for h in range(FOX_HEADS):
        vt_ref[h * VT_ROWS:h * VT_ROWS + FOX_HEAD_DIM, :] = vt[h * FOX_HEAD_DIM:(h + 1) * FOX_HEAD_DIM, :]
        vt_ref[h * VT_ROWS + FOX_HEAD_DIM:(h + 1) * VT_ROWS, :] = ones_rows
    sm = _dot(hn, wsm_ref[...])
    sm_ref[...] = sm
    smt_ref[...] = lax.dot_general(wsmt_ref[...], hn, _NT, preferred_element_type=F32)

    raw = sm + colb_ref[...]
    logf = -_softplus(-raw)
    ri = lax.broadcasted_iota(I32, (TM, TM), 0)
    ci = lax.broadcasted_iota(I32, (TM, TM), 1)
    fcum = _dot_exact((ri >= ci).astype(BF16), logf, 3) + fcar_ref[0:1, :]
    fcar_ref[...] = jnp.broadcast_to(fcum[TM - 1:TM, :], fcar_ref.shape)
    fs = fcum * (-LOG2E)
    hi = fs.astype(BF16)
    r1 = fs - hi.astype(F32)
    mid = r1.astype(BF16)
    lo = (r1 - mid.astype(F32)).astype(BF16)
    lane = lax.broadcasted_iota(I32, (TM, LANES), 1)
    pieces = jnp.where(lane < 16, hi, jnp.where(lane < 24, mid, lo))
    kbias = _dot(pieces, place_ref[...])
    qc = _dot(hn, wq_ref[...]) * (FOX_HEAD_DIM ** -0.5 * LOG2E)
    kc = _dot(hn, wk_ref[...])
    for h in range(FOX_HEADS):
        pair = slice((h // 2) * LANES, (h // 2 + 1) * LANES)
        tile = slice(h * LANES, (h + 1) * LANES)
        own = (lane < FOX_HEAD_DIM) if h % 2 == 0 else (lane >= FOX_HEAD_DIM)
        q_ref[:, tile] = jnp.where(own, qc[:, pair], qones_ref[:, tile]).astype(BF16)
        k_ref[:, tile] = jnp.where(own, kc[:, pair], kbias[:, tile]).astype(BF16)


def _in_proj(x2, g_mix, wz, wxbc, wq, wk, wvt, wsm, wsmt, colb, qones, place, S):
    T = x2.shape[0]
    tm = TM_PROJ
    row = lambda i: (i, 0)
    const = lambda i: (0, 0)
    full = lambda a: pl.BlockSpec(a.shape, const)
    return pl.pallas_call(
        functools.partial(_inproj_kernel, tiles_per_seq=S // tm),
        grid=(T // tm,),
        in_specs=[pl.BlockSpec((tm, D_MODEL), row), full(g_mix), full(wz), full(wxbc), full(wq), full(wk),
                  full(wvt), full(wsm), full(wsmt), full(colb), full(qones), full(place)],
        out_specs=[pl.BlockSpec((tm, D_SSD), row),
                   pl.BlockSpec((tm, D_XBC), row),
                   pl.BlockSpec((tm, D_QK), row),
                   pl.BlockSpec((tm, D_QK), row),
                   pl.BlockSpec((FOX_HEADS * VT_ROWS, tm), lambda i: (0, i)),
                   pl.BlockSpec((tm, LANES), row),
                   pl.BlockSpec((SSD_HEADS, tm), lambda i: (0, i))],
        out_shape=[jax.ShapeDtypeStruct((T, D_SSD), BF16),
                   jax.ShapeDtypeStruct((T, D_XBC), BF16),
                   jax.ShapeDtypeStruct((T, D_QK), BF16),
                   jax.ShapeDtypeStruct((T, D_QK), BF16),
                   jax.ShapeDtypeStruct((FOX_HEADS * VT_ROWS, T), BF16),
                   jax.ShapeDtypeStruct((T, LANES), F32),
                   jax.ShapeDtypeStruct((SSD_HEADS, T), F32)],
        scratch_shapes=[pltpu.VMEM((SUBLANES, LANES), F32)],
        compiler_params=pltpu.CompilerParams(dimension_semantics=("arbitrary",),
                                             vmem_limit_bytes=VMEM_LIMIT),
        name="in_proj",
    )(x2, g_mix, wz, wxbc, wq, wk, wvt, wsm, wsmt, colb, qones, place)


def _ssd_kernel(xbc_ref, z_ref, sm_ref, smt_ref, convw_ref, convb_ref, colb_ref, cola_ref,
                rowb_ref, rowa_ref, dexp_ref, gssd_ref, expand_ref,
                y_ref, tail_ref, st_ref):
    L = xbc_ref.shape[0]
    R = SSD_HEADS // SSD_GROUPS
    TAIL = tail_ref.shape[0]
    c = pl.program_id(1)

    @pl.when(c == 0)
    def _():
        tail_ref[...] = jnp.zeros(tail_ref.shape, BF16)
        st_ref[...] = jnp.zeros(st_ref.shape, F32)

    ri = lax.broadcasted_iota(I32, (L, L), 0)
    ci = lax.broadcasted_iota(I32, (L, L), 1)
    causal = ri >= ci
    tril = causal.astype(BF16)
    triu = (ri <= ci).astype(BF16)

    zf = z_ref[...].astype(F32)
    zgate = zf * _sigmoid(zf)

    x_bf = xbc_ref[...]
    tail = tail_ref[...]
    hr = lax.broadcasted_iota(I32, (SUBLANES, TAIL), 0)
    hc = lax.broadcasted_iota(I32, (SUBLANES, TAIL), 1)
    conv = convb_ref[...] + convw_ref[SSD_CONV - 1:SSD_CONV, :] * x_bf.astype(F32)
    head = None
    for d in range(1, SSD_CONV):
        w_d = convw_ref[SSD_CONV - 1 - d:SSD_CONV - d, :]
        conv = conv + w_d * _dot((ri - d == ci).astype(BF16), x_bf)
        t = w_d * _dot((hc == hr + (TAIL - d)).astype(BF16), tail)
        head = t if head is None else head + t
    conv = jnp.concatenate([conv[0:SUBLANES] + head, conv[SUBLANES:]], axis=0)
    tail_ref[...] = xbc_ref[L - TAIL:L, :]
    xc = conv * _sigmoid(conv)
    xs = xc[:, 0:D_SSD]
    xs_bf = xs.astype(BF16)
    bm = xc[:, D_SSD:D_SSD + SSD_GROUPS * SSD_STATE].astype(BF16)
    cm = xc[:, D_SSD + SSD_GROUPS * SSD_STATE:].astype(BF16)

    dtr = _softplus(smt_ref[...] + rowb_ref[:, 0:1])
    adt_r = dtr * (-LOG2E * jnp.exp(rowa_ref[:, 0:1]))
    csr = None
    for part in _split_bf16(adt_r, 3):
        t = _dot(part, triu)
        csr = t if csr is None else csr + t

    dtc = _softplus(sm_ref[...] + colb_ref[...])
    adt_c = dtc * (-LOG2E * jnp.exp(cola_ref[...]))
    cs_c = _dot_exact(tril, adt_c, 3)
    cs_last = cs_c[L - 1:L, :]
    e1 = jnp.exp2(cs_c)
    wst = dtc * jnp.exp2(cs_last - cs_c)
    ex = expand_ref[...]
    p1 = _split_bf16(e1, 2)
    e1x = _dot(p1[0], ex) + _dot(p1[1], ex)
    p2 = _split_bf16(wst, 2)
    wstx = _dot(p2[0], ex) + _dot(p2[1], ex)

    xw = (xs * wstx).astype(BF16)
    dec8 = jnp.broadcast_to(jnp.exp2(csr[:, L - 1:L]), (SSD_HEADS, SSD_STATE))

    ydiag = []
    yoff = []
    for g in range(SSD_GROUPS):
        bg = bm[:, g * SSD_STATE:(g + 1) * SSD_STATE]
        cg = cm[:, g * SSD_STATE:(g + 1) * SSD_STATE]
        gmat = lax.dot_general(cg, bg, _NT, preferred_element_type=F32)
        s_old = st_ref[g]
        yoff.append(lax.dot_general(cg, s_old.astype(BF16), _NT, preferred_element_type=F32))
        for j in range(R):
            h = g * R + j
            seg = cs_c[:, h:h + 1] - csr[h:h + 1, :]
            lm = jnp.exp2(jnp.where(causal, seg, -jnp.inf))
            m = (gmat * lm * dtr[h:h + 1, :]).astype(BF16)
            ydiag.append(_dot(m, xs_bf[:, h * SSD_HEAD_DIM:(h + 1) * SSD_HEAD_DIM]))
        upd = lax.dot_general(xw[:, g * R * SSD_HEAD_DIM:(g + 1) * R * SSD_HEAD_DIM], bg, _TN,
                              preferred_element_type=F32)
        dec = jnp.concatenate(
            [jnp.broadcast_to(dec8[g * R + j:g * R + j + 1, :], (SSD_HEAD_DIM, SSD_STATE)) for j in range(R)],
            axis=0)
        st_ref[g] = dec * s_old + upd

    y = jnp.concatenate(ydiag, axis=1) + e1x * jnp.concatenate(yoff, axis=1) + dexp_ref[...] * xs
    y_ref[...] = _rms(y * zgate, gssd_ref[...]).astype(BF16)


def _ssd(xbc, z, sm, smt, convw, convb, colb, cola, rowb, rowa, dexp, gssd, expand, B, S):
    L = L_SSD
    nc = S // L
    T = B * S
    row = lambda b, c: (b * nc + c, 0)
    const = lambda b, c: (0, 0)
    full = lambda a: pl.BlockSpec(a.shape, const)
    return pl.pallas_call(
        _ssd_kernel,
        grid=(B, nc),
        in_specs=[pl.BlockSpec((L, D_XBC), row),
                  pl.BlockSpec((L, D_SSD), row),
                  pl.BlockSpec((L, LANES), row),
                  pl.BlockSpec((SSD_HEADS, L), lambda b, c: (0, b * nc + c)),
                  full(convw), full(convb), full(colb), full(cola), full(rowb), full(rowa),
                  full(dexp), full(gssd), full(expand)],
        out_specs=pl.BlockSpec((L, D_SSD), row),
        out_shape=jax.ShapeDtypeStruct((T, D_SSD), BF16),
        scratch_shapes=[pltpu.VMEM((16, D_XBC), BF16),
                        pltpu.VMEM((SSD_GROUPS, (SSD_HEADS // SSD_GROUPS) * SSD_HEAD_DIM, SSD_STATE), F32)],
        compiler_params=pltpu.CompilerParams(dimension_semantics=("arbitrary", "arbitrary"),
                                             vmem_limit_bytes=VMEM_LIMIT),
        name="ssd",
    )(xbc, z, sm, smt, convw, convb, colb, cola, rowb, rowa, dexp, gssd, expand)


def _fox_kernel(q_ref, k_ref, vt_ref, o_ref, sa_ref, sb_ref):
    TQ = q_ref.shape[0]
    TK = TK_FOX
    assert TQ == 2 * TK
    n_pairs = pl.program_id(1)
    n_full = 2 * n_pairs
    key_idx = lax.broadcasted_iota(I32, (TK, TQ), 0)
    qry_idx = lax.broadcasted_iota(I32, (TK, TQ), 1)
    heads = range(FOX_HEADS)
    qs = [q_ref[:, h * LANES:(h + 1) * LANES] for h in heads]

    def scores(j, s_ref, h):
        ks = pl.multiple_of(j * TK, TK)
        s_ref[h] = lax.dot_general(k_ref[pl.ds(ks, TK), h * LANES:(h + 1) * LANES], qs[h], _NT,
                                   preferred_element_type=F32)

    def softmax_pv(j, s_ref, h, state, diag):
        ks = pl.multiple_of(j * TK, TK)
        m, acc = state
        s = s_ref[h]
        if diag is not None:
            s = jnp.where(key_idx + diag * TK <= qry_idx, s, -jnp.inf)
        mn = jnp.maximum(m, jnp.max(s, axis=0, keepdims=True))
        p = jnp.exp2(s - mn).astype(BF16)
        acc = jnp.exp2(m - mn) * acc + _dot(vt_ref[h * VT_ROWS:(h + 1) * VT_ROWS, pl.ds(ks, TK)], p)
        return mn, acc

    def step(j, cur_ref, nxt_ref, carry, diag):
        new = []
        for h in heads:
            if nxt_ref is not None:
                scores(j + 1, nxt_ref, h)
            new.append(softmax_pv(j, cur_ref, h, carry[h], diag))
        return tuple(new)

    def pair(t, carry):
        carry = step(2 * t, sa_ref, sb_ref, carry, None)
        return step(2 * t + 1, sb_ref, sa_ref, carry, None)

    carry = tuple((jnp.full((1, TQ), -1e30, F32), jnp.zeros((VT_ROWS, TQ), F32)) for _ in heads)
    for h in heads:
        scores(0, sa_ref, h)
    carry = lax.fori_loop(0, n_pairs, pair, carry)
    carry = step(n_full, sa_ref, sb_ref, carry, 0)
    carry = step(n_full + 1, sb_ref, None, carry, 1)
    o_ref[...] = jnp.concatenate(
        [acc[0:FOX_HEAD_DIM] / acc[FOX_HEAD_DIM:FOX_HEAD_DIM + 1] for _, acc in carry], axis=0).astype(BF16)


def _fox(q, k, vt, B, S):
    TQ = TQ_FOX
    nq = S // TQ
    T = B * S
    return pl.pallas_call(
        _fox_kernel,
        grid=(B, nq),
        in_specs=[pl.BlockSpec((TQ, D_QK), lambda b, i: (b * nq + i, 0)),
                  pl.BlockSpec((S, D_QK), lambda b, i: (b, 0)),
                  pl.BlockSpec((FOX_HEADS * VT_ROWS, S), lambda b, i: (0, b))],
        out_specs=pl.BlockSpec((D_FOX, TQ), lambda b, i: (0, b * nq + i)),
        out_shape=jax.ShapeDtypeStruct((D_FOX, T), BF16),
        scratch_shapes=[pltpu.VMEM((FOX_HEADS, TK_FOX, TQ), F32), pltpu.VMEM((FOX_HEADS, TK_FOX, TQ), F32)],
        compiler_params=pltpu.CompilerParams(dimension_semantics=("arbitrary", "arbitrary"),
                                             vmem_limit_bytes=VMEM_LIMIT),
        name="fox",
    )(q, k, vt)


def _outproj_kernel(x_ref, ys_ref, yft_ref, wo1_ref, wo2_ref, g_ref, wr_ref, br_ref,
                    h1_ref, xrow_ref, meta_ref, cnt_ref, carry_ref):
    TM = x_ref.shape[0]
    i = pl.program_id(0)

    @pl.when(i == 0)
    def _():
        carry_ref[...] = jnp.zeros(carry_ref.shape, F32)

    h1 = (x_ref[...] + _dot(ys_ref[...], wo1_ref[...])
          + lax.dot_general(yft_ref[...], wo2_ref[...], _TN, preferred_element_type=F32))
    h1_ref[...] = h1
    hn = _rms(h1, g_ref[...])
    h_hi, h_lo = _split_bf16(hn, 2)
    lg = lax.dot_general(wr_ref[...], h_hi, _NT, preferred_element_type=F32)
    logits = (lg[0:ROUTE_ROWS] + lg[ROUTE_ROWS:]
              + lax.dot_general(wr_ref[0:ROUTE_ROWS, :], h_lo, _NT, preferred_element_type=F32)
              + br_ref[:, 0:1])
    row = lax.broadcasted_iota(I32, (ROUTE_ROWS, TM), 0).astype(F32)
    ninf = -jnp.inf
    far = float(ROUTE_ROWS)
    gl = jnp.where(row < N_GROUPS, logits, ninf)
    gmax = jnp.max(gl, axis=0, keepdims=True)
    p_grp = 1.0 / jnp.sum(jnp.exp(gl - gmax), axis=0, keepdims=True)
    g_idx = jnp.min(jnp.where(gl == gmax, row, far), axis=0, keepdims=True)
    e0 = N_GROUPS + EXPERTS_PER_GROUP * g_idx
    el = jnp.where((row >= e0) & (row < e0 + EXPERTS_PER_GROUP), logits, ninf)
    v1 = jnp.max(el, axis=0, keepdims=True)
    i1 = jnp.min(jnp.where(el == v1, row, far), axis=0, keepdims=True)
    el2 = jnp.where(row == i1, ninf, el)
    v2 = jnp.max(el2, axis=0, keepdims=True)
    i2 = jnp.min(jnp.where(el2 == v2, row, far), axis=0, keepdims=True)
    e2 = jnp.exp(v2 - v1)
    w1 = p_grp / (1.0 + e2)
    w2 = p_grp * e2 / (1.0 + e2)
    l1 = i1 - e0
    l2 = i2 - e0
    first = l1 < l2
    a = jnp.where(first, l1, l2)
    b = jnp.where(first, l2, l1)
    wa = jnp.where(first, w1, w2)
    wb = jnp.where(first, w2, w1)
    pair = a * (5.0 - a) * 0.5 + b - 1.0
    bucket = g_idx * N_PAIRS + pair

    onehot = (row == bucket).astype(BF16)
    ri = lax.broadcasted_iota(I32, (TM, TM), 0)
    ci = lax.broadcasted_iota(I32, (TM, TM), 1)
    incl = _dot(onehot, (ri <= ci).astype(BF16))
    oh = onehot.astype(F32)
    rank = jnp.sum((incl - oh + carry_ref[:, 0:1]) * oh, axis=0, keepdims=True)
    carry_ref[...] = carry_ref[...] + jnp.broadcast_to(incl[:, TM - 1:TM], carry_ref.shape)
    cnt_ref[...] = carry_ref[...]

    r8 = lax.broadcasted_iota(I32, (SUBLANES, TM), 0)
    meta_ref[...] = jnp.where(r8 == 0, bucket, jnp.where(r8 == 1, rank, 0.0))
    wrows = jnp.where(r8 == 0, wa, jnp.where(r8 == 1, wb, 0.0))
    eye = (lax.broadcasted_iota(I32, (SUBLANES, LANES), 0)
           == lax.broadcasted_iota(I32, (SUBLANES, LANES), 1)).astype(BF16)
    wcols = None
    for piece in _split_bf16(wrows, 3):
        t = lax.dot_general(piece, eye, _TN, preferred_element_type=F32)
        wcols = t if wcols is None else wcols + t
    _to_token_tiles(xrow_ref, hn, TOK_PITCH)
    xrow_ref[pl.ds(TOK_ROWS, TM, stride=TOK_PITCH), :] = wcols
    for s in range(TOK_ROWS + 1, TOK_PITCH):
        xrow_ref[pl.ds(s, TM, stride=TOK_PITCH), :] = jnp.zeros((TM, LANES), F32)


def _out_proj(x2, yssd, yfoxt, wo1, wo2, g_ffn, wr, br):
    T = x2.shape[0]
    tm = TM_PROJ
    row = lambda i: (i, 0)
    const = lambda i: (0, 0)
    return pl.pallas_call(
        _outproj_kernel,
        grid=(T // tm,),
        in_specs=[pl.BlockSpec((tm, D_MODEL), row),
                  pl.BlockSpec((tm, D_SSD), row),
                  pl.BlockSpec((D_FOX, tm), lambda i: (0, i)),
                  pl.BlockSpec(wo1.shape, const),
                  pl.BlockSpec(wo2.shape, const),
                  pl.BlockSpec((1, D_MODEL), const),
                  pl.BlockSpec(wr.shape, const),
                  pl.BlockSpec(br.shape, const)],
        out_specs=[pl.BlockSpec((tm, D_MODEL), row),
                   pl.BlockSpec((tm * TOK_PITCH, LANES), row),
                   pl.BlockSpec((SUBLANES, tm), lambda i: (0, i)),
                   pl.BlockSpec((ROUTE_ROWS, LANES), const)],
        out_shape=[jax.ShapeDtypeStruct((T, D_MODEL), F32),
                   jax.ShapeDtypeStruct((T * TOK_PITCH, LANES), F32),
                   jax.ShapeDtypeStruct((SUBLANES, T), F32),
                   jax.ShapeDtypeStruct((ROUTE_ROWS, LANES), F32)],
        scratch_shapes=[pltpu.VMEM((ROUTE_ROWS, LANES), F32)],
        compiler_params=pltpu.CompilerParams(dimension_semantics=("arbitrary",),
                                             vmem_limit_bytes=VMEM_LIMIT),
        name="out_proj_router",
    )(x2, yssd, yfoxt, wo1, wo2, g_ffn, wr, br)


def _dispatch_kernel(ztile_ref, pos_ref, src_ref, dst_hbm, zero_ref, sem, zsem):
    TD = pos_ref.shape[-1]
    tile_rows = TM_MOE * TOK_PITCH

    @pl.when(pl.program_id(0) == 0)
    def _():
        zero_ref[...] = jnp.zeros(zero_ref.shape, F32)

        def zcopy(e):
            dst = pl.multiple_of(ztile_ref[e] * tile_rows, tile_rows)
            return pltpu.make_async_copy(zero_ref, dst_hbm.at[pl.ds(dst, tile_rows)], zsem)

        def zstart(e, c):
            @pl.when(ztile_ref[e] >= 0)
            def _():
                zcopy(e).start()
            return c

        def zwait(e, c):
            @pl.when(ztile_ref[e] >= 0)
            def _():
                zcopy(e).wait()
            return c

        lax.fori_loop(0, ztile_ref.shape[0], zstart, 0)
        lax.fori_loop(0, ztile_ref.shape[0], zwait, 0)

    def copy(r):
        dst = pl.multiple_of(pos_ref[0, r] * TOK_PITCH, 4)
        src = pl.multiple_of(r * TOK_PITCH, 4)
        return pltpu.make_async_copy(src_ref.at[pl.ds(src, TOK_PITCH)], dst_hbm.at[pl.ds(dst, TOK_PITCH)], sem)

    def start(rb, c):
        for u in range(DMA_ISSUE_UNROLL):
            copy(DMA_ISSUE_UNROLL * rb + u).start(priority=u % 2)
        return c

    lax.fori_loop(0, TD // DMA_ISSUE_UNROLL, start, 0)

    def wait(r, c):
        copy(r).wait()
        return c

    lax.fori_loop(0, TD, wait, 0, unroll=DMA_WAIT_UNROLL)


def _dispatch(ztiles, pos, xrow, n_rows):
    T = pos.shape[0]
    td = TD_DISP
    pos3 = pos.reshape(T // td, 1, td)
    grid_spec = pltpu.PrefetchScalarGridSpec(
        num_scalar_prefetch=1,
        grid=(T // td,),
        in_specs=[pl.BlockSpec((None, 1, td), lambda i, zt: (i, 0, 0), memory_space=pltpu.SMEM),
                  pl.BlockSpec((td * TOK_PITCH, LANES), lambda i, zt: (i, 0))],
        out_specs=pl.BlockSpec(memory_space=pl.ANY),
        scratch_shapes=[pltpu.VMEM((TM_MOE * TOK_PITCH, LANES), F32),
                        pltpu.SemaphoreType.DMA(()), pltpu.SemaphoreType.DMA(())],
    )
    return pl.pallas_call(
        _dispatch_kernel,
        grid_spec=grid_spec,
        out_shape=jax.ShapeDtypeStruct((n_rows * TOK_PITCH, LANES), F32),
        compiler_params=pltpu.CompilerParams(dimension_semantics=("arbitrary",),
                                             vmem_limit_bytes=VMEM_LIMIT),
        name="dispatch",
    )(ztiles, pos3, xrow)


def _experts_kernel(ea_ref, eb_ref, blk_ref, valid_ref,
                    xs_ref, wga_ref, wua_ref, wda_ref, wgb_ref, wub_ref, wdb_ref, ys_ref):
    del ea_ref, eb_ref, blk_ref
    i = pl.program_id(0)
    TM = xs_ref.shape[0] // TOK_PITCH

    @pl.when(valid_ref[i] > 0)
    def _():
        x = _from_token_tiles(xs_ref, TM, TOK_PITCH).astype(BF16)
        w = xs_ref[pl.ds(TOK_ROWS, TM, stride=TOK_PITCH), :]

        def expert(wg_ref, wu_ref, wd_ref, wt):
            g = _dot(x, wg_ref[...])
            u = _dot(x, wu_ref[...])
            return _dot((g * _sigmoid(g) * u * wt).astype(BF16), wd_ref[...])

        y = expert(wga_ref, wua_ref, wda_ref, w[:, 0:1]) + expert(wgb_ref, wub_ref, wdb_ref, w[:, 1:2])
        _to_token_tiles(ys_ref, y, TOK_ROWS)

    @pl.when(valid_ref[i] == 0)
    def _():
        ys_ref[...] = jnp.zeros(ys_ref.shape, F32)


def _experts(ea, eb, blk, valid, xs, wg, wu, wd):
    tm = TM_MOE
    n_rows = xs.shape[0] // TOK_PITCH
    nt = n_rows // tm
    xmap = lambda i, ea, eb, blk, valid: (blk[i], 0)
    amap = lambda i, ea, eb, blk, valid: (ea[i], 0, 0)
    bmap = lambda i, ea, eb, blk, valid: (eb[i], 0, 0)
    up_spec = lambda m: pl.BlockSpec((None, D_MODEL, D_EXPERT), m)
    dn_spec = lambda m: pl.BlockSpec((None, D_EXPERT, D_MODEL), m)
    grid_spec = pltpu.PrefetchScalarGridSpec(
        num_scalar_prefetch=4,
        grid=(nt,),
        in_specs=[pl.BlockSpec((tm * TOK_PITCH, LANES), xmap),
                  up_spec(amap), up_spec(amap), dn_spec(amap),
                  up_spec(bmap), up_spec(bmap), dn_spec(bmap)],
        out_specs=pl.BlockSpec((tm * TOK_ROWS, LANES), lambda i, ea, eb, blk, valid: (i, 0)),
    )
    return pl.pallas_call(
        _experts_kernel,
        grid_spec=grid_spec,
        out_shape=jax.ShapeDtypeStruct((n_rows * TOK_ROWS, LANES), F32),
        compiler_params=pltpu.CompilerParams(dimension_semantics=("arbitrary",),
                                             vmem_limit_bytes=VMEM_LIMIT),
        name="experts",
    )(ea, eb, blk, valid, xs, wg, wu, wd, wg, wu, wd)


def _combine_kernel(pos_cur_ref, pos_nxt_ref, h1_ref, p_ref, wpp_ref, wpg_ref, gple_ref, gfin_ref,
                    ys_hbm, o_ref, buf_ref, sem):
    TM = h1_ref.shape[0]
    i = pl.program_id(0)
    n = pl.num_programs(0)
    slot = i % 2

    def copy(pos_ref, s, r):
        src = pl.multiple_of(pos_ref[0, r] * TOK_ROWS, TOK_ROWS)
        dst = pl.multiple_of(r * TOK_PITCH, 4)
        return pltpu.make_async_copy(ys_hbm.at[pl.ds(src, TOK_ROWS)], buf_ref.at[s, pl.ds(dst, TOK_ROWS)],
                                     sem.at[s])

    def gather(pos_ref, s):
        def start(rb, c):
            for u in range(DMA_ISSUE_UNROLL):
                copy(pos_ref, s, DMA_ISSUE_UNROLL * rb + u).start(priority=u % 2)
            return c
        lax.fori_loop(0, TM // DMA_ISSUE_UNROLL, start, 0)

    @pl.when(i == 0)
    def _():
        gather(pos_cur_ref, 0)

    @pl.when(i + 1 < n)
    def _():
        gather(pos_nxt_ref, 1 - slot)

    def wait(r, c):
        copy(pos_cur_ref, slot, r).wait()
        return c

    lax.fori_loop(0, TM, wait, 0, unroll=DMA_WAIT_UNROLL)

    h2 = h1_ref[...] + _from_token_tiles(buf_ref.at[slot], TM, TOK_PITCH)
    n3 = _rms(h2, gple_ref[...]).astype(BF16)
    gate = _sigmoid(_dot(n3, wpg_ref[...]))
    pp = _dot(p_ref[...].astype(BF16), wpp_ref[...])
    o_ref[...] = _rms(h2 + pp * gate, gfin_ref[...])


def _combine(pos, h1, p2, wpp, wpg, g_ple, g_final, ys):
    T = h1.shape[0]
    tm = TM_COMB
    nt = T // tm
    pos3 = pos.reshape(nt, 1, tm)
    row = lambda i: (i, 0)
    const = lambda i: (0, 0)
    return pl.pallas_call(
        _combine_kernel,
        grid=(nt,),
        in_specs=[pl.BlockSpec((None, 1, tm), lambda i: (i, 0, 0), memory_space=pltpu.SMEM),
                  pl.BlockSpec((None, 1, tm), lambda i: (jnp.minimum(i + 1, nt - 1), 0, 0),
                               memory_space=pltpu.SMEM),
                  pl.BlockSpec((tm, D_MODEL), row),
                  pl.BlockSpec((tm, D_PLE), row),
                  pl.BlockSpec(wpp.shape, const),
                  pl.BlockSpec(wpg.shape, const),
                  pl.BlockSpec((1, D_MODEL), const),
                  pl.BlockSpec((1, D_MODEL), const),
                  pl.BlockSpec(memory_space=pl.ANY)],
        out_specs=pl.BlockSpec((tm, D_MODEL), row),
        out_shape=jax.ShapeDtypeStruct((T, D_MODEL), F32),
        scratch_shapes=[pltpu.VMEM((2, tm * TOK_PITCH, LANES), F32),
                        pltpu.SemaphoreType.DMA((2,))],
        compiler_params=pltpu.CompilerParams(dimension_semantics=("arbitrary",),
                                             vmem_limit_bytes=VMEM_LIMIT),
        name="combine_ple",
    )(pos3, pos3, h1, p2, wpp, wpg, g_ple, g_final, ys)


def _tile_tables(counts, n_tiles):
    nt = (counts + TM_MOE - 1) // TM_MOE
    cum = jnp.cumsum(nt)
    off = (cum - nt) * TM_MOE
    total = cum[-1]
    tile = jnp.arange(n_tiles, dtype=I32)
    valid = (tile < total).astype(I32)
    blk = jnp.maximum(jnp.minimum(tile, total - 1), 0)
    tb = jnp.minimum(jnp.sum((cum[None, :] <= blk[:, None]).astype(I32), axis=1), N_BUCKETS - 1)
    pair_a = jnp.array([0, 0, 0, 1, 1, 2], I32)
    pair_b = jnp.array([1, 2, 3, 2, 3, 3], I32)
    grp = tb // N_PAIRS
    ea = grp * EXPERTS_PER_GROUP + pair_a[tb % N_PAIRS]
    eb = grp * EXPERTS_PER_GROUP + pair_b[tb % N_PAIRS]
    last = jnp.where(nt > 0, cum - 1, -1)
    tail = total + jnp.arange(N_BUCKETS, dtype=I32)
    ztiles = jnp.concatenate([last, jnp.where(tail < n_tiles, tail, -1)]).astype(I32)
    return off, ea, eb, blk, valid, ztiles


def kernel(x, p, w_in, conv_w, conv_b, dt_bias, a_log, d_skip, g_ssd, fox_fbias, w_out, g_mix, g_ffn,
           w_route_group, b_route_group, w_route_expert, b_route_expert, w_exp_gate, w_exp_up, w_exp_down,
           g_ple, w_ple_proj, w_ple_gate, g_final):
    B, S, _ = x.shape
    T = B * S
    assert S % L_SSD == 0 and S % TQ_FOX == 0 and S % TM_PROJ == 0 and T % TD_DISP == 0
    x2 = x.reshape(T, D_MODEL)
    p2 = p[0].reshape(T, D_PLE)

    wi = w_in[0]
    o_xbc = D_SSD
    o_dt = o_xbc + D_XBC
    o_q = o_dt + SSD_HEADS
    o_k = o_q + D_FOX
    o_v = o_k + D_FOX
    o_f = o_v + D_FOX
    wz = wi[:, 0:o_xbc].astype(BF16)
    wxbc = wi[:, o_xbc:o_dt].astype(BF16)
    wq = wi[:, o_q:o_k].astype(BF16)
    wk = wi[:, o_k:o_v].astype(BF16)
    wvt = wi[:, o_v:o_f].T.astype(BF16)
    w_dt = wi[:, o_dt:o_q]
    w_f = wi[:, o_f:o_f + FOX_HEADS]
    wsm = jnp.pad(jnp.concatenate([w_dt] + [w_f] * F_PARTS, axis=1),
                  ((0, 0), (0, LANES - SSD_HEADS - F_PARTS * FOX_HEADS))).astype(BF16)
    wsmt = w_dt.T.astype(BF16)
    lane_pad = lambda v: jnp.pad(v.reshape(1, -1), ((0, 0), (0, LANES - v.shape[-1])))
    colb = lane_pad(jnp.concatenate([dt_bias[0]] + [fox_fbias[0]] * F_PARTS))
    cola = lane_pad(a_log[0])
    rowb = jnp.broadcast_to(dt_bias[0][:, None], (SSD_HEADS, LANES))
    rowa = jnp.broadcast_to(a_log[0][:, None], (SSD_HEADS, LANES))
    dexp = jnp.repeat(d_skip[0], SSD_HEAD_DIM).reshape(1, D_SSD)
    expand = (jnp.arange(LANES)[:, None] == (jnp.arange(D_SSD) // SSD_HEAD_DIM)[None, :]).astype(BF16)
    col = jnp.arange(D_QK)
    head = col // LANES
    j_bias = col % LANES - jnp.where(head % 2 == 0, FOX_HEAD_DIM, 0)
    bias_col = (j_bias >= 0) & (j_bias < F_PARTS)
    qones = bias_col.astype(F32).reshape(1, D_QK)
    src_lane = SSD_HEADS + j_bias * FOX_HEADS + head
    place = (bias_col[None, :] & (jnp.arange(LANES)[:, None] == src_lane[None, :])).astype(BF16)
    wo = w_out[0].astype(BF16)
    wr = jnp.pad(jnp.concatenate([w_route_group[0], w_route_expert[0]], axis=1).T,
                 ((0, ROUTE_ROWS - N_GROUPS - N_EXPERTS), (0, 0)))
    wr_hi = wr.astype(BF16)
    wr = jnp.concatenate([wr_hi, (wr - wr_hi.astype(F32)).astype(BF16)], axis=0)
    br = jnp.broadcast_to(jnp.pad(jnp.concatenate([b_route_group[0], b_route_expert[0]]),
                                  (0, ROUTE_ROWS - N_GROUPS - N_EXPERTS))[:, None], (ROUTE_ROWS, LANES))
    wg = w_exp_gate[0].reshape(N_EXPERTS, D_MODEL, D_EXPERT).astype(BF16)
    wu = w_exp_up[0].reshape(N_EXPERTS, D_MODEL, D_EXPERT).astype(BF16)
    wd = w_exp_down[0].reshape(N_EXPERTS, D_EXPERT, D_MODEL).astype(BF16)

    z, xbc, q, k, vt, sm, smt = _in_proj(x2, g_mix[0].reshape(1, -1), wz, wxbc, wq, wk, wvt, wsm, wsmt,
                                         colb, qones, place, S)
    yssd = _ssd(xbc, z, sm, smt, conv_w[0].reshape(SSD_CONV, D_XBC), conv_b[0].reshape(1, -1),
                colb, cola, rowb, rowa, dexp, g_ssd[0].reshape(1, -1), expand, B, S)
    yfoxt = _fox(q, k, vt, B, S)
    h1, xrow, meta, cnt = _out_proj(x2, yssd, yfoxt, wo[0:D_SSD], wo[D_SSD:], g_ffn[0].reshape(1, -1), wr, br)

    n_rows = T + N_BUCKETS * TM_MOE
    bucket = meta[0].astype(I32)
    rank = meta[1].astype(I32)
    off, ea, eb, blk, valid, ztiles = _tile_tables(cnt[0:N_BUCKETS, 0].astype(I32), n_rows // TM_MOE)
    pos = rank
    for b in range(N_BUCKETS):
        pos = pos + jnp.where(bucket == b, off[b], 0)

    xs = _dispatch(ztiles, pos, xrow, n_rows)
    ys = _experts(ea, eb, blk, valid, xs, wg, wu, wd)
    out = _combine(pos, h1, p2, w_ple_proj[0].astype(BF16), w_ple_gate[0].astype(BF16),
                   g_ple[0].reshape(1, -1), g_final.reshape(1, -1), ys)
    return out.reshape(B, S, D_MODEL)
```

```python
import functools

import jax
import jax.numpy as jnp
from jax import lax
from jax.experimental import pallas as pl
from jax.experimental.pallas import tpu as pltpu

F32 = jnp.float32
BF16 = jnp.bfloat16
I32 = jnp.int32

D_MODEL = 1024
SSD_HEADS = 8
SSD_HEAD_DIM = 64
SSD_GROUPS = 2
SSD_STATE = 128
SSD_CONV = 4
D_SSD = SSD_HEADS * SSD_HEAD_DIM
D_XBC = D_SSD + 2 * SSD_GROUPS * SSD_STATE
FOX_HEADS = 8
FOX_HEAD_DIM = 64
D_FOX = FOX_HEADS * FOX_HEAD_DIM
N_GROUPS = 4
EXPERTS_PER_GROUP = 4
N_EXPERTS = N_GROUPS * EXPERTS_PER_GROUP
N_PAIRS = 6
N_BUCKETS = N_GROUPS * N_PAIRS
ROUTE_ROWS = 32
D_EXPERT = 512
D_PLE = 256
EPS = 1e-6
LOG2E = 1.4426950408889634
LANES = 128
SUBLANES = 8
TOK_ROWS = D_MODEL // LANES
TOK_PITCH = 12
D_QK = FOX_HEADS * LANES
F_PARTS = 3
VT_ROWS = FOX_HEAD_DIM + 16
VMEM_LIMIT = 52 * 1024 * 1024

TM_PROJ = 512
L_SSD = 256
TQ_FOX = 512
TK_FOX = 256
TM_MOE = 256
TD_DISP = 1024
TM_COMB = 512
DMA_WAIT_UNROLL = 16
DMA_ISSUE_UNROLL = 8

_NT = (((1,), (1,)), ((), ()))
_TN = (((0,), (0,)), ((), ()))


def _rms(x, g):
    ms = jnp.mean(x * x, axis=-1, keepdims=True)
    return x * lax.rsqrt(ms + EPS) * g


def _sigmoid(x):
    return 1.0 / (1.0 + jnp.exp(-x))


def _softplus(x):
    return jnp.maximum(x, 0.0) + jnp.log(1.0 + jnp.exp(-jnp.abs(x)))


def _split_bf16(x, parts):
    out = []
    r = x
    for _ in range(parts):
        h = r.astype(BF16)
        out.append(h)
        r = r - h.astype(F32)
    return out


def _dot(a, b):
    return jnp.dot(a, b, preferred_element_type=F32)


def _dot_exact(a01, x, parts):
    acc = None
    for piece in _split_bf16(x, parts):
        t = _dot(a01, piece)
        acc = t if acc is None else acc + t
    return acc


def _to_token_tiles(ref, x, stride, offset=0):
    m = x.shape[0]
    for s in range(TOK_ROWS):
        ref[pl.ds(offset + s, m, stride=stride), :] = x[:, s * LANES:(s + 1) * LANES]


def _from_token_tiles(ref, m, stride, offset=0):
    return jnp.concatenate([ref[pl.ds(offset + s, m, stride=stride), :] for s in range(TOK_ROWS)], axis=1)


def _inproj_kernel(x_ref, g_ref, wz_ref, wxbc_ref, wq_ref, wk_ref, wvt_ref, wsm_ref, wsmt_ref,
                   colb_ref, qones_ref, place_ref,
                   z_ref, xbc_ref, q_ref, k_ref, vt_ref, sm_ref, smt_ref, fcar_ref, *, tiles_per_seq):
    TM = x_ref.shape[0]

    @pl.when(pl.program_id(0) % tiles_per_seq == 0)
    def _():
        fcar_ref[...] = jnp.zeros(fcar_ref.shape, F32)

    hn = _rms(x_ref[...], g_ref[...]).astype(BF16)
    z_ref[...] = _dot(hn, wz_ref[...]).astype(BF16)
    xbc_ref[...] = _dot(hn, wxbc_ref[...]).astype(BF16)
    vt = lax.dot_general(wvt_ref[...], hn, _NT, preferred_element_type=F32).astype(BF16)
    ones_rows = jnp.ones((VT_ROWS - FOX_HEAD_DIM, TM), BF16)
    for h in range(FOX_HEADS):
        vt_ref[h * VT_ROWS:h * VT_ROWS + FOX_HEAD_DIM, :] = vt[h * FOX_HEAD_DIM:(h + 1) * FOX_HEAD_DIM, :]
        vt_ref[h * VT_ROWS + FOX_HEAD_DIM:(h + 1) * VT_ROWS, :] = ones_rows
    sm = _dot(hn, wsm_ref[...])
    sm_ref[...] = sm
    smt_ref[...] = lax.dot_general(wsmt_ref[...], hn, _NT, preferred_element_type=F32)

    raw = sm + colb_ref[...]
    logf = -_softplus(-raw)
    ri = lax.broadcasted_iota(I32, (TM, TM), 0)
    ci = lax.broadcasted_iota(I32, (TM, TM), 1)
    fcum = _dot_exact((ri >= ci).astype(BF16), logf, 3) + fcar_ref[0:1, :]
    fcar_ref[...] = jnp.broadcast_to(fcum[TM - 1:TM, :], fcar_ref.shape)
    fs = fcum * (-LOG2E)
    hi = fs.astype(BF16)
    r1 = fs - hi.astype(F32)
    mid = r1.astype(BF16)
    lo = (r1 - mid.astype(F32)).astype(BF16)
    lane = lax.broadcasted_iota(I32, (TM, LANES), 1)
    pieces = jnp.where(lane < 16, hi, jnp.where(lane < 24, mid, lo))
    kbias = _dot(pieces, place_ref[...])
    qc = _dot(hn, wq_ref[...]) * (FOX_HEAD_DIM ** -0.5 * LOG2E)
    kc = _dot(hn, wk_ref[...])
    for h in range(FOX_HEADS):
        pair = slice((h // 2) * LANES, (h // 2 + 1) * LANES)
        tile = slice(h * LANES, (h + 1) * LANES)
        own = (lane < FOX_HEAD_DIM) if h % 2 == 0 else (lane >= FOX_HEAD_DIM)
        q_ref[:, tile] = jnp.where(own, qc[:, pair], qones_ref[:, tile]).astype(BF16)
        k_ref[:, tile] = jnp.where(own, kc[:, pair], kbias[:, tile]).astype(BF16)


def _in_proj(x2, g_mix, wz, wxbc, wq, wk, wvt, wsm, wsmt, colb, qones, place, S):
    T = x2.shape[0]
    tm = TM_PROJ
    row = lambda i: (i, 0)
    const = lambda i: (0, 0)
    full = lambda a: pl.BlockSpec(a.shape, const)
    return pl.pallas_call(
        functools.partial(_inproj_kernel, tiles_per_seq=S // tm),
        grid=(T // tm,),
        in_specs=[pl.BlockSpec((tm, D_MODEL), row), full(g_mix), full(wz), full(wxbc), full(wq), full(wk),
                  full(wvt), full(wsm), full(wsmt), full(colb), full(qones), full(place)],
        out_specs=[pl.BlockSpec((tm, D_SSD), row),
                   pl.BlockSpec((tm, D_XBC), row),
                   pl.BlockSpec((tm, D_QK), row),
                   pl.BlockSpec((tm, D_QK), row),
                   pl.BlockSpec((FOX_HEADS * VT_ROWS, tm), lambda i: (0, i)),
                   pl.BlockSpec((tm, LANES), row),
                   pl.BlockSpec((SSD_HEADS, tm), lambda i: (0, i))],
        out_shape=[jax.ShapeDtypeStruct((T, D_SSD), BF16),
                   jax.ShapeDtypeStruct((T, D_XBC), BF16),
                   jax.ShapeDtypeStruct((T, D_QK), BF16),
                   jax.ShapeDtypeStruct((T, D_QK), BF16),
                   jax.ShapeDtypeStruct((FOX_HEADS * VT_ROWS, T), BF16),
                   jax.ShapeDtypeStruct((T, LANES), F32),
                   jax.ShapeDtypeStruct((SSD_HEADS, T), F32)],
        scratch_shapes=[pltpu.VMEM((SUBLANES, LANES), F32)],
        compiler_params=pltpu.CompilerParams(dimension_semantics=("arbitrary",),
                                             vmem_limit_bytes=VMEM_LIMIT),
        name="in_proj",
    )(x2, g_mix, wz, wxbc, wq, wk, wvt, wsm, wsmt, colb, qones, place)


def _ssd_kernel(xbc_ref, z_ref, sm_ref, smt_ref, convw_ref, convb_ref, colb_ref, cola_ref,
                rowb_ref, rowa_ref, dexp_ref, gssd_ref, expand_ref,
                y_ref, tail_ref, st_ref):
    L = xbc_ref.shape[0]
    R = SSD_HEADS // SSD_GROUPS
    TAIL = tail_ref.shape[0]
    c = pl.program_id(1)

    @pl.when(c == 0)
    def _():
        tail_ref[...] = jnp.zeros(tail_ref.shape, BF16)
        st_ref[...] = jnp.zeros(st_ref.shape, F32)

    ri = lax.broadcasted_iota(I32, (L, L), 0)
    ci = lax.broadcasted_iota(I32, (L, L), 1)
    causal = ri >= ci
    tril = causal.astype(BF16)
    triu = (ri <= ci).astype(BF16)

    zf = z_ref[...].astype(F32)
    zgate = zf * _sigmoid(zf)

    x_bf = xbc_ref[...]
    tail = tail_ref[...]
    hr = lax.broadcasted_iota(I32, (SUBLANES, TAIL), 0)
    hc = lax.broadcasted_iota(I32, (SUBLANES, TAIL), 1)
    conv = convb_ref[...] + convw_ref[SSD_CONV - 1:SSD_CONV, :] * x_bf.astype(F32)
    head = None
    for d in range(1, SSD_CONV):
        w_d = convw_ref[SSD_CONV - 1 - d:SSD_CONV - d, :]
        conv = conv + w_d * _dot((ri - d == ci).astype(BF16), x_bf)
        t = w_d * _dot((hc == hr + (TAIL - d)).astype(BF16), tail)
        head = t if head is None else head + t
    conv = jnp.concatenate([conv[0:SUBLANES] + head, conv[SUBLANES:]], axis=0)
    tail_ref[...] = xbc_ref[L - TAIL:L, :]
    xc = conv * _sigmoid(conv)
    xs = xc[:, 0:D_SSD]
    xs_bf = xs.astype(BF16)
    bm = xc[:, D_SSD:D_SSD + SSD_GROUPS * SSD_STATE].astype(BF16)
    cm = xc[:, D_SSD + SSD_GROUPS * SSD_STATE:].astype(BF16)

    dtr = _softplus(smt_ref[...] + rowb_ref[:, 0:1])
    adt_r = dtr * (-LOG2E * jnp.exp(rowa_ref[:, 0:1]))
    csr = None
    for part in _split_bf16(adt_r, 3):
        t = _dot(part, triu)
        csr = t if csr is None else csr + t

    dtc = _softplus(sm_ref[...] + colb_ref[...])
    adt_c = dtc * (-LOG2E * jnp.exp(cola_ref[...]))
    cs_c = _dot_exact(tril, adt_c, 3)
    cs_last = cs_c[L - 1:L, :]
    e1 = jnp.exp2(cs_c)
    wst = dtc * jnp.exp2(cs_last - cs_c)
    ex = expand_ref[...]
    p1 = _split_bf16(e1, 2)
    e1x = _dot(p1[0], ex) + _dot(p1[1], ex)
    p2 = _split_bf16(wst, 2)
    wstx = _dot(p2[0], ex) + _dot(p2[1], ex)

    xw = (xs * wstx).astype(BF16)
    dec8 = jnp.broadcast_to(jnp.exp2(csr[:, L - 1:L]), (SSD_HEADS, SSD_STATE))

    ydiag = []
    yoff = []
    for g in range(SSD_GROUPS):
        bg = bm[:, g * SSD_STATE:(g + 1) * SSD_STATE]
        cg = cm[:, g * SSD_STATE:(g + 1) * SSD_STATE]
        gmat = lax.dot_general(cg, bg, _NT, preferred_element_type=F32)
        s_old = st_ref[g]
        yoff.append(lax.dot_general(cg, s_old.astype(BF16), _NT, preferred_element_type=F32))
        for j in range(R):
            h = g * R + j
            seg = cs_c[:, h:h + 1] - csr[h:h + 1, :]
            lm = jnp.exp2(jnp.where(causal, seg, -jnp.inf))
            m = (gmat * lm * dtr[h:h + 1, :]).astype(BF16)
            ydiag.append(_dot(m, xs_bf[:, h * SSD_HEAD_DIM:(h + 1) * SSD_HEAD_DIM]))
        upd = lax.dot_general(xw[:, g * R * SSD_HEAD_DIM:(g + 1) * R * SSD_HEAD_DIM], bg, _TN,
                              preferred_element_type=F32)
        dec = jnp.concatenate(
            [jnp.broadcast_to(dec8[g * R + j:g * R + j + 1, :], (SSD_HEAD_DIM, SSD_STATE)) for j in range(R)],
            axis=0)
        st_ref[g] = dec * s_old + upd

    y = jnp.concatenate(ydiag, axis=1) + e1x * jnp.concatenate(yoff, axis=1) + dexp_ref[...] * xs
    y_ref[...] = _rms(y * zgate, gssd_ref[...]).astype(BF16)


def _ssd(xbc, z, sm, smt, convw, convb, colb, cola, rowb, rowa, dexp, gssd, expand, B, S):
    L = L_SSD
    nc = S // L
    T = B * S
    row = lambda b, c: (b * nc + c, 0)
    const = lambda b, c: (0, 0)
    full = lambda a: pl.BlockSpec(a.shape, const)
    return pl.pallas_call(
        _ssd_kernel,
        grid=(B, nc),
        in_specs=[pl.BlockSpec((L, D_XBC), row),
                  pl.BlockSpec((L, D_SSD), row),
                  pl.BlockSpec((L, LANES), row),
                  pl.BlockSpec((SSD_HEADS, L), lambda b, c: (0, b * nc + c)),
                  full(convw), full(convb), full(colb), full(cola), full(rowb), full(rowa),
                  full(dexp), full(gssd), full(expand)],
        out_specs=pl.BlockSpec((L, D_SSD), row),
        out_shape=jax.ShapeDtypeStruct((T, D_SSD), BF16),
        scratch_shapes=[pltpu.VMEM((16, D_XBC), BF16),
                        pltpu.VMEM((SSD_GROUPS, (SSD_HEADS // SSD_GROUPS) * SSD_HEAD_DIM, SSD_STATE), F32)],
        compiler_params=pltpu.CompilerParams(dimension_semantics=("arbitrary", "arbitrary"),
                                             vmem_limit_bytes=VMEM_LIMIT),
        name="ssd",
    )(xbc, z, sm, smt, convw, convb, colb, cola, rowb, rowa, dexp, gssd, expand)


def _fox_kernel(q_ref, k_ref, vt_ref, o_ref, sa_ref, sb_ref):
    TQ = q_ref.shape[0]
    TK = TK_FOX
    assert TQ == 2 * TK
    n_pairs = pl.program_id(1)
    n_full = 2 * n_pairs
    key_idx = lax.broadcasted_iota(I32, (TK, TQ), 0)
    qry_idx = lax.broadcasted_iota(I32, (TK, TQ), 1)
    heads = range(FOX_HEADS)
    qs = [q_ref[:, h * LANES:(h + 1) * LANES] for h in heads]

    def masked(s, diag):
        return s if diag is None else jnp.where(key_idx + diag * TK <= qry_idx, s, -jnp.inf)

    def scores(j, s_ref, h):
        ks = pl.multiple_of(j * TK, TK)
        s = lax.dot_general(k_ref[pl.ds(ks, TK), h * LANES:(h + 1) * LANES], qs[h], _NT,
                            preferred_element_type=F32)
        s_ref[h] = s
        return jnp.max(s, axis=0, keepdims=True)

    def softmax_pv(j, s_ref, h, state, diag):
        ks = pl.multiple_of(j * TK, TK)
        m, acc, smax = state
        s = s_ref[h]
        if diag is not None:
            s = jnp.where(key_idx + diag * TK <= qry_idx, s, -jnp.inf)
            smax = jnp.max(s, axis=0, keepdims=True)
        mn = jnp.maximum(m, smax)
        p = jnp.exp2(s - mn).astype(BF16)
        acc = jnp.exp2(m - mn) * acc + _dot(vt_ref[h * VT_ROWS:(h + 1) * VT_ROWS, pl.ds(ks, TK)], p)
        return mn, acc

    def step(j, cur_ref, nxt_ref, carry, diag):
        new = []
        for h in heads:
            smax_nxt = scores(j + 1, nxt_ref, h) if nxt_ref is not None else carry[h][2]
            new.append(softmax_pv(j, cur_ref, h, carry[h], diag) + (smax_nxt,))
        return tuple(new)

    def pair(t, carry):
        carry = step(2 * t, sa_ref, sb_ref, carry, None)
        return step(2 * t + 1, sb_ref, sa_ref, carry, None)

    carry = tuple((jnp.full((1, TQ), -1e30, F32), jnp.zeros((VT_ROWS, TQ), F32), scores(0, sa_ref, h))
                  for h in heads)
    carry = lax.fori_loop(0, n_pairs, pair, carry)
    carry = step(n_full, sa_ref, sb_ref, carry, 0)
    carry = step(n_full + 1, sb_ref, None, carry, 1)
    o_ref[...] = jnp.concatenate(
        [acc[0:FOX_HEAD_DIM] / acc[FOX_HEAD_DIM:FOX_HEAD_DIM + 1] for _, acc, _ in carry], axis=0).astype(BF16)


def _fox(q, k, vt, B, S):
    TQ = TQ_FOX
    nq = S // TQ
    T = B * S
    return pl.pallas_call(
        _fox_kernel,
        grid=(B, nq),
        in_specs=[pl.BlockSpec((TQ, D_QK), lambda b, i: (b * nq + i, 0)),
                  pl.BlockSpec((S, D_QK), lambda b, i: (b, 0)),
                  pl.BlockSpec((FOX_HEADS * VT_ROWS, S), lambda b, i: (0, b))],
        out_specs=pl.BlockSpec((D_FOX, TQ), lambda b, i: (0, b * nq + i)),
        out_shape=jax.ShapeDtypeStruct((D_FOX, T), BF16),
        scratch_shapes=[pltpu.VMEM((FOX_HEADS, TK_FOX, TQ), F32), pltpu.VMEM((FOX_HEADS, TK_FOX, TQ), F32)],
        compiler_params=pltpu.CompilerParams(dimension_semantics=("arbitrary", "arbitrary"),
                                             vmem_limit_bytes=VMEM_LIMIT),
        name="fox",
    )(q, k, vt)


def _outproj_kernel(x_ref, ys_ref, yft_ref, wo1_ref, wo2_ref, g_ref, wr_ref, br_ref,
                    h1_ref, xrow_ref, meta_ref, cnt_ref, carry_ref):
    TM = x_ref.shape[0]
    i = pl.program_id(0)

    @pl.when(i == 0)
    def _():
        carry_ref[...] = jnp.zeros(carry_ref.shape, F32)

    h1 = (x_ref[...] + _dot(ys_ref[...], wo1_ref[...])
          + lax.dot_general(yft_ref[...], wo2_ref[...], _TN, preferred_element_type=F32))
    h1_ref[...] = h1
    hn = _rms(h1, g_ref[...])
    h_hi, h_lo = _split_bf16(hn, 2)
    lg = lax.dot_general(wr_ref[...], h_hi, _NT, preferred_element_type=F32)
    logits = (lg[0:ROUTE_ROWS] + lg[ROUTE_ROWS:]
              + lax.dot_general(wr_ref[0:ROUTE_ROWS, :], h_lo, _NT, preferred_element_type=F32)
              + br_ref[:, 0:1])
    row = lax.broadcasted_iota(I32, (ROUTE_ROWS, TM), 0).astype(F32)
    ninf = -jnp.inf
    far = float(ROUTE_ROWS)
    gl = jnp.where(row < N_GROUPS, logits, ninf)
    gmax = jnp.max(gl, axis=0, keepdims=True)
    p_grp = 1.0 / jnp.sum(jnp.exp(gl - gmax), axis=0, keepdims=True)
    g_idx = jnp.min(jnp.where(gl == gmax, row, far), axis=0, keepdims=True)
    e0 = N_GROUPS + EXPERTS_PER_GROUP * g_idx
    el = jnp.where((row >= e0) & (row < e0 + EXPERTS_PER_GROUP), logits, ninf)
    v1 = jnp.max(el, axis=0, keepdims=True)
    i1 = jnp.min(jnp.where(el == v1, row, far), axis=0, keepdims=True)
    el2 = jnp.where(row == i1, ninf, el)
    v2 = jnp.max(el2, axis=0, keepdims=True)
    i2 = jnp.min(jnp.where(el2 == v2, row, far), axis=0, keepdims=True)
    e2 = jnp.exp(v2 - v1)
    w1 = p_grp / (1.0 + e2)
    w2 = p_grp * e2 / (1.0 + e2)
    l1 = i1 - e0
    l2 = i2 - e0
    first = l1 < l2
    a = jnp.where(first, l1, l2)
    b = jnp.where(first, l2, l1)
    wa = jnp.where(first, w1, w2)
    wb = jnp.where(first, w2, w1)
    pair = a * (5.0 - a) * 0.5 + b - 1.0
    bucket = g_idx * N_PAIRS + pair

    onehot = (row == bucket).astype(BF16)
    ri = lax.broadcasted_iota(I32, (TM, TM), 0)
    ci = lax.broadcasted_iota(I32, (TM, TM), 1)
    incl = _dot(onehot, (ri <= ci).astype(BF16))
    oh = onehot.astype(F32)
    rank = jnp.sum((incl - oh + carry_ref[:, 0:1]) * oh, axis=0, keepdims=True)
    carry_ref[...] = carry_ref[...] + jnp.broadcast_to(incl[:, TM - 1:TM], carry_ref.shape)
    cnt_ref[...] = carry_ref[...]

    r8 = lax.broadcasted_iota(I32, (SUBLANES, TM), 0)
    meta_ref[...] = jnp.where(r8 == 0, bucket, jnp.where(r8 == 1, rank, 0.0))
    wrows = jnp.where(r8 == 0, wa, jnp.where(r8 == 1, wb, 0.0))
    eye = (lax.broadcasted_iota(I32, (SUBLANES, LANES), 0)
           == lax.broadcasted_iota(I32, (SUBLANES, LANES), 1)).astype(BF16)
    wcols = None
    for piece in _split_bf16(wrows, 3):
        t = lax.dot_general(piece, eye, _TN, preferred_element_type=F32)
        wcols = t if wcols is None else wcols + t
    _to_token_tiles(xrow_ref, hn, TOK_PITCH)
    xrow_ref[pl.ds(TOK_ROWS, TM, stride=TOK_PITCH), :] = wcols
    for s in range(TOK_ROWS + 1, TOK_PITCH):
        xrow_ref[pl.ds(s, TM, stride=TOK_PITCH), :] = jnp.zeros((TM, LANES), F32)


def _out_proj(x2, yssd, yfoxt, wo1, wo2, g_ffn, wr, br):
    T = x2.shape[0]
    tm = TM_PROJ
    row = lambda i: (i, 0)
    const = lambda i: (0, 0)
    return pl.pallas_call(
        _outproj_kernel,
        grid=(T // tm,),
        in_specs=[pl.BlockSpec((tm, D_MODEL), row),
                  pl.BlockSpec((tm, D_SSD), row),
                  pl.BlockSpec((D_FOX, tm), lambda i: (0, i)),
                  pl.BlockSpec(wo1.shape, const),
                  pl.BlockSpec(wo2.shape, const),
                  pl.BlockSpec((1, D_MODEL), const),
                  pl.BlockSpec(wr.shape, const),
                  pl.BlockSpec(br.shape, const)],
        out_specs=[pl.BlockSpec((tm, D_MODEL), row),
                   pl.BlockSpec((tm * TOK_PITCH, LANES), row),
                   pl.BlockSpec((SUBLANES, tm), lambda i: (0, i)),
                   pl.BlockSpec((ROUTE_ROWS, LANES), const)],
        out_shape=[jax.ShapeDtypeStruct((T, D_MODEL), F32),
                   jax.ShapeDtypeStruct((T * TOK_PITCH, LANES), F32),
                   jax.ShapeDtypeStruct((SUBLANES, T), F32),
                   jax.ShapeDtypeStruct((ROUTE_ROWS, LANES), F32)],
        scratch_shapes=[pltpu.VMEM((ROUTE_ROWS, LANES), F32)],
        compiler_params=pltpu.CompilerParams(dimension_semantics=("arbitrary",),
                                             vmem_limit_bytes=VMEM_LIMIT),
        name="out_proj_router",
    )(x2, yssd, yfoxt, wo1, wo2, g_ffn, wr, br)


def _dispatch_kernel(ztile_ref, pos_ref, src_ref, dst_hbm, zero_ref, sem, zsem):
    TD = pos_ref.shape[-1]
    tile_rows = TM_MOE * TOK_PITCH

    @pl.when(pl.program_id(0) == 0)
    def _():
        zero_ref[...] = jnp.zeros(zero_ref.shape, F32)

        def zcopy(e):
            dst = pl.multiple_of(ztile_ref[e] * tile_rows, tile_rows)
            return pltpu.make_async_copy(zero_ref, dst_hbm.at[pl.ds(dst, tile_rows)], zsem)

        def zstart(e, c):
            @pl.when(ztile_ref[e] >= 0)
            def _():
                zcopy(e).start()
            return c

        def zwait(e, c):
            @pl.when(ztile_ref[e] >= 0)
            def _():
                zcopy(e).wait()
            return c

        lax.fori_loop(0, ztile_ref.shape[0], zstart, 0)
        lax.fori_loop(0, ztile_ref.shape[0], zwait, 0)

    def copy(r):
        dst = pl.multiple_of(pos_ref[0, r] * TOK_PITCH, 4)
        src = pl.multiple_of(r * TOK_PITCH, 4)
        return pltpu.make_async_copy(src_ref.at[pl.ds(src, TOK_PITCH)], dst_hbm.at[pl.ds(dst, TOK_PITCH)], sem)

    def start(rb, c):
        for u in range(DMA_ISSUE_UNROLL):
            copy(DMA_ISSUE_UNROLL * rb + u).start(priority=u % 2)
        return c

    lax.fori_loop(0, TD // DMA_ISSUE_UNROLL, start, 0)

    def wait(r, c):
        copy(r).wait()
        return c

    lax.fori_loop(0, TD, wait, 0, unroll=DMA_WAIT_UNROLL)


def _dispatch(ztiles, pos, xrow, n_rows):
    T = pos.shape[0]
    td = TD_DISP
    pos3 = pos.reshape(T // td, 1, td)
    grid_spec = pltpu.PrefetchScalarGridSpec(
        num_scalar_prefetch=1,
        grid=(T // td,),
        in_specs=[pl.BlockSpec((None, 1, td), lambda i, zt: (i, 0, 0), memory_space=pltpu.SMEM),
                  pl.BlockSpec((td * TOK_PITCH, LANES), lambda i, zt: (i, 0))],
        out_specs=pl.BlockSpec(memory_space=pl.ANY),
        scratch_shapes=[pltpu.VMEM((TM_MOE * TOK_PITCH, LANES), F32),
                        pltpu.SemaphoreType.DMA(()), pltpu.SemaphoreType.DMA(())],
    )
    return pl.pallas_call(
        _dispatch_kernel,
        grid_spec=grid_spec,
        out_shape=jax.ShapeDtypeStruct((n_rows * TOK_PITCH, LANES), F32),
        compiler_params=pltpu.CompilerParams(dimension_semantics=("arbitrary",),
                                             vmem_limit_bytes=VMEM_LIMIT),
        name="dispatch",
    )(ztiles, pos3, xrow)


def _experts_kernel(ea_ref, eb_ref, blk_ref, valid_ref,
                    xs_ref, wga_ref, wua_ref, wda_ref, wgb_ref, wub_ref, wdb_ref, ys_ref):
    del ea_ref, eb_ref, blk_ref
    i = pl.program_id(0)
    TM = xs_ref.shape[0] // TOK_PITCH

    @pl.when(valid_ref[i] > 0)
    def _():
        x = _from_token_tiles(xs_ref, TM, TOK_PITCH).astype(BF16)
        w = xs_ref[pl.ds(TOK_ROWS, TM, stride=TOK_PITCH), :]

        def expert(wg_ref, wu_ref, wd_ref, wt):
            g = _dot(x, wg_ref[...])
            u = _dot(x, wu_ref[...])
            return _dot((g * _sigmoid(g) * u * wt).astype(BF16), wd_ref[...])

        y = expert(wga_ref, wua_ref, wda_ref, w[:, 0:1]) + expert(wgb_ref, wub_ref, wdb_ref, w[:, 1:2])
        _to_token_tiles(ys_ref, y, TOK_ROWS)

    @pl.when(valid_ref[i] == 0)
    def _():
        ys_ref[...] = jnp.zeros(ys_ref.shape, F32)


def _experts(ea, eb, blk, valid, xs, wg, wu, wd):
    tm = TM_MOE
    n_rows = xs.shape[0] // TOK_PITCH
    nt = n_rows // tm
    xmap = lambda i, ea, eb, blk, valid: (blk[i], 0)
    amap = lambda i, ea, eb, blk, valid: (ea[i], 0, 0)
    bmap = lambda i, ea, eb, blk, valid: (eb[i], 0, 0)
    up_spec = lambda m: pl.BlockSpec((None, D_MODEL, D_EXPERT), m)
    dn_spec = lambda m: pl.BlockSpec((None, D_EXPERT, D_MODEL), m)
    grid_spec = pltpu.PrefetchScalarGridSpec(
        num_scalar_prefetch=4,
        grid=(nt,),
        in_specs=[pl.BlockSpec((tm * TOK_PITCH, LANES), xmap),
                  up_spec(amap), up_spec(amap), dn_spec(amap),
                  up_spec(bmap), up_spec(bmap), dn_spec(bmap)],
        out_specs=pl.BlockSpec((tm * TOK_ROWS, LANES), lambda i, ea, eb, blk, valid: (i, 0)),
    )
    return pl.pallas_call(
        _experts_kernel,
        grid_spec=grid_spec,
        out_shape=jax.ShapeDtypeStruct((n_rows * TOK_ROWS, LANES), F32),
        compiler_params=pltpu.CompilerParams(dimension_semantics=("arbitrary",),
                                             vmem_limit_bytes=VMEM_LIMIT),
        name="experts",
    )(ea, eb, blk, valid, xs, wg, wu, wd, wg, wu, wd)


def _combine_kernel(pos_cur_ref, pos_nxt_ref, h1_ref, p_ref, wpp_ref, wpg_ref, gple_ref, gfin_ref,
                    ys_hbm, o_ref, buf_ref, sem):
    TM = h1_ref.shape[0]
    i = pl.program_id(0)
    n = pl.num_programs(0)
    slot = i % 2

    def copy(pos_ref, s, r):
        src = pl.multiple_of(pos_ref[0, r] * TOK_ROWS, TOK_ROWS)
        dst = pl.multiple_of(r * TOK_PITCH, 4)
        return pltpu.make_async_copy(ys_hbm.at[pl.ds(src, TOK_ROWS)], buf_ref.at[s, pl.ds(dst, TOK_ROWS)],
                                     sem.at[s])

    def gather(pos_ref, s):
        def start(rb, c):
            for u in range(DMA_ISSUE_UNROLL):
                copy(pos_ref, s, DMA_ISSUE_UNROLL * rb + u).start(priority=u % 2)
            return c
        lax.fori_loop(0, TM // DMA_ISSUE_UNROLL, start, 0)

    @pl.when(i == 0)
    def _():
        gather(pos_cur_ref, 0)

    @pl.when(i + 1 < n)
    def _():
        gather(pos_nxt_ref, 1 - slot)

    def wait(r, c):
        copy(pos_cur_ref, slot, r).wait()
        return c

    lax.fori_loop(0, TM, wait, 0, unroll=DMA_WAIT_UNROLL)

    h2 = h1_ref[...] + _from_token_tiles(buf_ref.at[slot], TM, TOK_PITCH)
    n3 = _rms(h2, gple_ref[...]).astype(BF16)
    gate = _sigmoid(_dot(n3, wpg_ref[...]))
    pp = _dot(p_ref[...].astype(BF16), wpp_ref[...])
    o_ref[...] = _rms(h2 + pp * gate, gfin_ref[...])


def _combine(pos, h1, p2, wpp, wpg, g_ple, g_final, ys):
    T = h1.shape[0]
    tm = TM_COMB
    nt = T // tm
    pos3 = pos.reshape(nt, 1, tm)
    row = lambda i: (i, 0)
    const = lambda i: (0, 0)
    return pl.pallas_call(
        _combine_kernel,
        grid=(nt,),
        in_specs=[pl.BlockSpec((None, 1, tm), lambda i: (i, 0, 0), memory_space=pltpu.SMEM),
                  pl.BlockSpec((None, 1, tm), lambda i: (jnp.minimum(i + 1, nt - 1), 0, 0),
                               memory_space=pltpu.SMEM),
                  pl.BlockSpec((tm, D_MODEL), row),
                  pl.BlockSpec((tm, D_PLE), row),
                  pl.BlockSpec(wpp.shape, const),
                  pl.BlockSpec(wpg.shape, const),
                  pl.BlockSpec((1, D_MODEL), const),
                  pl.BlockSpec((1, D_MODEL), const),
                  pl.BlockSpec(memory_space=pl.ANY)],
        out_specs=pl.BlockSpec((tm, D_MODEL), row),
        out_shape=jax.ShapeDtypeStruct((T, D_MODEL), F32),
        scratch_shapes=[pltpu.VMEM((2, tm * TOK_PITCH, LANES), F32),
                        pltpu.SemaphoreType.DMA((2,))],
        compiler_params=pltpu.CompilerParams(dimension_semantics=("arbitrary",),
                                             vmem_limit_bytes=VMEM_LIMIT),
        name="combine_ple",
    )(pos3, pos3, h1, p2, wpp, wpg, g_ple, g_final, ys)


def _tile_tables(counts, n_tiles):
    nt = (counts + TM_MOE - 1) // TM_MOE
    cum = jnp.cumsum(nt)
    off = (cum - nt) * TM_MOE
    total = cum[-1]
    tile = jnp.arange(n_tiles, dtype=I32)
    valid = (tile < total).astype(I32)
    blk = jnp.maximum(jnp.minimum(tile, total - 1), 0)
    tb = jnp.minimum(jnp.sum((cum[None, :] <= blk[:, None]).astype(I32), axis=1), N_BUCKETS - 1)
    pair_a = jnp.array([0, 0, 0, 1, 1, 2], I32)
    pair_b = jnp.array([1, 2, 3, 2, 3, 3], I32)
    grp = tb // N_PAIRS
    ea = grp * EXPERTS_PER_GROUP + pair_a[tb % N_PAIRS]
    eb = grp * EXPERTS_PER_GROUP + pair_b[tb % N_PAIRS]
    last = jnp.where(nt > 0, cum - 1, -1)
    tail = total + jnp.arange(N_BUCKETS, dtype=I32)
    ztiles = jnp.concatenate([last, jnp.where(tail < n_tiles, tail, -1)]).astype(I32)
    return off, ea, eb, blk, valid, ztiles


def kernel(x, p, w_in, conv_w, conv_b, dt_bias, a_log, d_skip, g_ssd, fox_fbias, w_out, g_mix, g_ffn,
           w_route_group, b_route_group, w_route_expert, b_route_expert, w_exp_gate, w_exp_up, w_exp_down,
           g_ple, w_ple_proj, w_ple_gate, g_final):
    B, S, _ = x.shape
    T = B * S
    assert S % L_SSD == 0 and S % TQ_FOX == 0 and S % TM_PROJ == 0 and T % TD_DISP == 0
    x2 = x.reshape(T, D_MODEL)
    p2 = p[0].reshape(T, D_PLE)

    wi = w_in[0]
    o_xbc = D_SSD
    o_dt = o_xbc + D_XBC
    o_q = o_dt + SSD_HEADS
    o_k = o_q + D_FOX
    o_v = o_k + D_FOX
    o_f = o_v + D_FOX
    wz = wi[:, 0:o_xbc].astype(BF16)
    wxbc = wi[:, o_xbc:o_dt].astype(BF16)
    wq = wi[:, o_q:o_k].astype(BF16)
    wk = wi[:, o_k:o_v].astype(BF16)
    wvt = wi[:, o_v:o_f].T.astype(BF16)
    w_dt = wi[:, o_dt:o_q]
    w_f = wi[:, o_f:o_f + FOX_HEADS]
    wsm = jnp.pad(jnp.concatenate([w_dt] + [w_f] * F_PARTS, axis=1),
                  ((0, 0), (0, LANES - SSD_HEADS - F_PARTS * FOX_HEADS))).astype(BF16)
    wsmt = w_dt.T.astype(BF16)
    lane_pad = lambda v: jnp.pad(v.reshape(1, -1), ((0, 0), (0, LANES - v.shape[-1])))
    colb = lane_pad(jnp.concatenate([dt_bias[0]] + [fox_fbias[0]] * F_PARTS))
    cola = lane_pad(a_log[0])
    rowb = jnp.broadcast_to(dt_bias[0][:, None], (SSD_HEADS, LANES))
    rowa = jnp.broadcast_to(a_log[0][:, None], (SSD_HEADS, LANES))
    dexp = jnp.repeat(d_skip[0], SSD_HEAD_DIM).reshape(1, D_SSD)
    expand = (jnp.arange(LANES)[:, None] == (jnp.arange(D_SSD) // SSD_HEAD_DIM)[None, :]).astype(BF16)
    col = jnp.arange(D_QK)
    head = col // LANES
    j_bias = col % LANES - jnp.where(head % 2 == 0, FOX_HEAD_DIM, 0)
    bias_col = (j_bias >= 0) & (j_bias < F_PARTS)
    qones = bias_col.astype(F32).reshape(1, D_QK)
    src_lane = SSD_HEADS + j_bias * FOX_HEADS + head
    place = (bias_col[None, :] & (jnp.arange(LANES)[:, None] == src_lane[None, :])).astype(BF16)
    wo = w_out[0].astype(BF16)
    wr = jnp.pad(jnp.concatenate([w_route_group[0], w_route_expert[0]], axis=1).T,
                 ((0, ROUTE_ROWS - N_GROUPS - N_EXPERTS), (0, 0)))
    wr_hi = wr.astype(BF16)
    wr = jnp.concatenate([wr_hi, (wr - wr_hi.astype(F32)).astype(BF16)], axis=0)
    br = jnp.broadcast_to(jnp.pad(jnp.concatenate([b_route_group[0], b_route_expert[0]]),
                                  (0, ROUTE_ROWS - N_GROUPS - N_EXPERTS))[:, None], (ROUTE_ROWS, LANES))
    wg = w_exp_gate[0].reshape(N_EXPERTS, D_MODEL, D_EXPERT).astype(BF16)
    wu = w_exp_up[0].reshape(N_EXPERTS, D_MODEL, D_EXPERT).astype(BF16)
    wd = w_exp_down[0].reshape(N_EXPERTS, D_EXPERT, D_MODEL).astype(BF16)

    z, xbc, q, k, vt, sm, smt = _in_proj(x2, g_mix[0].reshape(1, -1), wz, wxbc, wq, wk, wvt, wsm, wsmt,
                                         colb, qones, place, S)
    yssd = _ssd(xbc, z, sm, smt, conv_w[0].reshape(SSD_CONV, D_XBC), conv_b[0].reshape(1, -1),
                colb, cola, rowb, rowa, dexp, g_ssd[0].reshape(1, -1), expand, B, S)
    yfoxt = _fox(q, k, vt, B, S)
    h1, xrow, meta, cnt = _out_proj(x2, yssd, yfoxt, wo[0:D_SSD], wo[D_SSD:], g_ffn[0].reshape(1, -1), wr, br)

    n_rows = T + N_BUCKETS * TM_MOE
    bucket = meta[0].astype(I32)
    rank = meta[1].astype(I32)
    off, ea, eb, blk, valid, ztiles = _tile_tables(cnt[0:N_BUCKETS, 0].astype(I32), n_rows // TM_MOE)
    pos = rank
    for b in range(N_BUCKETS):
        pos = pos + jnp.where(bucket == b, off[b], 0)

    xs = _dispatch(ztiles, pos, xrow, n_rows)
    ys = _experts(ea, eb, blk, valid, xs, wg, wu, wd)
    out = _combine(pos, h1, p2, w_ple_proj[0].astype(BF16), w_ple_gate[0].astype(BF16),
                   g_ple[0].reshape(1, -1), g_final.reshape(1, -1), ys)
    return out.reshape(B, S, D_MODEL)
```

```python
import functools

import jax
import jax.numpy as jnp
from jax import lax
from jax.experimental import pallas as pl
from jax.experimental.pallas import tpu as pltpu

F32 = jnp.float32
BF16 = jnp.bfloat16
I32 = jnp.int32

D_MODEL = 1024
SSD_HEADS = 8
SSD_HEAD_DIM = 64
SSD_GROUPS = 2
SSD_STATE = 128
SSD_CONV = 4
D_SSD = SSD_HEADS * SSD_HEAD_DIM
D_XBC = D_SSD + 2 * SSD_GROUPS * SSD_STATE
FOX_HEADS = 8
FOX_HEAD_DIM = 64
D_FOX = FOX_HEADS * FOX_HEAD_DIM
N_GROUPS = 4
EXPERTS_PER_GROUP = 4
N_EXPERTS = N_GROUPS * EXPERTS_PER_GROUP
N_PAIRS = 6
N_BUCKETS = N_GROUPS * N_PAIRS
ROUTE_ROWS = 32
D_EXPERT = 512
D_PLE = 256
EPS = 1e-6
LOG2E = 1.4426950408889634
LANES = 128
SUBLANES = 8
TOK_ROWS = D_MODEL // LANES
TOK_PITCH = 12
D_QK = FOX_HEADS * LANES
F_PARTS = 3
VT_ROWS = FOX_HEAD_DIM + 16
VMEM_LIMIT = 52 * 1024 * 1024

TM_PROJ = 512
L_SSD = 256
TQ_FOX = 512
TK_FOX = 256
TM_MOE = 256
TD_DISP = 1024
TM_COMB = 512
DMA_WAIT_UNROLL = 16
DMA_ISSUE_UNROLL = 8

_NT = (((1,), (1,)), ((), ()))
_TN = (((0,), (0,)), ((), ()))


def _rms(x, g):
    ms = jnp.mean(x * x, axis=-1, keepdims=True)
    return x * lax.rsqrt(ms + EPS) * g


def _sigmoid(x):
    return 1.0 / (1.0 + jnp.exp(-x))


def _softplus(x):
    return jnp.maximum(x, 0.0) + jnp.log(1.0 + jnp.exp(-jnp.abs(x)))


def _split_bf16(x, parts):
    out = []
    r = x
    for _ in range(parts):
        h = r.astype(BF16)
        out.append(h)
        r = r - h.astype(F32)
    return out


def _dot(a, b):
    return jnp.dot(a, b, preferred_element_type=F32)


def _dot_exact(a01, x, parts):
    acc = None
    for piece in _split_bf16(x, parts):
        t = _dot(a01, piece)
        acc = t if acc is None else acc + t
    return acc


def _to_token_tiles(ref, x, stride, offset=0):
    m = x.shape[0]
    for s in range(TOK_ROWS):
        ref[pl.ds(offset + s, m, stride=stride), :] = x[:, s * LANES:(s + 1) * LANES]


def _from_token_tiles(ref, m, stride, offset=0):
    return jnp.concatenate([ref[pl.ds(offset + s, m, stride=stride), :] for s in range(TOK_ROWS)], axis=1)


def _inproj_kernel(x_ref, g_ref, wz_ref, wxbc_ref, wq_ref, wk_ref, wvt_ref, wsm_ref, wsmt_ref,
                   fbias_ref, qones_ref, place_ref,
                   z_ref, xbc_ref, q_ref, k_ref, vt_ref, sm_ref, smt_ref, fcar_ref, *, tiles_per_seq):
    TM = x_ref.shape[0]

    @pl.when(pl.program_id(0) % tiles_per_seq == 0)
    def _():
        fcar_ref[...] = jnp.zeros(fcar_ref.shape, F32)

    hn = _rms(x_ref[...], g_ref[...]).astype(BF16)
    z_ref[...] = _dot(hn, wz_ref[...]).astype(BF16)
    xbc_ref[...] = _dot(hn, wxbc_ref[...]).astype(BF16)
    vt = lax.dot_general(wvt_ref[...], hn, _NT, preferred_element_type=F32).astype(BF16)
    ones_rows = jnp.ones((VT_ROWS - FOX_HEAD_DIM, TM), BF16)
    for h in range(FOX_HEADS):
        vt_ref[h * VT_ROWS:h * VT_ROWS + FOX_HEAD_DIM, :] = vt[h * FOX_HEAD_DIM:(h + 1) * FOX_HEAD_DIM, :]
        vt_ref[h * VT_ROWS + FOX_HEAD_DIM:(h + 1) * VT_ROWS, :] = ones_rows
    sm_ref[...] = _dot(hn, wsm_ref[...])
    smt = lax.dot_general(wsmt_ref[...], hn, _NT, preferred_element_type=F32)
    smt_ref[...] = smt[0:SSD_HEADS]

    logf = -_softplus(-(smt[SSD_HEADS:] + fbias_ref[:, 0:1]))
    ri = lax.broadcasted_iota(I32, (TM, TM), 0)
    ci = lax.broadcasted_iota(I32, (TM, TM), 1)
    triu = (ri <= ci).astype(BF16)
    fcum = fcar_ref[:, 0:1]
    for piece in _split_bf16(logf, 3):
        fcum = fcum + _dot(piece, triu)
    fcar_ref[...] = jnp.broadcast_to(fcum[:, TM - 1:TM], fcar_ref.shape)
    pieces = _split_bf16(fcum * (-LOG2E), F_PARTS)
    stacked = jnp.concatenate([p.astype(F32) for p in pieces] + [jnp.zeros((SUBLANES, TM), F32)],
                              axis=0).astype(BF16)
    kbias = lax.dot_general(stacked, place_ref[...], _TN, preferred_element_type=F32)
    lane = lax.broadcasted_iota(I32, (TM, LANES), 1)
    qc = _dot(hn, wq_ref[...]) * (FOX_HEAD_DIM ** -0.5 * LOG2E)
    kc = _dot(hn, wk_ref[...])
    for h in range(FOX_HEADS):
        pair = slice((h // 2) * LANES, (h // 2 + 1) * LANES)
        tile = slice(h * LANES, (h + 1) * LANES)
        own = (lane < FOX_HEAD_DIM) if h % 2 == 0 else (lane >= FOX_HEAD_DIM)
        q_ref[:, tile] = jnp.where(own, qc[:, pair], qones_ref[:, tile]).astype(BF16)
        k_ref[:, tile] = jnp.where(own, kc[:, pair], kbias[:, tile]).astype(BF16)


def _in_proj(x2, g_mix, wz, wxbc, wq, wk, wvt, wsm, wsmt, fbias, qones, place, S):
    T = x2.shape[0]
    tm = TM_PROJ
    row = lambda i: (i, 0)
    const = lambda i: (0, 0)
    full = lambda a: pl.BlockSpec(a.shape, const)
    return pl.pallas_call(
        functools.partial(_inproj_kernel, tiles_per_seq=S // tm),
        grid=(T // tm,),
        in_specs=[pl.BlockSpec((tm, D_MODEL), row), full(g_mix), full(wz), full(wxbc), full(wq), full(wk),
                  full(wvt), full(wsm), full(wsmt), full(fbias), full(qones), full(place)],
        out_specs=[pl.BlockSpec((tm, D_SSD), row),
                   pl.BlockSpec((tm, D_XBC), row),
                   pl.BlockSpec((tm, D_QK), row),
                   pl.BlockSpec((tm, D_QK), row),
                   pl.BlockSpec((FOX_HEADS * VT_ROWS, tm), lambda i: (0, i)),
                   pl.BlockSpec((tm, LANES), row),
                   pl.BlockSpec((SSD_HEADS, tm), lambda i: (0, i))],
        out_shape=[jax.ShapeDtypeStruct((T, D_SSD), BF16),
                   jax.ShapeDtypeStruct((T, D_XBC), BF16),
                   jax.ShapeDtypeStruct((T, D_QK), BF16),
                   jax.ShapeDtypeStruct((T, D_QK), BF16),
                   jax.ShapeDtypeStruct((FOX_HEADS * VT_ROWS, T), BF16),
                   jax.ShapeDtypeStruct((T, LANES), F32),
                   jax.ShapeDtypeStruct((SSD_HEADS, T), F32)],
        scratch_shapes=[pltpu.VMEM((SUBLANES, LANES), F32)],
        compiler_params=pltpu.CompilerParams(dimension_semantics=("arbitrary",),
                                             vmem_limit_bytes=VMEM_LIMIT),
        name="in_proj",
    )(x2, g_mix, wz, wxbc, wq, wk, wvt, wsm, wsmt, fbias, qones, place)


def _ssd_kernel(xbc_ref, z_ref, sm_ref, smt_ref, convw_ref, convb_ref, colb_ref, cola_ref,
                rowb_ref, rowa_ref, dexp_ref, gssd_ref, expand_ref,
                y_ref, tail_ref, st_ref):
    L = xbc_ref.shape[0]
    R = SSD_HEADS // SSD_GROUPS
    TAIL = tail_ref.shape[0]
    c = pl.program_id(1)

    @pl.when(c == 0)
    def _():
        tail_ref[...] = jnp.zeros(tail_ref.shape, BF16)
        st_ref[...] = jnp.zeros(st_ref.shape, F32)

    ri = lax.broadcasted_iota(I32, (L, L), 0)
    ci = lax.broadcasted_iota(I32, (L, L), 1)
    causal = ri >= ci
    tril = causal.astype(BF16)
    triu = (ri <= ci).astype(BF16)

    zf = z_ref[...].astype(F32)
    zgate = zf * _sigmoid(zf)

    x_bf = xbc_ref[...]
    tail = tail_ref[...]
    hr = lax.broadcasted_iota(I32, (SUBLANES, TAIL), 0)
    hc = lax.broadcasted_iota(I32, (SUBLANES, TAIL), 1)
    conv = convb_ref[...] + convw_ref[SSD_CONV - 1:SSD_CONV, :] * x_bf.astype(F32)
    head = None
    for d in range(1, SSD_CONV):
        w_d = convw_ref[SSD_CONV - 1 - d:SSD_CONV - d, :]
        conv = conv + w_d * _dot((ri - d == ci).astype(BF16), x_bf)
        t = w_d * _dot((hc == hr + (TAIL - d)).astype(BF16), tail)
        head = t if head is None else head + t
    conv = jnp.concatenate([conv[0:SUBLANES] + head, conv[SUBLANES:]], axis=0)
    tail_ref[...] = xbc_ref[L - TAIL:L, :]
    xc = conv * _sigmoid(conv)
    xs = xc[:, 0:D_SSD]
    xs_bf = xs.astype(BF16)
    bm = xc[:, D_SSD:D_SSD + SSD_GROUPS * SSD_STATE].astype(BF16)
    cm = xc[:, D_SSD + SSD_GROUPS * SSD_STATE:].astype(BF16)

    dtr = _softplus(smt_ref[...] + rowb_ref[:, 0:1])
    adt_r = dtr * (-LOG2E * jnp.exp(rowa_ref[:, 0:1]))
    csr = None
    for part in _split_bf16(adt_r, 3):
        t = _dot(part, triu)
        csr = t if csr is None else csr + t

    dtc = _softplus(sm_ref[...] + colb_ref[...])
    adt_c = dtc * (-LOG2E * jnp.exp(cola_ref[...]))
    cs_c = _dot_exact(tril, adt_c, 3)
    cs_last = cs_c[L - 1:L, :]
    e1 = jnp.exp2(cs_c)
    wst = dtc * jnp.exp2(cs_last - cs_c)
    ex = expand_ref[...]
    p1 = _split_bf16(e1, 2)
    e1x = _dot(p1[0], ex) + _dot(p1[1], ex)
    p2 = _split_bf16(wst, 2)
    wstx = _dot(p2[0], ex) + _dot(p2[1], ex)

    xw = (xs * wstx).astype(BF16)
    dec8 = jnp.broadcast_to(jnp.exp2(csr[:, L - 1:L]), (SSD_HEADS, SSD_STATE))

    ydiag = []
    yoff = []
    for g in range(SSD_GROUPS):
        bg = bm[:, g * SSD_STATE:(g + 1) * SSD_STATE]
        cg = cm[:, g * SSD_STATE:(g + 1) * SSD_STATE]
        gmat = lax.dot_general(cg, bg, _NT, preferred_element_type=F32)
        s_old = st_ref[g]
        yoff.append(lax.dot_general(cg, s_old.astype(BF16), _NT, preferred_element_type=F32))
        for j in range(R):
            h = g * R + j
            seg = cs_c[:, h:h + 1] - csr[h:h + 1, :]
            lm = jnp.exp2(jnp.where(causal, seg, -jnp.inf))
            m = (gmat * lm * dtr[h:h + 1, :]).astype(BF16)
            ydiag.append(_dot(m, xs_bf[:, h * SSD_HEAD_DIM:(h + 1) * SSD_HEAD_DIM]))
        upd = lax.dot_general(xw[:, g * R * SSD_HEAD_DIM:(g + 1) * R * SSD_HEAD_DIM], bg, _TN,
                              preferred_element_type=F32)
        dec = jnp.concatenate(
            [jnp.broadcast_to(dec8[g * R + j:g * R + j + 1, :], (SSD_HEAD_DIM, SSD_STATE)) for j in range(R)],
            axis=0)
        st_ref[g] = dec * s_old + upd

    y = jnp.concatenate(ydiag, axis=1) + e1x * jnp.concatenate(yoff, axis=1) + dexp_ref[...] * xs
    y_ref[...] = _rms(y * zgate, gssd_ref[...]).astype(BF16)


def _ssd(xbc, z, sm, smt, convw, convb, colb, cola, rowb, rowa, dexp, gssd, expand, B, S):
    L = L_SSD
    nc = S // L
    T = B * S
    row = lambda b, c: (b * nc + c, 0)
    const = lambda b, c: (0, 0)
    full = lambda a: pl.BlockSpec(a.shape, const)
    return pl.pallas_call(
        _ssd_kernel,
        grid=(B, nc),
        in_specs=[pl.BlockSpec((L, D_XBC), row),
                  pl.BlockSpec((L, D_SSD), row),
                  pl.BlockSpec((L, LANES), row),
                  pl.BlockSpec((SSD_HEADS, L), lambda b, c: (0, b * nc + c)),
                  full(convw), full(convb), full(colb), full(cola), full(rowb), full(rowa),
                  full(dexp), full(gssd), full(expand)],
        out_specs=pl.BlockSpec((L, D_SSD), row),
        out_shape=jax.ShapeDtypeStruct((T, D_SSD), BF16),
        scratch_shapes=[pltpu.VMEM((16, D_XBC), BF16),
                        pltpu.VMEM((SSD_GROUPS, (SSD_HEADS // SSD_GROUPS) * SSD_HEAD_DIM, SSD_STATE), F32)],
        compiler_params=pltpu.CompilerParams(dimension_semantics=("arbitrary", "arbitrary"),
                                             vmem_limit_bytes=VMEM_LIMIT),
        name="ssd",
    )(xbc, z, sm, smt, convw, convb, colb, cola, rowb, rowa, dexp, gssd, expand)


def _fox_kernel(q_ref, k_ref, vt_ref, o_ref, sa_ref, sb_ref):
    TQ = q_ref.shape[0]
    TK = TK_FOX
    assert TQ == 2 * TK
    n_pairs = pl.program_id(1)
    n_full = 2 * n_pairs
    key_idx = lax.broadcasted_iota(I32, (TK, TQ), 0)
    qry_idx = lax.broadcasted_iota(I32, (TK, TQ), 1)
    heads = range(FOX_HEADS)
    qs = [q_ref[:, h * LANES:(h + 1) * LANES] for h in heads]

    def masked(s, diag):
        return s if diag is None else jnp.where(key_idx + diag * TK <= qry_idx, s, -jnp.inf)

    def scores(j, s_ref, h):
        ks = pl.multiple_of(j * TK, TK)
        s = lax.dot_general(k_ref[pl.ds(ks, TK), h * LANES:(h + 1) * LANES], qs[h], _NT,
                            preferred_element_type=F32)
        s_ref[h] = s
        return jnp.max(s, axis=0, keepdims=True)

    def softmax_pv(j, s_ref, h, state, diag):
        ks = pl.multiple_of(j * TK, TK)
        m, acc, smax = state
        s = s_ref[h]
        if diag is not None:
            s = jnp.where(key_idx + diag * TK <= qry_idx, s, -jnp.inf)
            smax = jnp.max(s, axis=0, keepdims=True)
        mn = jnp.maximum(m, smax)
        p = jnp.exp2(s - mn).astype(BF16)
        acc = jnp.exp2(m - mn) * acc + _dot(vt_ref[h * VT_ROWS:(h + 1) * VT_ROWS, pl.ds(ks, TK)], p)
        return mn, acc

    def step(j, cur_ref, nxt_ref, carry, diag):
        new = []
        for h in heads:
            smax_nxt = scores(j + 1, nxt_ref, h) if nxt_ref is not None else carry[h][2]
            new.append(softmax_pv(j, cur_ref, h, carry[h], diag) + (smax_nxt,))
        return tuple(new)

    def pair(t, carry):
        carry = step(2 * t, sa_ref, sb_ref, carry, None)
        return step(2 * t + 1, sb_ref, sa_ref, carry, None)

    carry = tuple((jnp.full((1, TQ), -1e30, F32), jnp.zeros((VT_ROWS, TQ), F32), scores(0, sa_ref, h))
                  for h in heads)
    carry = lax.fori_loop(0, n_pairs, pair, carry)
    carry = step(n_full, sa_ref, sb_ref, carry, 0)
    carry = step(n_full + 1, sb_ref, None, carry, 1)
    o_ref[...] = jnp.concatenate(
        [acc[0:FOX_HEAD_DIM] / acc[FOX_HEAD_DIM:FOX_HEAD_DIM + 1] for _, acc, _ in carry], axis=0).astype(BF16)


def _fox(q, k, vt, B, S):
    TQ = TQ_FOX
    nq = S // TQ
    T = B * S
    return pl.pallas_call(
        _fox_kernel,
        grid=(B, nq),
        in_specs=[pl.BlockSpec((TQ, D_QK), lambda b, i: (b * nq + i, 0)),
                  pl.BlockSpec((S, D_QK), lambda b, i: (b, 0)),
                  pl.BlockSpec((FOX_HEADS * VT_ROWS, S), lambda b, i: (0, b))],
        out_specs=pl.BlockSpec((D_FOX, TQ), lambda b, i: (0, b * nq + i)),
        out_shape=jax.ShapeDtypeStruct((D_FOX, T), BF16),
        scratch_shapes=[pltpu.VMEM((FOX_HEADS, TK_FOX, TQ), F32), pltpu.VMEM((FOX_HEADS, TK_FOX, TQ), F32)],
        compiler_params=pltpu.CompilerParams(dimension_semantics=("arbitrary", "arbitrary"),
                                             vmem_limit_bytes=VMEM_LIMIT),
        name="fox",
    )(q, k, vt)


def _outproj_kernel(x_ref, ys_ref, yft_ref, wo1_ref, wo2_ref, g_ref, wr_ref, br_ref,
                    h1_ref, xrow_ref, meta_ref, cnt_ref, carry_ref):
    TM = x_ref.shape[0]
    i = pl.program_id(0)

    @pl.when(i == 0)
    def _():
        carry_ref[...] = jnp.zeros(carry_ref.shape, F32)

    h1 = (x_ref[...] + _dot(ys_ref[...], wo1_ref[...])
          + lax.dot_general(yft_ref[...], wo2_ref[...], _TN, preferred_element_type=F32))
    h1_ref[...] = h1
    hn = _rms(h1, g_ref[...])
    h_hi, h_lo = _split_bf16(hn, 2)
    lg = lax.dot_general(wr_ref[...], h_hi, _NT, preferred_element_type=F32)
    logits = (lg[0:ROUTE_ROWS] + lg[ROUTE_ROWS:]
              + lax.dot_general(wr_ref[0:ROUTE_ROWS, :], h_lo, _NT, preferred_element_type=F32)
              + br_ref[:, 0:1])
    row = lax.broadcasted_iota(I32, (ROUTE_ROWS, TM), 0).astype(F32)
    ninf = -jnp.inf
    far = float(ROUTE_ROWS)
    gl = jnp.where(row < N_GROUPS, logits, ninf)
    gmax = jnp.max(gl, axis=0, keepdims=True)
    p_grp = 1.0 / jnp.sum(jnp.exp(gl - gmax), axis=0, keepdims=True)
    g_idx = jnp.min(jnp.where(gl == gmax, row, far), axis=0, keepdims=True)
    e0 = N_GROUPS + EXPERTS_PER_GROUP * g_idx
    el = jnp.where((row >= e0) & (row < e0 + EXPERTS_PER_GROUP), logits, ninf)
    v1 = jnp.max(el, axis=0, keepdims=True)
    i1 = jnp.min(jnp.where(el == v1, row, far), axis=0, keepdims=True)
    el2 = jnp.where(row == i1, ninf, el)
    v2 = jnp.max(el2, axis=0, keepdims=True)
    i2 = jnp.min(jnp.where(el2 == v2, row, far), axis=0, keepdims=True)
    e2 = jnp.exp(v2 - v1)
    w1 = p_grp / (1.0 + e2)
    w2 = p_grp * e2 / (1.0 + e2)
    l1 = i1 - e0
    l2 = i2 - e0
    first = l1 < l2
    a = jnp.where(first, l1, l2)
    b = jnp.where(first, l2, l1)
    wa = jnp.where(first, w1, w2)
    wb = jnp.where(first, w2, w1)
    pair = a * (5.0 - a) * 0.5 + b - 1.0
    bucket = g_idx * N_PAIRS + pair

    onehot = (row == bucket).astype(BF16)
    ri = lax.broadcasted_iota(I32, (TM, TM), 0)
    ci = lax.broadcasted_iota(I32, (TM, TM), 1)
    incl = _dot(onehot, (ri <= ci).astype(BF16))
    oh = onehot.astype(F32)
    rank = jnp.sum((incl - oh + carry_ref[:, 0:1]) * oh, axis=0, keepdims=True)
    carry_ref[...] = carry_ref[...] + jnp.broadcast_to(incl[:, TM - 1:TM], carry_ref.shape)
    cnt_ref[...] = carry_ref[...]

    r8 = lax.broadcasted_iota(I32, (SUBLANES, TM), 0)
    meta_ref[...] = jnp.where(r8 == 0, bucket, jnp.where(r8 == 1, rank, 0.0))
    wrows = jnp.where(r8 == 0, wa, jnp.where(r8 == 1, wb, 0.0))
    eye = (lax.broadcasted_iota(I32, (SUBLANES, LANES), 0)
           == lax.broadcasted_iota(I32, (SUBLANES, LANES), 1)).astype(BF16)
    wcols = None
    for piece in _split_bf16(wrows, 3):
        t = lax.dot_general(piece, eye, _TN, preferred_element_type=F32)
        wcols = t if wcols is None else wcols + t
    _to_token_tiles(xrow_ref, hn, TOK_PITCH)
    xrow_ref[pl.ds(TOK_ROWS, TM, stride=TOK_PITCH), :] = wcols
    for s in range(TOK_ROWS + 1, TOK_PITCH):
        xrow_ref[pl.ds(s, TM, stride=TOK_PITCH), :] = jnp.zeros((TM, LANES), F32)


def _out_proj(x2, yssd, yfoxt, wo1, wo2, g_ffn, wr, br):
    T = x2.shape[0]
    tm = TM_PROJ
    row = lambda i: (i, 0)
    const = lambda i: (0, 0)
    return pl.pallas_call(
        _outproj_kernel,
        grid=(T // tm,),
        in_specs=[pl.BlockSpec((tm, D_MODEL), row),
                  pl.BlockSpec((tm, D_SSD), row),
                  pl.BlockSpec((D_FOX, tm), lambda i: (0, i)),
                  pl.BlockSpec(wo1.shape, const),
                  pl.BlockSpec(wo2.shape, const),
                  pl.BlockSpec((1, D_MODEL), const),
                  pl.BlockSpec(wr.shape, const),
                  pl.BlockSpec(br.shape, const)],
        out_specs=[pl.BlockSpec((tm, D_MODEL), row),
                   pl.BlockSpec((tm * TOK_PITCH, LANES), row),
                   pl.BlockSpec((SUBLANES, tm), lambda i: (0, i)),
                   pl.BlockSpec((ROUTE_ROWS, LANES), const)],
        out_shape=[jax.ShapeDtypeStruct((T, D_MODEL), F32),
                   jax.ShapeDtypeStruct((T * TOK_PITCH, LANES), F32),
                   jax.ShapeDtypeStruct((SUBLANES, T), F32),
                   jax.ShapeDtypeStruct((ROUTE_ROWS, LANES), F32)],
        scratch_shapes=[pltpu.VMEM((ROUTE_ROWS, LANES), F32)],
        compiler_params=pltpu.CompilerParams(dimension_semantics=("arbitrary",),
                                             vmem_limit_bytes=VMEM_LIMIT),
        name="out_proj_router",
    )(x2, yssd, yfoxt, wo1, wo2, g_ffn, wr, br)


def _dispatch_kernel(ztile_ref, pos_ref, src_ref, dst_hbm, zero_ref, sem, zsem):
    TD = pos_ref.shape[-1]
    tile_rows = TM_MOE * TOK_PITCH

    @pl.when(pl.program_id(0) == 0)
    def _():
        zero_ref[...] = jnp.zeros(zero_ref.shape, F32)

        def zcopy(e):
            dst = pl.multiple_of(ztile_ref[e] * tile_rows, tile_rows)
            return pltpu.make_async_copy(zero_ref, dst_hbm.at[pl.ds(dst, tile_rows)], zsem)

        def zstart(e, c):
            @pl.when(ztile_ref[e] >= 0)
            def _():
                zcopy(e).start()
            return c

        def zwait(e, c):
            @pl.when(ztile_ref[e] >= 0)
            def _():
                zcopy(e).wait()
            return c

        lax.fori_loop(0, ztile_ref.shape[0], zstart, 0)
        lax.fori_loop(0, ztile_ref.shape[0], zwait, 0)

    def copy(r):
        dst = pl.multiple_of(pos_ref[0, r] * TOK_PITCH, 4)
        src = pl.multiple_of(r * TOK_PITCH, 4)
        return pltpu.make_async_copy(src_ref.at[pl.ds(src, TOK_PITCH)], dst_hbm.at[pl.ds(dst, TOK_PITCH)], sem)

    def start(rb, c):
        for u in range(DMA_ISSUE_UNROLL):
            copy(DMA_ISSUE_UNROLL * rb + u).start(priority=u % 2)
        return c

    lax.fori_loop(0, TD // DMA_ISSUE_UNROLL, start, 0)

    def wait(r, c):
        copy(r).wait()
        return c

    lax.fori_loop(0, TD, wait, 0, unroll=DMA_WAIT_UNROLL)


def _dispatch(ztiles, pos, xrow, n_rows):
    T = pos.shape[0]
    td = TD_DISP
    pos3 = pos.reshape(T // td, 1, td)
    grid_spec = pltpu.PrefetchScalarGridSpec(
        num_scalar_prefetch=1,
        grid=(T // td,),
        in_specs=[pl.BlockSpec((None, 1, td), lambda i, zt: (i, 0, 0), memory_space=pltpu.SMEM),
                  pl.BlockSpec((td * TOK_PITCH, LANES), lambda i, zt: (i, 0))],
        out_specs=pl.BlockSpec(memory_space=pl.ANY),
        scratch_shapes=[pltpu.VMEM((TM_MOE * TOK_PITCH, LANES), F32),
                        pltpu.SemaphoreType.DMA(()), pltpu.SemaphoreType.DMA(())],
    )
    return pl.pallas_call(
        _dispatch_kernel,
        grid_spec=grid_spec,
        out_shape=jax.ShapeDtypeStruct((n_rows * TOK_PITCH, LANES), F32),
        compiler_params=pltpu.CompilerParams(dimension_semantics=("arbitrary",),
                                             vmem_limit_bytes=VMEM_LIMIT),
        name="dispatch",
    )(ztiles, pos3, xrow)


def _experts_kernel(ea_ref, eb_ref, blk_ref, valid_ref,
                    xs_ref, wga_ref, wua_ref, wda_ref, wgb_ref, wub_ref, wdb_ref, ys_ref):
    del ea_ref, eb_ref, blk_ref
    i = pl.program_id(0)
    TM = xs_ref.shape[0] // TOK_PITCH

    @pl.when(valid_ref[i] > 0)
    def _():
        x = _from_token_tiles(xs_ref, TM, TOK_PITCH).astype(BF16)
        w = xs_ref[pl.ds(TOK_ROWS, TM, stride=TOK_PITCH), :]

        def expert(wg_ref, wu_ref, wd_ref, wt):
            g = _dot(x, wg_ref[...])
            u = _dot(x, wu_ref[...])
            return _dot((g * _sigmoid(g) * u * wt).astype(BF16), wd_ref[...])

        y = expert(wga_ref, wua_ref, wda_ref, w[:, 0:1]) + expert(wgb_ref, wub_ref, wdb_ref, w[:, 1:2])
        _to_token_tiles(ys_ref, y, TOK_ROWS)

    @pl.when(valid_ref[i] == 0)
    def _():
        ys_ref[...] = jnp.zeros(ys_ref.shape, F32)


def _experts(ea, eb, blk, valid, xs, wg, wu, wd):
    tm = TM_MOE
    n_rows = xs.shape[0] // TOK_PITCH
    nt = n_rows // tm
    xmap = lambda i, ea, eb, blk, valid: (blk[i], 0)
    amap = lambda i, ea, eb, blk, valid: (ea[i], 0, 0)
    bmap = lambda i, ea, eb, blk, valid: (eb[i], 0, 0)
    up_spec = lambda m: pl.BlockSpec((None, D_MODEL, D_EXPERT), m)
    dn_spec = lambda m: pl.BlockSpec((None, D_EXPERT, D_MODEL), m)
    grid_spec = pltpu.PrefetchScalarGridSpec(
        num_scalar_prefetch=4,
        grid=(nt,),
        in_specs=[pl.BlockSpec((tm * TOK_PITCH, LANES), xmap),
                  up_spec(amap), up_spec(amap), dn_spec(amap),
                  up_spec(bmap), up_spec(bmap), dn_spec(bmap)],
        out_specs=pl.BlockSpec((tm * TOK_ROWS, LANES), lambda i, ea, eb, blk, valid: (i, 0)),
    )
    return pl.pallas_call(
        _experts_kernel,
        grid_spec=grid_spec,
        out_shape=jax.ShapeDtypeStruct((n_rows * TOK_ROWS, LANES), F32),
        compiler_params=pltpu.CompilerParams(dimension_semantics=("arbitrary",),
                                             vmem_limit_bytes=VMEM_LIMIT),
        name="experts",
    )(ea, eb, blk, valid, xs, wg, wu, wd, wg, wu, wd)


def _combine_kernel(pos_cur_ref, pos_nxt_ref, h1_ref, p_ref, wpp_ref, wpg_ref, gple_ref, gfin_ref,
                    ys_hbm, o_ref, buf_ref, sem):
    TM = h1_ref.shape[0]
    i = pl.program_id(0)
    n = pl.num_programs(0)
    slot = i % 2

    def copy(pos_ref, s, r):
        src = pl.multiple_of(pos_ref[0, r] * TOK_ROWS, TOK_ROWS)
        dst = pl.multiple_of(r * TOK_PITCH, 4)
        return pltpu.make_async_copy(ys_hbm.at[pl.ds(src, TOK_ROWS)], buf_ref.at[s, pl.ds(dst, TOK_ROWS)],
                                     sem.at[s])

    def gather(pos_ref, s):
        def start(rb, c):
            for u in range(DMA_ISSUE_UNROLL):
                copy(pos_ref, s, DMA_ISSUE_UNROLL * rb + u).start(priority=u % 2)
            return c
        lax.fori_loop(0, TM // DMA_ISSUE_UNROLL, start, 0)

    @pl.when(i == 0)
    def _():
        gather(pos_cur_ref, 0)

    @pl.when(i + 1 < n)
    def _():
        gather(pos_nxt_ref, 1 - slot)

    def wait(r, c):
        copy(pos_cur_ref, slot, r).wait()
        return c

    lax.fori_loop(0, TM, wait, 0, unroll=DMA_WAIT_UNROLL)

    h2 = h1_ref[...] + _from_token_tiles(buf_ref.at[slot], TM, TOK_PITCH)
    n3 = _rms(h2, gple_ref[...]).astype(BF16)
    gate = _sigmoid(_dot(n3, wpg_ref[...]))
    pp = _dot(p_ref[...].astype(BF16), wpp_ref[...])
    o_ref[...] = _rms(h2 + pp * gate, gfin_ref[...])


def _combine(pos, h1, p2, wpp, wpg, g_ple, g_final, ys):
    T = h1.shape[0]
    tm = TM_COMB
    nt = T // tm
    pos3 = pos.reshape(nt, 1, tm)
    row = lambda i: (i, 0)
    const = lambda i: (0, 0)
    return pl.pallas_call(
        _combine_kernel,
        grid=(nt,),
        in_specs=[pl.BlockSpec((None, 1, tm), lambda i: (i, 0, 0), memory_space=pltpu.SMEM),
                  pl.BlockSpec((None, 1, tm), lambda i: (jnp.minimum(i + 1, nt - 1), 0, 0),
                               memory_space=pltpu.SMEM),
                  pl.BlockSpec((tm, D_MODEL), row),
                  pl.BlockSpec((tm, D_PLE), row),
                  pl.BlockSpec(wpp.shape, const),
                  pl.BlockSpec(wpg.shape, const),
                  pl.BlockSpec((1, D_MODEL), const),
                  pl.BlockSpec((1, D_MODEL), const),
                  pl.BlockSpec(memory_space=pl.ANY)],
        out_specs=pl.BlockSpec((tm, D_MODEL), row),
        out_shape=jax.ShapeDtypeStruct((T, D_MODEL), F32),
        scratch_shapes=[pltpu.VMEM((2, tm * TOK_PITCH, LANES), F32),
                        pltpu.SemaphoreType.DMA((2,))],
        compiler_params=pltpu.CompilerParams(dimension_semantics=("arbitrary",),
                                             vmem_limit_bytes=VMEM_LIMIT),
        name="combine_ple",
    )(pos3, pos3, h1, p2, wpp, wpg, g_ple, g_final, ys)


def _tile_tables(counts, n_tiles):
    nt = (counts + TM_MOE - 1) // TM_MOE
    cum = jnp.cumsum(nt)
    off = (cum - nt) * TM_MOE
    total = cum[-1]
    tile = jnp.arange(n_tiles, dtype=I32)
    valid = (tile < total).astype(I32)
    blk = jnp.maximum(jnp.minimum(tile, total - 1), 0)
    tb = jnp.minimum(jnp.sum((cum[None, :] <= blk[:, None]).astype(I32), axis=1), N_BUCKETS - 1)
    pair_a = jnp.array([0, 0, 0, 1, 1, 2], I32)
    pair_b = jnp.array([1, 2, 3, 2, 3, 3], I32)
    grp = tb // N_PAIRS
    ea = grp * EXPERTS_PER_GROUP + pair_a[tb % N_PAIRS]
    eb = grp * EXPERTS_PER_GROUP + pair_b[tb % N_PAIRS]
    last = jnp.where(nt > 0, cum - 1, -1)
    tail = total + jnp.arange(N_BUCKETS, dtype=I32)
    ztiles = jnp.concatenate([last, jnp.where(tail < n_tiles, tail, -1)]).astype(I32)
    return off, ea, eb, blk, valid, ztiles


def kernel(x, p, w_in, conv_w, conv_b, dt_bias, a_log, d_skip, g_ssd, fox_fbias, w_out, g_mix, g_ffn,
           w_route_group, b_route_group, w_route_expert, b_route_expert, w_exp_gate, w_exp_up, w_exp_down,
           g_ple, w_ple_proj, w_ple_gate, g_final):
    B, S, _ = x.shape
    T = B * S
    assert S % L_SSD == 0 and S % TQ_FOX == 0 and S % TM_PROJ == 0 and T % TD_DISP == 0
    x2 = x.reshape(T, D_MODEL)
    p2 = p[0].reshape(T, D_PLE)

    wi = w_in[0]
    o_xbc = D_SSD
    o_dt = o_xbc + D_XBC
    o_q = o_dt + SSD_HEADS
    o_k = o_q + D_FOX
    o_v = o_k + D_FOX
    o_f = o_v + D_FOX
    wz = wi[:, 0:o_xbc].astype(BF16)
    wxbc = wi[:, o_xbc:o_dt].astype(BF16)
    wq = wi[:, o_q:o_k].astype(BF16)
    wk = wi[:, o_k:o_v].astype(BF16)
    wvt = wi[:, o_v:o_f].T.astype(BF16)
    w_dt = wi[:, o_dt:o_q]
    w_f = wi[:, o_f:o_f + FOX_HEADS]
    wsm = jnp.pad(w_dt, ((0, 0), (0, LANES - SSD_HEADS))).astype(BF16)
    wsmt = jnp.concatenate([w_dt, w_f], axis=1).T.astype(BF16)
    lane_pad = lambda v: jnp.pad(v.reshape(1, -1), ((0, 0), (0, LANES - v.shape[-1])))
    colb = lane_pad(dt_bias[0])
    fbias = jnp.broadcast_to(fox_fbias[0][:, None], (FOX_HEADS, LANES))
    cola = lane_pad(a_log[0])
    rowb = jnp.broadcast_to(dt_bias[0][:, None], (SSD_HEADS, LANES))
    rowa = jnp.broadcast_to(a_log[0][:, None], (SSD_HEADS, LANES))
    dexp = jnp.repeat(d_skip[0], SSD_HEAD_DIM).reshape(1, D_SSD)
    expand = (jnp.arange(LANES)[:, None] == (jnp.arange(D_SSD) // SSD_HEAD_DIM)[None, :]).astype(BF16)
    col = jnp.arange(D_QK)
    head = col // LANES
    j_bias = col % LANES - jnp.where(head % 2 == 0, FOX_HEAD_DIM, 0)
    bias_col = (j_bias >= 0) & (j_bias < F_PARTS)
    qones = bias_col.astype(F32).reshape(1, D_QK)
    src_row = j_bias * FOX_HEADS + head
    place = (bias_col[None, :] & (jnp.arange((F_PARTS + 1) * FOX_HEADS)[:, None] == src_row[None, :])).astype(BF16)
    wo = w_out[0].astype(BF16)
    wr = jnp.pad(jnp.concatenate([w_route_group[0], w_route_expert[0]], axis=1).T,
                 ((0, ROUTE_ROWS - N_GROUPS - N_EXPERTS), (0, 0)))
    wr_hi = wr.astype(BF16)
    wr = jnp.concatenate([wr_hi, (wr - wr_hi.astype(F32)).astype(BF16)], axis=0)
    br = jnp.broadcast_to(jnp.pad(jnp.concatenate([b_route_group[0], b_route_expert[0]]),
                                  (0, ROUTE_ROWS - N_GROUPS - N_EXPERTS))[:, None], (ROUTE_ROWS, LANES))
    wg = w_exp_gate[0].reshape(N_EXPERTS, D_MODEL, D_EXPERT).astype(BF16)
    wu = w_exp_up[0].reshape(N_EXPERTS, D_MODEL, D_EXPERT).astype(BF16)
    wd = w_exp_down[0].reshape(N_EXPERTS, D_EXPERT, D_MODEL).astype(BF16)

    z, xbc, q, k, vt, sm, smt = _in_proj(x2, g_mix[0].reshape(1, -1), wz, wxbc, wq, wk, wvt, wsm, wsmt,
                                         fbias, qones, place, S)
    yssd = _ssd(xbc, z, sm, smt, conv_w[0].reshape(SSD_CONV, D_XBC), conv_b[0].reshape(1, -1),
                colb, cola, rowb, rowa, dexp, g_ssd[0].reshape(1, -1), expand, B, S)
    yfoxt = _fox(q, k, vt, B, S)
    h1, xrow, meta, cnt = _out_proj(x2, yssd, yfoxt, wo[0:D_SSD], wo[D_SSD:], g_ffn[0].reshape(1, -1), wr, br)

    n_rows = T + N_BUCKETS * TM_MOE
    bucket = meta[0].astype(I32)
    rank = meta[1].astype(I32)
    off, ea, eb, blk, valid, ztiles = _tile_tables(cnt[0:N_BUCKETS, 0].astype(I32), n_rows // TM_MOE)
    pos = rank
    for b in range(N_BUCKETS):
        pos = pos + jnp.where(bucket == b, off[b], 0)

    xs = _dispatch(ztiles, pos, xrow, n_rows)
    ys = _experts(ea, eb, blk, valid, xs, wg, wu, wd)
    out = _combine(pos, h1, p2, w_ple_proj[0].astype(BF16), w_ple_gate[0].astype(BF16),
                   g_ple[0].reshape(1, -1), g_final.reshape(1, -1), ys)
    return out.reshape(B, S, D_MODEL)
```

```python
import functools

import jax
import jax.numpy as jnp
from jax import lax
from jax.experimental import pallas as pl
from jax.experimental.pallas import tpu as pltpu

F32 = jnp.float32
BF16 = jnp.bfloat16
I32 = jnp.int32

D_MODEL = 1024
SSD_HEADS = 8
SSD_HEAD_DIM = 64
SSD_GROUPS = 2
SSD_STATE = 128
SSD_CONV = 4
D_SSD = SSD_HEADS * SSD_HEAD_DIM
D_XBC = D_SSD + 2 * SSD_GROUPS * SSD_STATE
FOX_HEADS = 8
FOX_HEAD_DIM = 64
D_FOX = FOX_HEADS * FOX_HEAD_DIM
N_GROUPS = 4
EXPERTS_PER_GROUP = 4
N_EXPERTS = N_GROUPS * EXPERTS_PER_GROUP
N_PAIRS = 6
N_BUCKETS = N_GROUPS * N_PAIRS
ROUTE_ROWS = 32
D_EXPERT = 512
D_PLE = 256
EPS = 1e-6
LOG2E = 1.4426950408889634
LANES = 128
SUBLANES = 8
BF16_SUBLANES = 16
TOK_ROWS = D_MODEL // LANES
TOK_PITCH = 12
PITCH_ALIGN = 4
D_QK = FOX_HEADS * LANES
F_PARTS = 3
VT_ROWS = FOX_HEAD_DIM + BF16_SUBLANES
VMEM_LIMIT = 52 * 1024 * 1024

TM_PROJ = 512
L_SSD = 256
SSD_CHUNKS_PER_STEP = 2
TQ_FOX = 512
TK_FOX = 256
TM_MOE = 256
TD_DISP = 1024
TM_COMB = 512
DMA_WAIT_UNROLL = 16
DMA_ISSUE_UNROLL = 8

_NT = (((1,), (1,)), ((), ()))
_TN = (((0,), (0,)), ((), ()))


def _rms(x, g):
    ms = jnp.mean(x * x, axis=-1, keepdims=True)
    return x * lax.rsqrt(ms + EPS) * g


def _sigmoid(x):
    return 1.0 / (1.0 + jnp.exp(-x))


def _softplus(x):
    return jnp.maximum(x, 0.0) + jnp.log(1.0 + jnp.exp(-jnp.abs(x)))


def _split_bf16(x, parts):
    out = []
    r = x
    for _ in range(parts):
        h = r.astype(BF16)
        out.append(h)
        r = r - h.astype(F32)
    return out


def _dot(a, b):
    return jnp.dot(a, b, preferred_element_type=F32)


def _dot_exact(a01, x, parts):
    acc = None
    for piece in _split_bf16(x, parts):
        t = _dot(a01, piece)
        acc = t if acc is None else acc + t
    return acc


def _to_token_tiles(ref, x, stride, offset=0):
    m = x.shape[0]
    for s in range(TOK_ROWS):
        ref[pl.ds(offset + s, m, stride=stride), :] = x[:, s * LANES:(s + 1) * LANES]


def _from_token_tiles(ref, m, stride, offset=0):
    return jnp.concatenate([ref[pl.ds(offset + s, m, stride=stride), :] for s in range(TOK_ROWS)], axis=1)


def _inproj_kernel(x_ref, g_ref, wz_ref, wxbc_ref, wq_ref, wk_ref, wvt_ref, wsm_ref, wsmt_ref,
                   fbias_ref, qones_ref, place_ref,
                   z_ref, xbc_ref, q_ref, k_ref, vt_ref, sm_ref, smt_ref, fcar_ref, *, tiles_per_seq):
    TM = x_ref.shape[0]

    @pl.when(pl.program_id(0) % tiles_per_seq == 0)
    def _():
        fcar_ref[...] = jnp.zeros(fcar_ref.shape, F32)

    hn = _rms(x_ref[...], g_ref[...]).astype(BF16)
    z_ref[...] = _dot(hn, wz_ref[...]).astype(BF16)
    xbc_ref[...] = _dot(hn, wxbc_ref[...]).astype(BF16)
    vt = lax.dot_general(wvt_ref[...], hn, _NT, preferred_element_type=F32).astype(BF16)
    ones_rows = jnp.ones((VT_ROWS - FOX_HEAD_DIM, TM), BF16)
    for h in range(FOX_HEADS):
        vt_ref[h * VT_ROWS:h * VT_ROWS + FOX_HEAD_DIM, :] = vt[h * FOX_HEAD_DIM:(h + 1) * FOX_HEAD_DIM, :]
        vt_ref[h * VT_ROWS + FOX_HEAD_DIM:(h + 1) * VT_ROWS, :] = ones_rows
    sm_ref[...] = _dot(hn, wsm_ref[...])
    smt = lax.dot_general(wsmt_ref[...], hn, _NT, preferred_element_type=F32)
    smt_ref[...] = smt[0:SSD_HEADS]

    logf = -_softplus(-(smt[SSD_HEADS:] + fbias_ref[:, 0:1]))
    ri = lax.broadcasted_iota(I32, (TM, TM), 0)
    ci = lax.broadcasted_iota(I32, (TM, TM), 1)
    triu = (ri <= ci).astype(BF16)
    fcum = fcar_ref[:, 0:1]
    for piece in _split_bf16(logf, 3):
        fcum = fcum + _dot(piece, triu)
    fcar_ref[...] = jnp.broadcast_to(fcum[:, TM - 1:TM], fcar_ref.shape)
    pieces = _split_bf16(fcum * (-LOG2E), F_PARTS)
    stacked = jnp.concatenate([p.astype(F32) for p in pieces] + [jnp.zeros((SUBLANES, TM), F32)],
                              axis=0).astype(BF16)
    kbias = lax.dot_general(stacked, place_ref[...], _TN, preferred_element_type=F32)
    lane = lax.broadcasted_iota(I32, (TM, LANES), 1)
    qc = _dot(hn, wq_ref[...]) * (FOX_HEAD_DIM ** -0.5 * LOG2E)
    kc = _dot(hn, wk_ref[...])
    for h in range(FOX_HEADS):
        pair = slice((h // 2) * LANES, (h // 2 + 1) * LANES)
        tile = slice(h * LANES, (h + 1) * LANES)
        own = (lane < FOX_HEAD_DIM) if h % 2 == 0 else (lane >= FOX_HEAD_DIM)
        q_ref[:, tile] = jnp.where(own, qc[:, pair], qones_ref[:, tile]).astype(BF16)
        k_ref[:, tile] = jnp.where(own, kc[:, pair], kbias[:, tile]).astype(BF16)


def _in_proj(x2, g_mix, wz, wxbc, wq, wk, wvt, wsm, wsmt, fbias, qones, place, S):
    T = x2.shape[0]
    tm = TM_PROJ
    row = lambda i: (i, 0)
    const = lambda i: (0, 0)
    full = lambda a: pl.BlockSpec(a.shape, const)
    return pl.pallas_call(
        functools.partial(_inproj_kernel, tiles_per_seq=S // tm),
        grid=(T // tm,),
        in_specs=[pl.BlockSpec((tm, D_MODEL), row), full(g_mix), full(wz), full(wxbc), full(wq), full(wk),
                  full(wvt), full(wsm), full(wsmt), full(fbias), full(qones), full(place)],
        out_specs=[pl.BlockSpec((tm, D_SSD), row),
                   pl.BlockSpec((tm, D_XBC), row),
                   pl.BlockSpec((tm, D_QK), row),
                   pl.BlockSpec((tm, D_QK), row),
                   pl.BlockSpec((FOX_HEADS * VT_ROWS, tm), lambda i: (0, i)),
                   pl.BlockSpec((tm, LANES), row),
                   pl.BlockSpec((SSD_HEADS, tm), lambda i: (0, i))],
        out_shape=[jax.ShapeDtypeStruct((T, D_SSD), BF16),
                   jax.ShapeDtypeStruct((T, D_XBC), BF16),
                   jax.ShapeDtypeStruct((T, D_QK), BF16),
                   jax.ShapeDtypeStruct((T, D_QK), BF16),
                   jax.ShapeDtypeStruct((FOX_HEADS * VT_ROWS, T), BF16),
                   jax.ShapeDtypeStruct((T, LANES), F32),
                   jax.ShapeDtypeStruct((SSD_HEADS, T), F32)],
        scratch_shapes=[pltpu.VMEM((SUBLANES, LANES), F32)],
        compiler_params=pltpu.CompilerParams(dimension_semantics=("arbitrary",),
                                             vmem_limit_bytes=VMEM_LIMIT),
        name="in_proj",
    )(x2, g_mix, wz, wxbc, wq, wk, wvt, wsm, wsmt, fbias, qones, place)


def _ssd_kernel(xbc_ref, z_ref, sm_ref, smt_ref, *rest):
    *param_refs, y_ref, tail_ref, st_ref = rest

    @pl.when(pl.program_id(1) == 0)
    def _():
        tail_ref[...] = jnp.zeros(tail_ref.shape, BF16)
        st_ref[...] = jnp.zeros(st_ref.shape, F32)

    for c in range(SSD_CHUNKS_PER_STEP):
        rows = pl.ds(c * L_SSD, L_SSD)
        _ssd_chunk(xbc_ref.at[rows], z_ref.at[rows], sm_ref.at[rows], smt_ref.at[:, rows], *param_refs,
                   y_ref.at[rows], tail_ref, st_ref)


def _ssd_chunk(xbc_ref, z_ref, sm_ref, smt_ref, convw_ref, convb_ref, colb_ref, cola_ref,
               rowb_ref, rowa_ref, dexp_ref, gssd_ref, expand_ref,
               y_ref, tail_ref, st_ref):
    L = xbc_ref.shape[0]
    R = SSD_HEADS // SSD_GROUPS
    TAIL = tail_ref.shape[0]

    ri = lax.broadcasted_iota(I32, (L, L), 0)
    ci = lax.broadcasted_iota(I32, (L, L), 1)
    causal = ri >= ci
    tril = causal.astype(BF16)
    triu = (ri <= ci).astype(BF16)

    zf = z_ref[...].astype(F32)
    zgate = zf * _sigmoid(zf)

    x_bf = xbc_ref[...]
    tail = tail_ref[...]
    hr = lax.broadcasted_iota(I32, (SUBLANES, TAIL), 0)
    hc = lax.broadcasted_iota(I32, (SUBLANES, TAIL), 1)
    conv = convb_ref[...] + convw_ref[SSD_CONV - 1:SSD_CONV, :] * x_bf.astype(F32)
    head = None
    for d in range(1, SSD_CONV):
        w_d = convw_ref[SSD_CONV - 1 - d:SSD_CONV - d, :]
        conv = conv + w_d * _dot((ri - d == ci).astype(BF16), x_bf)
        t = w_d * _dot((hc == hr + (TAIL - d)).astype(BF16), tail)
        head = t if head is None else head + t
    conv = jnp.concatenate([conv[0:SUBLANES] + head, conv[SUBLANES:]], axis=0)
    tail_ref[...] = xbc_ref[L - TAIL:L, :]
    xc = conv * _sigmoid(conv)
    xs = xc[:, 0:D_SSD]
    xs_bf = xs.astype(BF16)
    bm = xc[:, D_SSD:D_SSD + SSD_GROUPS * SSD_STATE].astype(BF16)
    cm = xc[:, D_SSD + SSD_GROUPS * SSD_STATE:].astype(BF16)

    dtr = _softplus(smt_ref[...] + rowb_ref[:, 0:1])
    adt_r = dtr * (-LOG2E * jnp.exp(rowa_ref[:, 0:1]))
    csr = None
    for part in _split_bf16(adt_r, 3):
        t = _dot(part, triu)
        csr = t if csr is None else csr + t

    dtc = _softplus(sm_ref[...] + colb_ref[...])
    adt_c = dtc * (-LOG2E * jnp.exp(cola_ref[...]))
    cs_c = _dot_exact(tril, adt_c, 3)
    cs_last = cs_c[L - 1:L, :]
    e1 = jnp.exp2(cs_c)
    wst = dtc * jnp.exp2(cs_last - cs_c)
    ex = expand_ref[...]
    p1 = _split_bf16(e1, 2)
    e1x = _dot(p1[0], ex) + _dot(p1[1], ex)
    p2 = _split_bf16(wst, 2)
    wstx = _dot(p2[0], ex) + _dot(p2[1], ex)

    xw = (xs * wstx).astype(BF16)
    dec8 = jnp.broadcast_to(jnp.exp2(csr[:, L - 1:L]), (SSD_HEADS, SSD_STATE))

    ydiag = []
    yoff = []
    for g in range(SSD_GROUPS):
        bg = bm[:, g * SSD_STATE:(g + 1) * SSD_STATE]
        cg = cm[:, g * SSD_STATE:(g + 1) * SSD_STATE]
        gmat = lax.dot_general(cg, bg, _NT, preferred_element_type=F32)
        s_old = st_ref[g]
        yoff.append(lax.dot_general(cg, s_old.astype(BF16), _NT, preferred_element_type=F32))
        for j in range(R):
            h = g * R + j
            seg = cs_c[:, h:h + 1] - csr[h:h + 1, :]
            lm = jnp.exp2(jnp.where(causal, seg, -jnp.inf))
            m = (gmat * lm * dtr[h:h + 1, :]).astype(BF16)
            ydiag.append(_dot(m, xs_bf[:, h * SSD_HEAD_DIM:(h + 1) * SSD_HEAD_DIM]))
        upd = lax.dot_general(xw[:, g * R * SSD_HEAD_DIM:(g + 1) * R * SSD_HEAD_DIM], bg, _TN,
                              preferred_element_type=F32)
        dec = jnp.concatenate(
            [jnp.broadcast_to(dec8[g * R + j:g * R + j + 1, :], (SSD_HEAD_DIM, SSD_STATE)) for j in range(R)],
            axis=0)
        st_ref[g] = dec * s_old + upd

    y = jnp.concatenate(ydiag, axis=1) + e1x * jnp.concatenate(yoff, axis=1) + dexp_ref[...] * xs
    y_ref[...] = _rms(y * zgate, gssd_ref[...]).astype(BF16)


def _ssd(xbc, z, sm, smt, convw, convb, colb, cola, rowb, rowa, dexp, gssd, expand, B, S):
    L = L_SSD * SSD_CHUNKS_PER_STEP
    nc = S // L
    T = B * S
    row = lambda b, c: (b * nc + c, 0)
    const = lambda b, c: (0, 0)
    full = lambda a: pl.BlockSpec(a.shape, const)
    return pl.pallas_call(
        _ssd_kernel,
        grid=(B, nc),
        in_specs=[pl.BlockSpec((L, D_XBC), row),
                  pl.BlockSpec((L, D_SSD), row),
                  pl.BlockSpec((L, LANES), row),
                  pl.BlockSpec((SSD_HEADS, L), lambda b, c: (0, b * nc + c)),
                  full(convw), full(convb), full(colb), full(cola), full(rowb), full(rowa),
                  full(dexp), full(gssd), full(expand)],
        out_specs=pl.BlockSpec((L, D_SSD), row),
        out_shape=jax.ShapeDtypeStruct((T, D_SSD), BF16),
        scratch_shapes=[pltpu.VMEM((BF16_SUBLANES, D_XBC), BF16),
                        pltpu.VMEM((SSD_GROUPS, (SSD_HEADS // SSD_GROUPS) * SSD_HEAD_DIM, SSD_STATE), F32)],
        compiler_params=pltpu.CompilerParams(dimension_semantics=("arbitrary", "arbitrary"),
                                             vmem_limit_bytes=VMEM_LIMIT),
        name="ssd",
    )(xbc, z, sm, smt, convw, convb, colb, cola, rowb, rowa, dexp, gssd, expand)


def _fox_kernel(q_ref, k_ref, vt_ref, o_ref, sa_ref, sb_ref):
    TQ = q_ref.shape[0]
    TK = TK_FOX
    assert TQ == 2 * TK
    n_pairs = pl.program_id(1)
    n_full = 2 * n_pairs
    key_idx = lax.broadcasted_iota(I32, (TK, TQ), 0)
    qry_idx = lax.broadcasted_iota(I32, (TK, TQ), 1)
    heads = range(FOX_HEADS)
    qs = [q_ref[:, h * LANES:(h + 1) * LANES] for h in heads]

    def scores(j, s_ref, h):
        ks = pl.multiple_of(j * TK, TK)
        s = lax.dot_general(k_ref[pl.ds(ks, TK), h * LANES:(h + 1) * LANES], qs[h], _NT,
                            preferred_element_type=F32)
        s_ref[h] = s
        return jnp.max(s, axis=0, keepdims=True)

    def softmax_pv(j, s_ref, h, state, diag):
        ks = pl.multiple_of(j * TK, TK)
        m, acc, smax = state
        s = s_ref[h]
        if diag is not None:
            s = jnp.where(key_idx + diag * TK <= qry_idx, s, -jnp.inf)
            smax = jnp.max(s, axis=0, keepdims=True)
        mn = jnp.maximum(m, smax)
        p = jnp.exp2(s - mn).astype(BF16)
        acc = jnp.exp2(m - mn) * acc + _dot(vt_ref[h * VT_ROWS:(h + 1) * VT_ROWS, pl.ds(ks, TK)], p)
        return mn, acc

    def step(j, cur_ref, nxt_ref, carry, diag):
        new = []
        for h in heads:
            smax_nxt = scores(j + 1, nxt_ref, h) if nxt_ref is not None else carry[h][2]
            new.append(softmax_pv(j, cur_ref, h, carry[h], diag) + (smax_nxt,))
        return tuple(new)

    def pair(t, carry):
        carry = step(2 * t, sa_ref, sb_ref, carry, None)
        return step(2 * t + 1, sb_ref, sa_ref, carry, None)

    carry = tuple((jnp.full((1, TQ), -1e30, F32), jnp.zeros((VT_ROWS, TQ), F32), scores(0, sa_ref, h))
                  for h in heads)
    carry = lax.fori_loop(0, n_pairs, pair, carry)
    carry = step(n_full, sa_ref, sb_ref, carry, 0)
    carry = step(n_full + 1, sb_ref, None, carry, 1)
    o_ref[...] = jnp.concatenate(
        [acc[0:FOX_HEAD_DIM] / acc[FOX_HEAD_DIM:FOX_HEAD_DIM + 1] for _, acc, _ in carry], axis=0).astype(BF16)


def _fox(q, k, vt, B, S):
    TQ = TQ_FOX
    nq = S // TQ
    T = B * S
    return pl.pallas_call(
        _fox_kernel,
        grid=(B, nq),
        in_specs=[pl.BlockSpec((TQ, D_QK), lambda b, i: (b * nq + i, 0)),
                  pl.BlockSpec((S, D_QK), lambda b, i: (b, 0)),
                  pl.BlockSpec((FOX_HEADS * VT_ROWS, S), lambda b, i: (0, b))],
        out_specs=pl.BlockSpec((D_FOX, TQ), lambda b, i: (0, b * nq + i)),
        out_shape=jax.ShapeDtypeStruct((D_FOX, T), BF16),
        scratch_shapes=[pltpu.VMEM((FOX_HEADS, TK_FOX, TQ), F32), pltpu.VMEM((FOX_HEADS, TK_FOX, TQ), F32)],
        compiler_params=pltpu.CompilerParams(dimension_semantics=("arbitrary", "arbitrary"),
                                             vmem_limit_bytes=VMEM_LIMIT),
        name="fox",
    )(q, k, vt)


def _outproj_kernel(x_ref, ys_ref, yft_ref, wo1_ref, wo2_ref, g_ref, wr_ref, br_ref,
                    h1_ref, xrow_ref, meta_ref, cnt_ref, carry_ref):
    TM = x_ref.shape[0]
    i = pl.program_id(0)

    @pl.when(i == 0)
    def _():
        carry_ref[...] = jnp.zeros(carry_ref.shape, F32)

    h1 = (x_ref[...] + _dot(ys_ref[...], wo1_ref[...])
          + lax.dot_general(yft_ref[...], wo2_ref[...], _TN, preferred_element_type=F32))
    h1_ref[...] = h1
    hn = _rms(h1, g_ref[...])
    h_hi, h_lo = _split_bf16(hn, 2)
    lg = lax.dot_general(wr_ref[...], h_hi, _NT, preferred_element_type=F32)
    logits = (lg[0:ROUTE_ROWS] + lg[ROUTE_ROWS:]
              + lax.dot_general(wr_ref[0:ROUTE_ROWS, :], h_lo, _NT, preferred_element_type=F32)
              + br_ref[:, 0:1])
    row = lax.broadcasted_iota(I32, (ROUTE_ROWS, TM), 0).astype(F32)
    ninf = -jnp.inf
    far = float(ROUTE_ROWS)
    gl = jnp.where(row < N_GROUPS, logits, ninf)
    gmax = jnp.max(gl, axis=0, keepdims=True)
    p_grp = 1.0 / jnp.sum(jnp.exp(gl - gmax), axis=0, keepdims=True)
    g_idx = jnp.min(jnp.where(gl == gmax, row, far), axis=0, keepdims=True)
    e0 = N_GROUPS + EXPERTS_PER_GROUP * g_idx
    el = jnp.where((row >= e0) & (row < e0 + EXPERTS_PER_GROUP), logits, ninf)
    v1 = jnp.max(el, axis=0, keepdims=True)
    i1 = jnp.min(jnp.where(el == v1, row, far), axis=0, keepdims=True)
    el2 = jnp.where(row == i1, ninf, el)
    v2 = jnp.max(el2, axis=0, keepdims=True)
    i2 = jnp.min(jnp.where(el2 == v2, row, far), axis=0, keepdims=True)
    e2 = jnp.exp(v2 - v1)
    w1 = p_grp / (1.0 + e2)
    w2 = p_grp * e2 / (1.0 + e2)
    l1 = i1 - e0
    l2 = i2 - e0
    first = l1 < l2
    a = jnp.where(first, l1, l2)
    b = jnp.where(first, l2, l1)
    wa = jnp.where(first, w1, w2)
    wb = jnp.where(first, w2, w1)
    pair = a * (5.0 - a) * 0.5 + b - 1.0
    bucket = g_idx * N_PAIRS + pair

    onehot = (row == bucket).astype(BF16)
    ri = lax.broadcasted_iota(I32, (TM, TM), 0)
    ci = lax.broadcasted_iota(I32, (TM, TM), 1)
    incl = _dot(onehot, (ri <= ci).astype(BF16))
    oh = onehot.astype(F32)
    rank = jnp.sum((incl - oh + carry_ref[:, 0:1]) * oh, axis=0, keepdims=True)
    carry_ref[...] = carry_ref[...] + jnp.broadcast_to(incl[:, TM - 1:TM], carry_ref.shape)
    cnt_ref[...] = carry_ref[...]

    r8 = lax.broadcasted_iota(I32, (SUBLANES, TM), 0)
    meta_ref[...] = jnp.where(r8 == 0, bucket, jnp.where(r8 == 1, rank, 0.0))
    wrows = jnp.where(r8 == 0, wa, jnp.where(r8 == 1, wb, 0.0))
    eye = (lax.broadcasted_iota(I32, (SUBLANES, LANES), 0)
           == lax.broadcasted_iota(I32, (SUBLANES, LANES), 1)).astype(BF16)
    wcols = None
    for piece in _split_bf16(wrows, 3):
        t = lax.dot_general(piece, eye, _TN, preferred_element_type=F32)
        wcols = t if wcols is None else wcols + t
    _to_token_tiles(xrow_ref, hn, TOK_PITCH)
    xrow_ref[pl.ds(TOK_ROWS, TM, stride=TOK_PITCH), :] = wcols
    for s in range(TOK_ROWS + 1, TOK_PITCH):
        xrow_ref[pl.ds(s, TM, stride=TOK_PITCH), :] = jnp.zeros((TM, LANES), F32)


def _out_proj(x2, yssd, yfoxt, wo1, wo2, g_ffn, wr, br):
    T = x2.shape[0]
    tm = TM_PROJ
    row = lambda i: (i, 0)
    const = lambda i: (0, 0)
    return pl.pallas_call(
        _outproj_kernel,
        grid=(T // tm,),
        in_specs=[pl.BlockSpec((tm, D_MODEL), row),
                  pl.BlockSpec((tm, D_SSD), row),
                  pl.BlockSpec((D_FOX, tm), lambda i: (0, i)),
                  pl.BlockSpec(wo1.shape, const),
                  pl.BlockSpec(wo2.shape, const),
                  pl.BlockSpec((1, D_MODEL), const),
                  pl.BlockSpec(wr.shape, const),
                  pl.BlockSpec(br.shape, const)],
        out_specs=[pl.BlockSpec((tm, D_MODEL), row),
                   pl.BlockSpec((tm * TOK_PITCH, LANES), row),
                   pl.BlockSpec((SUBLANES, tm), lambda i: (0, i)),
                   pl.BlockSpec((ROUTE_ROWS, LANES), const)],
        out_shape=[jax.ShapeDtypeStruct((T, D_MODEL), F32),
                   jax.ShapeDtypeStruct((T * TOK_PITCH, LANES), F32),
                   jax.ShapeDtypeStruct((SUBLANES, T), F32),
                   jax.ShapeDtypeStruct((ROUTE_ROWS, LANES), F32)],
        scratch_shapes=[pltpu.VMEM((ROUTE_ROWS, LANES), F32)],
        compiler_params=pltpu.CompilerParams(dimension_semantics=("arbitrary",),
                                             vmem_limit_bytes=VMEM_LIMIT),
        name="out_proj_router",
    )(x2, yssd, yfoxt, wo1, wo2, g_ffn, wr, br)


def _dispatch_kernel(ztile_ref, pos_ref, src_ref, dst_hbm, zero_ref, sem, zsem):
    TD = pos_ref.shape[-1]
    tile_rows = TM_MOE * TOK_PITCH

    @pl.when(pl.program_id(0) == 0)
    def _():
        zero_ref[...] = jnp.zeros(zero_ref.shape, F32)

        def zcopy(e):
            dst = pl.multiple_of(ztile_ref[e] * tile_rows, tile_rows)
            return pltpu.make_async_copy(zero_ref, dst_hbm.at[pl.ds(dst, tile_rows)], zsem)

        def zstart(e, c):
            @pl.when(ztile_ref[e] >= 0)
            def _():
                zcopy(e).start()
            return c

        def zwait(e, c):
            @pl.when(ztile_ref[e] >= 0)
            def _():
                zcopy(e).wait()
            return c

        lax.fori_loop(0, ztile_ref.shape[0], zstart, 0)
        lax.fori_loop(0, ztile_ref.shape[0], zwait, 0)

    def copy(r):
        dst = pl.multiple_of(pos_ref[0, r] * TOK_PITCH, PITCH_ALIGN)
        src = pl.multiple_of(r * TOK_PITCH, PITCH_ALIGN)
        return pltpu.make_async_copy(src_ref.at[pl.ds(src, TOK_PITCH)], dst_hbm.at[pl.ds(dst, TOK_PITCH)], sem)

    def start(rb, c):
        for u in range(DMA_ISSUE_UNROLL):
            copy(DMA_ISSUE_UNROLL * rb + u).start(priority=u % 2)
        return c

    lax.fori_loop(0, TD // DMA_ISSUE_UNROLL, start, 0)

    def wait(r, c):
        copy(r).wait()
        return c

    lax.fori_loop(0, TD, wait, 0, unroll=DMA_WAIT_UNROLL)


def _dispatch(ztiles, pos, xrow, n_rows):
    T = pos.shape[0]
    td = TD_DISP
    pos3 = pos.reshape(T // td, 1, td)
    grid_spec = pltpu.PrefetchScalarGridSpec(
        num_scalar_prefetch=1,
        grid=(T // td,),
        in_specs=[pl.BlockSpec((None, 1, td), lambda i, zt: (i, 0, 0), memory_space=pltpu.SMEM),
                  pl.BlockSpec((td * TOK_PITCH, LANES), lambda i, zt: (i, 0))],
        out_specs=pl.BlockSpec(memory_space=pl.ANY),
        scratch_shapes=[pltpu.VMEM((TM_MOE * TOK_PITCH, LANES), F32),
                        pltpu.SemaphoreType.DMA(()), pltpu.SemaphoreType.DMA(())],
    )
    return pl.pallas_call(
        _dispatch_kernel,
        grid_spec=grid_spec,
        out_shape=jax.ShapeDtypeStruct((n_rows * TOK_PITCH, LANES), F32),
        compiler_params=pltpu.CompilerParams(dimension_semantics=("arbitrary",),
                                             vmem_limit_bytes=VMEM_LIMIT),
        name="dispatch",
    )(ztiles, pos3, xrow)


def _experts_kernel(ea_ref, eb_ref, blk_ref, valid_ref,
                    xs_ref, wga_ref, wua_ref, wda_ref, wgb_ref, wub_ref, wdb_ref, ys_ref):
    del ea_ref, eb_ref, blk_ref
    i = pl.program_id(0)
    TM = xs_ref.shape[0] // TOK_PITCH

    @pl.when(valid_ref[i] > 0)
    def _():
        x = _from_token_tiles(xs_ref, TM, TOK_PITCH).astype(BF16)
        w = xs_ref[pl.ds(TOK_ROWS, TM, stride=TOK_PITCH), :]

        def expert(wg_ref, wu_ref, wd_ref, wt):
            g = _dot(x, wg_ref[...])
            u = _dot(x, wu_ref[...])
            return _dot((g * _sigmoid(g) * u * wt).astype(BF16), wd_ref[...])

        y = expert(wga_ref, wua_ref, wda_ref, w[:, 0:1]) + expert(wgb_ref, wub_ref, wdb_ref, w[:, 1:2])
        _to_token_tiles(ys_ref, y, TOK_ROWS)

    @pl.when(valid_ref[i] == 0)
    def _():
        ys_ref[...] = jnp.zeros(ys_ref.shape, F32)


def _experts(ea, eb, blk, valid, xs, wg, wu, wd):
    tm = TM_MOE
    n_rows = xs.shape[0] // TOK_PITCH
    nt = n_rows // tm
    xmap = lambda i, ea, eb, blk, valid: (blk[i], 0)
    amap = lambda i, ea, eb, blk, valid: (ea[i], 0, 0)
    bmap = lambda i, ea, eb, blk, valid: (eb[i], 0, 0)
    up_spec = lambda m: pl.BlockSpec((None, D_MODEL, D_EXPERT), m)
    dn_spec = lambda m: pl.BlockSpec((None, D_EXPERT, D_MODEL), m)
    grid_spec = pltpu.PrefetchScalarGridSpec(
        num_scalar_prefetch=4,
        grid=(nt,),
        in_specs=[pl.BlockSpec((tm * TOK_PITCH, LANES), xmap),
                  up_spec(amap), up_spec(amap), dn_spec(amap),
                  up_spec(bmap), up_spec(bmap), dn_spec(bmap)],
        out_specs=pl.BlockSpec((tm * TOK_ROWS, LANES), lambda i, ea, eb, blk, valid: (i, 0)),
    )
    return pl.pallas_call(
        _experts_kernel,
        grid_spec=grid_spec,
        out_shape=jax.ShapeDtypeStruct((n_rows * TOK_ROWS, LANES), F32),
        compiler_params=pltpu.CompilerParams(dimension_semantics=("arbitrary",),
                                             vmem_limit_bytes=VMEM_LIMIT),
        name="experts",
    )(ea, eb, blk, valid, xs, wg, wu, wd, wg, wu, wd)


def _combine_kernel(pos_cur_ref, pos_nxt_ref, h1_ref, p_ref, wpp_ref, wpg_ref, gple_ref, gfin_ref,
                    ys_hbm, o_ref, buf_ref, sem):
    TM = h1_ref.shape[0]
    i = pl.program_id(0)
    n = pl.num_programs(0)
    slot = i % 2

    def copy(pos_ref, s, r):
        src = pl.multiple_of(pos_ref[0, r] * TOK_ROWS, TOK_ROWS)
        dst = pl.multiple_of(r * TOK_PITCH, PITCH_ALIGN)
        return pltpu.make_async_copy(ys_hbm.at[pl.ds(src, TOK_ROWS)], buf_ref.at[s, pl.ds(dst, TOK_ROWS)],
                                     sem.at[s])

    def gather(pos_ref, s):
        def start(rb, c):
            for u in range(DMA_ISSUE_UNROLL):
                copy(pos_ref, s, DMA_ISSUE_UNROLL * rb + u).start(priority=u % 2)
            return c
        lax.fori_loop(0, TM // DMA_ISSUE_UNROLL, start, 0)

    @pl.when(i == 0)
    def _():
        gather(pos_cur_ref, 0)

    @pl.when(i + 1 < n)
    def _():
        gather(pos_nxt_ref, 1 - slot)

    def wait(r, c):
        copy(pos_cur_ref, slot, r).wait()
        return c

    lax.fori_loop(0, TM, wait, 0, unroll=DMA_WAIT_UNROLL)

    h2 = h1_ref[...] + _from_token_tiles(buf_ref.at[slot], TM, TOK_PITCH)
    n3 = _rms(h2, gple_ref[...]).astype(BF16)
    gate = _sigmoid(_dot(n3, wpg_ref[...]))
    pp = _dot(p_ref[...].astype(BF16), wpp_ref[...])
    o_ref[...] = _rms(h2 + pp * gate, gfin_ref[...])


def _combine(pos, h1, p2, wpp, wpg, g_ple, g_final, ys):
    T = h1.shape[0]
    tm = TM_COMB
    nt = T // tm
    pos3 = pos.reshape(nt, 1, tm)
    row = lambda i: (i, 0)
    const = lambda i: (0, 0)
    return pl.pallas_call(
        _combine_kernel,
        grid=(nt,),
        in_specs=[pl.BlockSpec((None, 1, tm), lambda i: (i, 0, 0), memory_space=pltpu.SMEM),
                  pl.BlockSpec((None, 1, tm), lambda i: (jnp.minimum(i + 1, nt - 1), 0, 0),
                               memory_space=pltpu.SMEM),
                  pl.BlockSpec((tm, D_MODEL), row),
                  pl.BlockSpec((tm, D_PLE), row),
                  pl.BlockSpec(wpp.shape, const),
                  pl.BlockSpec(wpg.shape, const),
                  pl.BlockSpec((1, D_MODEL), const),
                  pl.BlockSpec((1, D_MODEL), const),
                  pl.BlockSpec(memory_space=pl.ANY)],
        out_specs=pl.BlockSpec((tm, D_MODEL), row),
        out_shape=jax.ShapeDtypeStruct((T, D_MODEL), F32),
        scratch_shapes=[pltpu.VMEM((2, tm * TOK_PITCH, LANES), F32),
                        pltpu.SemaphoreType.DMA((2,))],
        compiler_params=pltpu.CompilerParams(dimension_semantics=("arbitrary",),
                                             vmem_limit_bytes=VMEM_LIMIT),
        name="combine_ple",
    )(pos3, pos3, h1, p2, wpp, wpg, g_ple, g_final, ys)


def _tile_tables(counts, n_tiles):
    nt = (counts + TM_MOE - 1) // TM_MOE
    cum = jnp.cumsum(nt)
    off = (cum - nt) * TM_MOE
    total = cum[-1]
    tile = jnp.arange(n_tiles, dtype=I32)
    valid = (tile < total).astype(I32)
    blk = jnp.maximum(jnp.minimum(tile, total - 1), 0)
    tb = jnp.minimum(jnp.sum((cum[None, :] <= blk[:, None]).astype(I32), axis=1), N_BUCKETS - 1)
    pair_a = jnp.array([0, 0, 0, 1, 1, 2], I32)
    pair_b = jnp.array([1, 2, 3, 2, 3, 3], I32)
    grp = tb // N_PAIRS
    ea = grp * EXPERTS_PER_GROUP + pair_a[tb % N_PAIRS]
    eb = grp * EXPERTS_PER_GROUP + pair_b[tb % N_PAIRS]
    last = jnp.where(nt > 0, cum - 1, -1)
    tail = total + jnp.arange(N_BUCKETS, dtype=I32)
    ztiles = jnp.concatenate([last, jnp.where(tail < n_tiles, tail, -1)]).astype(I32)
    return off, ea, eb, blk, valid, ztiles


def kernel(x, p, w_in, conv_w, conv_b, dt_bias, a_log, d_skip, g_ssd, fox_fbias, w_out, g_mix, g_ffn,
           w_route_group, b_route_group, w_route_expert, b_route_expert, w_exp_gate, w_exp_up, w_exp_down,
           g_ple, w_ple_proj, w_ple_gate, g_final):
    B, S, _ = x.shape
    T = B * S
    assert S % (L_SSD * SSD_CHUNKS_PER_STEP) == 0 and S % TQ_FOX == 0 and S % TM_PROJ == 0
    assert T % TD_DISP == 0 and T % TM_COMB == 0
    x2 = x.reshape(T, D_MODEL)
    p2 = p[0].reshape(T, D_PLE)

    wi = w_in[0]
    o_xbc = D_SSD
    o_dt = o_xbc + D_XBC
    o_q = o_dt + SSD_HEADS
    o_k = o_q + D_FOX
    o_v = o_k + D_FOX
    o_f = o_v + D_FOX
    wz = wi[:, 0:o_xbc].astype(BF16)
    wxbc = wi[:, o_xbc:o_dt].astype(BF16)
    wq = wi[:, o_q:o_k].astype(BF16)
    wk = wi[:, o_k:o_v].astype(BF16)
    wvt = wi[:, o_v:o_f].T.astype(BF16)
    w_dt = wi[:, o_dt:o_q]
    w_f = wi[:, o_f:o_f + FOX_HEADS]
    wsm = jnp.pad(w_dt, ((0, 0), (0, LANES - SSD_HEADS))).astype(BF16)
    wsmt = jnp.concatenate([w_dt, w_f], axis=1).T.astype(BF16)
    lane_pad = lambda v: jnp.pad(v.reshape(1, -1), ((0, 0), (0, LANES - v.shape[-1])))
    colb = lane_pad(dt_bias[0])
    fbias = jnp.broadcast_to(fox_fbias[0][:, None], (FOX_HEADS, LANES))
    cola = lane_pad(a_log[0])
    rowb = jnp.broadcast_to(dt_bias[0][:, None], (SSD_HEADS, LANES))
    rowa = jnp.broadcast_to(a_log[0][:, None], (SSD_HEADS, LANES))
    dexp = jnp.repeat(d_skip[0], SSD_HEAD_DIM).reshape(1, D_SSD)
    expand = (jnp.arange(LANES)[:, None] == (jnp.arange(D_SSD) // SSD_HEAD_DIM)[None, :]).astype(BF16)
    col = jnp.arange(D_QK)
    head = col // LANES
    j_bias = col % LANES - jnp.where(head % 2 == 0, FOX_HEAD_DIM, 0)
    bias_col = (j_bias >= 0) & (j_bias < F_PARTS)
    qones = bias_col.astype(F32).reshape(1, D_QK)
    src_row = j_bias * FOX_HEADS + head
    place = (bias_col[None, :] & (jnp.arange((F_PARTS + 1) * FOX_HEADS)[:, None] == src_row[None, :])).astype(BF16)
    wo = w_out[0].astype(BF16)
    wr = jnp.pad(jnp.concatenate([w_route_group[0], w_route_expert[0]], axis=1).T,
                 ((0, ROUTE_ROWS - N_GROUPS - N_EXPERTS), (0, 0)))
    wr_hi = wr.astype(BF16)
    wr = jnp.concatenate([wr_hi, (wr - wr_hi.astype(F32)).astype(BF16)], axis=0)
    br = jnp.broadcast_to(jnp.pad(jnp.concatenate([b_route_group[0], b_route_expert[0]]),
                                  (0, ROUTE_ROWS - N_GROUPS - N_EXPERTS))[:, None], (ROUTE_ROWS, LANES))
    wg = w_exp_gate[0].reshape(N_EXPERTS, D_MODEL, D_EXPERT).astype(BF16)
    wu = w_exp_up[0].reshape(N_EXPERTS, D_MODEL, D_EXPERT).astype(BF16)
    wd = w_exp_down[0].reshape(N_EXPERTS, D_EXPERT, D_MODEL).astype(BF16)

    z, xbc, q, k, vt, sm, smt = _in_proj(x2, g_mix[0].reshape(1, -1), wz, wxbc, wq, wk, wvt, wsm, wsmt,
                                         fbias, qones, place, S)
    yssd = _ssd(xbc, z, sm, smt, conv_w[0].reshape(SSD_CONV, D_XBC), conv_b[0].reshape(1, -1),
                colb, cola, rowb, rowa, dexp, g_ssd[0].reshape(1, -1), expand, B, S)
    yfoxt = _fox(q, k, vt, B, S)
    h1, xrow, meta, cnt = _out_proj(x2, yssd, yfoxt, wo[0:D_SSD], wo[D_SSD:], g_ffn[0].reshape(1, -1), wr, br)

    n_rows = T + N_BUCKETS * TM_MOE
    bucket = meta[0].astype(I32)
    rank = meta[1].astype(I32)
    off, ea, eb, blk, valid, ztiles = _tile_tables(cnt[0:N_BUCKETS, 0].astype(I32), n_rows // TM_MOE)
    pos = rank
    for b in range(N_BUCKETS):
        pos = pos + jnp.where(bucket == b, off[b], 0)

    xs = _dispatch(ztiles, pos, xrow, n_rows)
    ys = _experts(ea, eb, blk, valid, xs, wg, wu, wd)
    out = _combine(pos, h1, p2, w_ple_proj[0].astype(BF16), w_ple_gate[0].astype(BF16),
                   g_ple[0].reshape(1, -1), g_final.reshape(1, -1), ys)
    return out.reshape(B, S, D_MODEL)
```

```python
import functools

import jax
import jax.numpy as jnp
from jax import lax
from jax.experimental import pallas as pl
from jax.experimental.pallas import tpu as pltpu

F32 = jnp.float32
BF16 = jnp.bfloat16
I32 = jnp.int32

D_MODEL = 1024
SSD_HEADS = 8
SSD_HEAD_DIM = 64
SSD_GROUPS = 2
SSD_STATE = 128
SSD_CONV = 4
D_SSD = SSD_HEADS * SSD_HEAD_DIM
D_XBC = D_SSD + 2 * SSD_GROUPS * SSD_STATE
FOX_HEADS = 8
FOX_HEAD_DIM = 64
D_FOX = FOX_HEADS * FOX_HEAD_DIM
N_GROUPS = 4
EXPERTS_PER_GROUP = 4
N_EXPERTS = N_GROUPS * EXPERTS_PER_GROUP
N_PAIRS = 6
N_BUCKETS = N_GROUPS * N_PAIRS
ROUTE_ROWS = 32
D_EXPERT = 512
D_PLE = 256
EPS = 1e-6
LOG2E = 1.4426950408889634
LANES = 128
SUBLANES = 8
BF16_SUBLANES = 16
TOK_ROWS = D_MODEL // LANES
TOK_PITCH = 12
PITCH_ALIGN = 4
D_QK = FOX_HEADS * LANES
F_PARTS = 3
VT_ROWS = FOX_HEAD_DIM + BF16_SUBLANES
VMEM_LIMIT = 52 * 1024 * 1024

TM_PROJ = 512
L_SSD = 256
SSD_CHUNKS_PER_STEP = 2
TQ_FOX = 512
TK_FOX = 256
TM_MOE = 256
EXPERT_TILES_PER_STEP = 2
TD_DISP = 1024
TM_COMB = 512
DMA_WAIT_UNROLL = 16
DMA_ISSUE_UNROLL = 8

_NT = (((1,), (1,)), ((), ()))
_TN = (((0,), (0,)), ((), ()))


def _rms(x, g):
    ms = jnp.mean(x * x, axis=-1, keepdims=True)
    return x * lax.rsqrt(ms + EPS) * g


def _sigmoid(x):
    return 1.0 / (1.0 + jnp.exp(-x))


def _softplus(x):
    return jnp.maximum(x, 0.0) + jnp.log(1.0 + jnp.exp(-jnp.abs(x)))


def _split_bf16(x, parts):
    out = []
    r = x
    for _ in range(parts):
        h = r.astype(BF16)
        out.append(h)
        r = r - h.astype(F32)
    return out


def _dot(a, b):
    return jnp.dot(a, b, preferred_element_type=F32)


def _dot_exact(a01, x, parts):
    acc = None
    for piece in _split_bf16(x, parts):
        t = _dot(a01, piece)
        acc = t if acc is None else acc + t
    return acc


def _to_token_tiles(ref, x, stride, offset=0):
    m = x.shape[0]
    for s in range(TOK_ROWS):
        ref[pl.ds(offset + s, m, stride=stride), :] = x[:, s * LANES:(s + 1) * LANES]


def _from_token_tiles(ref, m, stride, offset=0):
    return jnp.concatenate([ref[pl.ds(offset + s, m, stride=stride), :] for s in range(TOK_ROWS)], axis=1)


def _inproj_kernel(x_ref, g_ref, wz_ref, wxbc_ref, wq_ref, wk_ref, wvt_ref, wsm_ref, wsmt_ref,
                   fbias_ref, qones_ref, place_ref,
                   z_ref, xbc_ref, q_ref, k_ref, vt_ref, sm_ref, smt_ref, fcar_ref, *, tiles_per_seq):
    TM = x_ref.shape[0]

    @pl.when(pl.program_id(0) % tiles_per_seq == 0)
    def _():
        fcar_ref[...] = jnp.zeros(fcar_ref.shape, F32)

    hn = _rms(x_ref[...], g_ref[...]).astype(BF16)
    z_ref[...] = _dot(hn, wz_ref[...]).astype(BF16)
    xbc_ref[...] = _dot(hn, wxbc_ref[...]).astype(BF16)
    vt = lax.dot_general(wvt_ref[...], hn, _NT, preferred_element_type=F32).astype(BF16)
    ones_rows = jnp.ones((VT_ROWS - FOX_HEAD_DIM, TM), BF16)
    for h in range(FOX_HEADS):
        vt_ref[h * VT_ROWS:h * VT_ROWS + FOX_HEAD_DIM, :] = vt[h * FOX_HEAD_DIM:(h + 1) * FOX_HEAD_DIM, :]
        vt_ref[h * VT_ROWS + FOX_HEAD_DIM:(h + 1) * VT_ROWS, :] = ones_rows
    sm_ref[...] = _dot(hn, wsm_ref[...])
    smt = lax.dot_general(wsmt_ref[...], hn, _NT, preferred_element_type=F32)
    smt_ref[...] = smt[0:SSD_HEADS]

    logf = -_softplus(-(smt[SSD_HEADS:] + fbias_ref[:, 0:1]))
    ri = lax.broadcasted_iota(I32, (TM, TM), 0)
    ci = lax.broadcasted_iota(I32, (TM, TM), 1)
    triu = (ri <= ci).astype(BF16)
    fcum = fcar_ref[:, 0:1]
    for piece in _split_bf16(logf, 3):
        fcum = fcum + _dot(piece, triu)
    fcar_ref[...] = jnp.broadcast_to(fcum[:, TM - 1:TM], fcar_ref.shape)
    pieces = _split_bf16(fcum * (-LOG2E), F_PARTS)
    stacked = jnp.concatenate([p.astype(F32) for p in pieces] + [jnp.zeros((SUBLANES, TM), F32)],
                              axis=0).astype(BF16)
    kbias = lax.dot_general(stacked, place_ref[...], _TN, preferred_element_type=F32)
    lane = lax.broadcasted_iota(I32, (TM, LANES), 1)
    qc = _dot(hn, wq_ref[...]) * (FOX_HEAD_DIM ** -0.5 * LOG2E)
    kc = _dot(hn, wk_ref[...])
    for h in range(FOX_HEADS):
        pair = slice((h // 2) * LANES, (h // 2 + 1) * LANES)
        tile = slice(h * LANES, (h + 1) * LANES)
        own = (lane < FOX_HEAD_DIM) if h % 2 == 0 else (lane >= FOX_HEAD_DIM)
        q_ref[:, tile] = jnp.where(own, qc[:, pair], qones_ref[:, tile]).astype(BF16)
        k_ref[:, tile] = jnp.where(own, kc[:, pair], kbias[:, tile]).astype(BF16)


def _in_proj(x2, g_mix, wz, wxbc, wq, wk, wvt, wsm, wsmt, fbias, qones, place, S):
    T = x2.shape[0]
    tm = TM_PROJ
    row = lambda i: (i, 0)
    const = lambda i: (0, 0)
    full = lambda a: pl.BlockSpec(a.shape, const)
    return pl.pallas_call(
        functools.partial(_inproj_kernel, tiles_per_seq=S // tm),
        grid=(T // tm,),
        in_specs=[pl.BlockSpec((tm, D_MODEL), row), full(g_mix), full(wz), full(wxbc), full(wq), full(wk),
                  full(wvt), full(wsm), full(wsmt), full(fbias), full(qones), full(place)],
        out_specs=[pl.BlockSpec((tm, D_SSD), row),
                   pl.BlockSpec((tm, D_XBC), row),
                   pl.BlockSpec((tm, D_QK), row),
                   pl.BlockSpec((tm, D_QK), row),
                   pl.BlockSpec((FOX_HEADS * VT_ROWS, tm), lambda i: (0, i)),
                   pl.BlockSpec((tm, LANES), row),
                   pl.BlockSpec((SSD_HEADS, tm), lambda i: (0, i))],
        out_shape=[jax.ShapeDtypeStruct((T, D_SSD), BF16),
                   jax.ShapeDtypeStruct((T, D_XBC), BF16),
                   jax.ShapeDtypeStruct((T, D_QK), BF16),
                   jax.ShapeDtypeStruct((T, D_QK), BF16),
                   jax.ShapeDtypeStruct((FOX_HEADS * VT_ROWS, T), BF16),
                   jax.ShapeDtypeStruct((T, LANES), F32),
                   jax.ShapeDtypeStruct((SSD_HEADS, T), F32)],
        scratch_shapes=[pltpu.VMEM((SUBLANES, LANES), F32)],
        compiler_params=pltpu.CompilerParams(dimension_semantics=("arbitrary",),
                                             vmem_limit_bytes=VMEM_LIMIT),
        name="in_proj",
    )(x2, g_mix, wz, wxbc, wq, wk, wvt, wsm, wsmt, fbias, qones, place)


def _ssd_kernel(xbc_ref, z_ref, sm_ref, smt_ref, *rest):
    *param_refs, y_ref, tail_ref, st_ref = rest

    @pl.when(pl.program_id(1) == 0)
    def _():
        tail_ref[...] = jnp.zeros(tail_ref.shape, BF16)
        st_ref[...] = jnp.zeros(st_ref.shape, F32)

    for c in range(SSD_CHUNKS_PER_STEP):
        rows = pl.ds(c * L_SSD, L_SSD)
        _ssd_chunk(xbc_ref.at[rows], z_ref.at[rows], sm_ref.at[rows], smt_ref.at[:, rows], *param_refs,
                   y_ref.at[rows], tail_ref, st_ref)


def _ssd_chunk(xbc_ref, z_ref, sm_ref, smt_ref, convw_ref, convb_ref, colb_ref, cola_ref,
               rowb_ref, rowa_ref, dexp_ref, gssd_ref, expand_ref,
               y_ref, tail_ref, st_ref):
    L = xbc_ref.shape[0]
    R = SSD_HEADS // SSD_GROUPS
    TAIL = tail_ref.shape[0]

    ri = lax.broadcasted_iota(I32, (L, L), 0)
    ci = lax.broadcasted_iota(I32, (L, L), 1)
    causal = ri >= ci
    tril = causal.astype(BF16)
    triu = (ri <= ci).astype(BF16)

    zf = z_ref[...].astype(F32)
    zgate = zf * _sigmoid(zf)

    x_bf = xbc_ref[...]
    tail = tail_ref[...]
    hr = lax.broadcasted_iota(I32, (SUBLANES, TAIL), 0)
    hc = lax.broadcasted_iota(I32, (SUBLANES, TAIL), 1)
    conv = convb_ref[...] + convw_ref[SSD_CONV - 1:SSD_CONV, :] * x_bf.astype(F32)
    head = None
    for d in range(1, SSD_CONV):
        w_d = convw_ref[SSD_CONV - 1 - d:SSD_CONV - d, :]
        conv = conv + w_d * _dot((ri - d == ci).astype(BF16), x_bf)
        t = w_d * _dot((hc == hr + (TAIL - d)).astype(BF16), tail)
        head = t if head is None else head + t
    conv = jnp.concatenate([conv[0:SUBLANES] + head, conv[SUBLANES:]], axis=0)
    tail_ref[...] = xbc_ref[L - TAIL:L, :]
    xc = conv * _sigmoid(conv)
    xs = xc[:, 0:D_SSD]
    xs_bf = xs.astype(BF16)
    bm = xc[:, D_SSD:D_SSD + SSD_GROUPS * SSD_STATE].astype(BF16)
    cm = xc[:, D_SSD + SSD_GROUPS * SSD_STATE:].astype(BF16)

    dtr = _softplus(smt_ref[...] + rowb_ref[:, 0:1])
    adt_r = dtr * (-LOG2E * jnp.exp(rowa_ref[:, 0:1]))
    csr = None
    for part in _split_bf16(adt_r, 3):
        t = _dot(part, triu)
        csr = t if csr is None else csr + t

    dtc = _softplus(sm_ref[...] + colb_ref[...])
    adt_c = dtc * (-LOG2E * jnp.exp(cola_ref[...]))
    cs_c = _dot_exact(tril, adt_c, 3)
    cs_last = cs_c[L - 1:L, :]
    e1 = jnp.exp2(cs_c)
    wst = dtc * jnp.exp2(cs_last - cs_c)
    ex = expand_ref[...]
    p1 = _split_bf16(e1, 2)
    e1x = _dot(p1[0], ex) + _dot(p1[1], ex)
    p2 = _split_bf16(wst, 2)
    wstx = _dot(p2[0], ex) + _dot(p2[1], ex)

    xw = (xs * wstx).astype(BF16)
    dec8 = jnp.broadcast_to(jnp.exp2(csr[:, L - 1:L]), (SSD_HEADS, SSD_STATE))

    ydiag = []
    yoff = []
    for g in range(SSD_GROUPS):
        bg = bm[:, g * SSD_STATE:(g + 1) * SSD_STATE]
        cg = cm[:, g * SSD_STATE:(g + 1) * SSD_STATE]
        gmat = lax.dot_general(cg, bg, _NT, preferred_element_type=F32)
        s_old = st_ref[g]
        yoff.append(lax.dot_general(cg, s_old.astype(BF16), _NT, preferred_element_type=F32))
        for j in range(R):
            h = g * R + j
            seg = cs_c[:, h:h + 1] - csr[h:h + 1, :]
            lm = jnp.exp2(jnp.where(causal, seg, -jnp.inf))
            m = (gmat * lm * dtr[h:h + 1, :]).astype(BF16)
            ydiag.append(_dot(m, xs_bf[:, h * SSD_HEAD_DIM:(h + 1) * SSD_HEAD_DIM]))
        upd = lax.dot_general(xw[:, g * R * SSD_HEAD_DIM:(g + 1) * R * SSD_HEAD_DIM], bg, _TN,
                              preferred_element_type=F32)
        dec = jnp.concatenate(
            [jnp.broadcast_to(dec8[g * R + j:g * R + j + 1, :], (SSD_HEAD_DIM, SSD_STATE)) for j in range(R)],
            axis=0)
        st_ref[g] = dec * s_old + upd

    y = jnp.concatenate(ydiag, axis=1) + e1x * jnp.concatenate(yoff, axis=1) + dexp_ref[...] * xs
    y_ref[...] = _rms(y * zgate, gssd_ref[...]).astype(BF16)


def _ssd(xbc, z, sm, smt, convw, convb, colb, cola, rowb, rowa, dexp, gssd, expand, B, S):
    L = L_SSD * SSD_CHUNKS_PER_STEP
    nc = S // L
    T = B * S
    row = lambda b, c: (b * nc + c, 0)
    const = lambda b, c: (0, 0)
    full = lambda a: pl.BlockSpec(a.shape, const)
    return pl.pallas_call(
        _ssd_kernel,
        grid=(B, nc),
        in_specs=[pl.BlockSpec((L, D_XBC), row),
                  pl.BlockSpec((L, D_SSD), row),
                  pl.BlockSpec((L, LANES), row),
                  pl.BlockSpec((SSD_HEADS, L), lambda b, c: (0, b * nc + c)),
                  full(convw), full(convb), full(colb), full(cola), full(rowb), full(rowa),
                  full(dexp), full(gssd), full(expand)],
        out_specs=pl.BlockSpec((L, D_SSD), row),
        out_shape=jax.ShapeDtypeStruct((T, D_SSD), BF16),
        scratch_shapes=[pltpu.VMEM((BF16_SUBLANES, D_XBC), BF16),
                        pltpu.VMEM((SSD_GROUPS, (SSD_HEADS // SSD_GROUPS) * SSD_HEAD_DIM, SSD_STATE), F32)],
        compiler_params=pltpu.CompilerParams(dimension_semantics=("arbitrary", "arbitrary"),
                                             vmem_limit_bytes=VMEM_LIMIT),
        name="ssd",
    )(xbc, z, sm, smt, convw, convb, colb, cola, rowb, rowa, dexp, gssd, expand)


def _fox_kernel(q_ref, k_ref, vt_ref, o_ref, sa_ref, sb_ref):
    TQ = q_ref.shape[0]
    TK = TK_FOX
    assert TQ == 2 * TK
    n_pairs = pl.program_id(1)
    n_full = 2 * n_pairs
    key_idx = lax.broadcasted_iota(I32, (TK, TQ), 0)
    qry_idx = lax.broadcasted_iota(I32, (TK, TQ), 1)
    heads = range(FOX_HEADS)
    qs = [q_ref[:, h * LANES:(h + 1) * LANES] for h in heads]

    def scores(j, s_ref, h):
        ks = pl.multiple_of(j * TK, TK)
        s = lax.dot_general(k_ref[pl.ds(ks, TK), h * LANES:(h + 1) * LANES], qs[h], _NT,
                            preferred_element_type=F32)
        s_ref[h] = s
        return jnp.max(s, axis=0, keepdims=True)

    def softmax_pv(j, s_ref, h, state, diag):
        ks = pl.multiple_of(j * TK, TK)
        m, acc, smax = state
        s = s_ref[h]
        if diag is not None:
            s = jnp.where(key_idx + diag * TK <= qry_idx, s, -jnp.inf)
            smax = jnp.max(s, axis=0, keepdims=True)
        mn = jnp.maximum(m, smax)
        p = jnp.exp2(s - mn).astype(BF16)
        acc = jnp.exp2(m - mn) * acc + _dot(vt_ref[h * VT_ROWS:(h + 1) * VT_ROWS, pl.ds(ks, TK)], p)
        return mn, acc

    def step(j, cur_ref, nxt_ref, carry, diag):
        new = []
        for h in heads:
            smax_nxt = scores(j + 1, nxt_ref, h) if nxt_ref is not None else carry[h][2]
            new.append(softmax_pv(j, cur_ref, h, carry[h], diag) + (smax_nxt,))
        return tuple(new)

    def pair(t, carry):
        carry = step(2 * t, sa_ref, sb_ref, carry, None)
        return step(2 * t + 1, sb_ref, sa_ref, carry, None)

    carry = tuple((jnp.full((1, TQ), -1e30, F32), jnp.zeros((VT_ROWS, TQ), F32), scores(0, sa_ref, h))
                  for h in heads)
    carry = lax.fori_loop(0, n_pairs, pair, carry)
    carry = step(n_full, sa_ref, sb_ref, carry, 0)
    carry = step(n_full + 1, sb_ref, None, carry, 1)
    o_ref[...] = jnp.concatenate(
        [acc[0:FOX_HEAD_DIM] / acc[FOX_HEAD_DIM:FOX_HEAD_DIM + 1] for _, acc, _ in carry], axis=0).astype(BF16)


def _fox(q, k, vt, B, S):
    TQ = TQ_FOX
    nq = S // TQ
    T = B * S
    return pl.pallas_call(
        _fox_kernel,
        grid=(B, nq),
        in_specs=[pl.BlockSpec((TQ, D_QK), lambda b, i: (b * nq + i, 0)),
                  pl.BlockSpec((S, D_QK), lambda b, i: (b, 0)),
                  pl.BlockSpec((FOX_HEADS * VT_ROWS, S), lambda b, i: (0, b))],
        out_specs=pl.BlockSpec((D_FOX, TQ), lambda b, i: (0, b * nq + i)),
        out_shape=jax.ShapeDtypeStruct((D_FOX, T), BF16),
        scratch_shapes=[pltpu.VMEM((FOX_HEADS, TK_FOX, TQ), F32), pltpu.VMEM((FOX_HEADS, TK_FOX, TQ), F32)],
        compiler_params=pltpu.CompilerParams(dimension_semantics=("arbitrary", "arbitrary"),
                                             vmem_limit_bytes=VMEM_LIMIT),
        name="fox",
    )(q, k, vt)


def _outproj_kernel(x_ref, ys_ref, yft_ref, wo1_ref, wo2_ref, g_ref, wr_ref, br_ref,
                    h1_ref, xrow_ref, meta_ref, cnt_ref, carry_ref):
    TM = x_ref.shape[0]
    i = pl.program_id(0)

    @pl.when(i == 0)
    def _():
        carry_ref[...] = jnp.zeros(carry_ref.shape, F32)

    h1 = (x_ref[...] + _dot(ys_ref[...], wo1_ref[...])
          + lax.dot_general(yft_ref[...], wo2_ref[...], _TN, preferred_element_type=F32))
    h1_ref[...] = h1
    hn = _rms(h1, g_ref[...])
    h_hi, h_lo = _split_bf16(hn, 2)
    lg = lax.dot_general(wr_ref[...], h_hi, _NT, preferred_element_type=F32)
    logits = (lg[0:ROUTE_ROWS] + lg[ROUTE_ROWS:]
              + lax.dot_general(wr_ref[0:ROUTE_ROWS, :], h_lo, _NT, preferred_element_type=F32)
              + br_ref[:, 0:1])
    row = lax.broadcasted_iota(I32, (ROUTE_ROWS, TM), 0).astype(F32)
    ninf = -jnp.inf
    far = float(ROUTE_ROWS)
    gl = jnp.where(row < N_GROUPS, logits, ninf)
    gmax = jnp.max(gl, axis=0, keepdims=True)
    p_grp = 1.0 / jnp.sum(jnp.exp(gl - gmax), axis=0, keepdims=True)
    g_idx = jnp.min(jnp.where(gl == gmax, row, far), axis=0, keepdims=True)
    e0 = N_GROUPS + EXPERTS_PER_GROUP * g_idx
    el = jnp.where((row >= e0) & (row < e0 + EXPERTS_PER_GROUP), logits, ninf)
    v1 = jnp.max(el, axis=0, keepdims=True)
    i1 = jnp.min(jnp.where(el == v1, row, far), axis=0, keepdims=True)
    el2 = jnp.where(row == i1, ninf, el)
    v2 = jnp.max(el2, axis=0, keepdims=True)
    i2 = jnp.min(jnp.where(el2 == v2, row, far), axis=0, keepdims=True)
    e2 = jnp.exp(v2 - v1)
    w1 = p_grp / (1.0 + e2)
    w2 = p_grp * e2 / (1.0 + e2)
    l1 = i1 - e0
    l2 = i2 - e0
    first = l1 < l2
    a = jnp.where(first, l1, l2)
    b = jnp.where(first, l2, l1)
    wa = jnp.where(first, w1, w2)
    wb = jnp.where(first, w2, w1)
    pair = a * (5.0 - a) * 0.5 + b - 1.0
    bucket = g_idx * N_PAIRS + pair

    onehot = (row == bucket).astype(BF16)
    ri = lax.broadcasted_iota(I32, (TM, TM), 0)
    ci = lax.broadcasted_iota(I32, (TM, TM), 1)
    incl = _dot(onehot, (ri <= ci).astype(BF16))
    oh = onehot.astype(F32)
    rank = jnp.sum((incl - oh + carry_ref[:, 0:1]) * oh, axis=0, keepdims=True)
    carry_ref[...] = carry_ref[...] + jnp.broadcast_to(incl[:, TM - 1:TM], carry_ref.shape)
    cnt_ref[...] = carry_ref[...]

    r8 = lax.broadcasted_iota(I32, (SUBLANES, TM), 0)
    meta_ref[...] = jnp.where(r8 == 0, bucket, jnp.where(r8 == 1, rank, 0.0))
    wrows = jnp.where(r8 == 0, wa, jnp.where(r8 == 1, wb, 0.0))
    eye = (lax.broadcasted_iota(I32, (SUBLANES, LANES), 0)
           == lax.broadcasted_iota(I32, (SUBLANES, LANES), 1)).astype(BF16)
    wcols = None
    for piece in _split_bf16(wrows, 3):
        t = lax.dot_general(piece, eye, _TN, preferred_element_type=F32)
        wcols = t if wcols is None else wcols + t
    _to_token_tiles(xrow_ref, hn, TOK_PITCH)
    xrow_ref[pl.ds(TOK_ROWS, TM, stride=TOK_PITCH), :] = wcols
    for s in range(TOK_ROWS + 1, TOK_PITCH):
        xrow_ref[pl.ds(s, TM, stride=TOK_PITCH), :] = jnp.zeros((TM, LANES), F32)


def _out_proj(x2, yssd, yfoxt, wo1, wo2, g_ffn, wr, br):
    T = x2.shape[0]
    tm = TM_PROJ
    row = lambda i: (i, 0)
    const = lambda i: (0, 0)
    return pl.pallas_call(
        _outproj_kernel,
        grid=(T // tm,),
        in_specs=[pl.BlockSpec((tm, D_MODEL), row),
                  pl.BlockSpec((tm, D_SSD), row),
                  pl.BlockSpec((D_FOX, tm), lambda i: (0, i)),
                  pl.BlockSpec(wo1.shape, const),
                  pl.BlockSpec(wo2.shape, const),
                  pl.BlockSpec((1, D_MODEL), const),
                  pl.BlockSpec(wr.shape, const),
                  pl.BlockSpec(br.shape, const)],
        out_specs=[pl.BlockSpec((tm, D_MODEL), row),
                   pl.BlockSpec((tm * TOK_PITCH, LANES), row),
                   pl.BlockSpec((SUBLANES, tm), lambda i: (0, i)),
                   pl.BlockSpec((ROUTE_ROWS, LANES), const)],
        out_shape=[jax.ShapeDtypeStruct((T, D_MODEL), F32),
                   jax.ShapeDtypeStruct((T * TOK_PITCH, LANES), F32),
                   jax.ShapeDtypeStruct((SUBLANES, T), F32),
                   jax.ShapeDtypeStruct((ROUTE_ROWS, LANES), F32)],
        scratch_shapes=[pltpu.VMEM((ROUTE_ROWS, LANES), F32)],
        compiler_params=pltpu.CompilerParams(dimension_semantics=("arbitrary",),
                                             vmem_limit_bytes=VMEM_LIMIT),
        name="out_proj_router",
    )(x2, yssd, yfoxt, wo1, wo2, g_ffn, wr, br)


def _dispatch_kernel(ztile_ref, pos_ref, src_ref, dst_hbm, zero_ref, sem, zsem):
    TD = pos_ref.shape[-1]
    tile_rows = TM_MOE * TOK_PITCH

    @pl.when(pl.program_id(0) == 0)
    def _():
        zero_ref[...] = jnp.zeros(zero_ref.shape, F32)

        def zcopy(e):
            dst = pl.multiple_of(ztile_ref[e] * tile_rows, tile_rows)
            return pltpu.make_async_copy(zero_ref, dst_hbm.at[pl.ds(dst, tile_rows)], zsem)

        def zstart(e, c):
            @pl.when(ztile_ref[e] >= 0)
            def _():
                zcopy(e).start()
            return c

        def zwait(e, c):
            @pl.when(ztile_ref[e] >= 0)
            def _():
                zcopy(e).wait()
            return c

        lax.fori_loop(0, ztile_ref.shape[0], zstart, 0)
        lax.fori_loop(0, ztile_ref.shape[0], zwait, 0)

    def copy(r):
        dst = pl.multiple_of(pos_ref[0, r] * TOK_PITCH, PITCH_ALIGN)
        src = pl.multiple_of(r * TOK_PITCH, PITCH_ALIGN)
        return pltpu.make_async_copy(src_ref.at[pl.ds(src, TOK_PITCH)], dst_hbm.at[pl.ds(dst, TOK_PITCH)], sem)

    def start(rb, c):
        for u in range(DMA_ISSUE_UNROLL):
            copy(DMA_ISSUE_UNROLL * rb + u).start(priority=u % 2)
        return c

    lax.fori_loop(0, TD // DMA_ISSUE_UNROLL, start, 0)

    def wait(r, c):
        copy(r).wait()
        return c

    lax.fori_loop(0, TD, wait, 0, unroll=DMA_WAIT_UNROLL)


def _dispatch(ztiles, pos, xrow, n_rows):
    T = pos.shape[0]
    td = TD_DISP
    pos3 = pos.reshape(T // td, 1, td)
    grid_spec = pltpu.PrefetchScalarGridSpec(
        num_scalar_prefetch=1,
        grid=(T // td,),
        in_specs=[pl.BlockSpec((None, 1, td), lambda i, zt: (i, 0, 0), memory_space=pltpu.SMEM),
                  pl.BlockSpec((td * TOK_PITCH, LANES), lambda i, zt: (i, 0))],
        out_specs=pl.BlockSpec(memory_space=pl.ANY),
        scratch_shapes=[pltpu.VMEM((TM_MOE * TOK_PITCH, LANES), F32),
                        pltpu.SemaphoreType.DMA(()), pltpu.SemaphoreType.DMA(())],
    )
    return pl.pallas_call(
        _dispatch_kernel,
        grid_spec=grid_spec,
        out_shape=jax.ShapeDtypeStruct((n_rows * TOK_PITCH, LANES), F32),
        compiler_params=pltpu.CompilerParams(dimension_semantics=("arbitrary",),
                                             vmem_limit_bytes=VMEM_LIMIT),
        name="dispatch",
    )(ztiles, pos3, xrow)


def _experts_kernel(ea_ref, eb_ref, blk_ref, valid_ref, xs_ref, *refs):
    del ea_ref, eb_ref, blk_ref
    w_refs, ys_ref = refs[:-1], refs[-1]
    n = EXPERT_TILES_PER_STEP
    TM = TM_MOE
    first = pl.program_id(0) * n

    def xs_tile(t):
        return xs_ref.at[pl.ds(t * TM * TOK_PITCH, TM * TOK_PITCH)]

    def ys_tile(t):
        return ys_ref.at[pl.ds(t * TM * TOK_ROWS, TM * TOK_ROWS)]

    def compute(t):
        wga_ref, wua_ref, wda_ref, wgb_ref, wub_ref, wdb_ref = w_refs[6 * t:6 * t + 6]
        x = _from_token_tiles(xs_tile(t), TM, TOK_PITCH).astype(BF16)
        w = xs_tile(t)[pl.ds(TOK_ROWS, TM, stride=TOK_PITCH), :]

        def expert(wg_ref, wu_ref, wd_ref, wt):
            g = _dot(x, wg_ref[...])
            u = _dot(x, wu_ref[...])
            return _dot((g * _sigmoid(g) * u * wt).astype(BF16), wd_ref[...])

        y = expert(wga_ref, wua_ref, wda_ref, w[:, 0:1]) + expert(wgb_ref, wub_ref, wdb_ref, w[:, 1:2])
        _to_token_tiles(ys_tile(t), y, TOK_ROWS)

    def clear(t):
        ys_tile(t)[...] = jnp.zeros((TM * TOK_ROWS, LANES), F32)

    in_use = valid_ref[first]
    for t in range(1, n):
        in_use = in_use + valid_ref[first + t]
    for k in range(n + 1):
        @pl.when(in_use == k)
        def _(k=k):
            for t in range(n):
                compute(t) if t < k else clear(t)


def _experts(ea, eb, blk, valid, xs, wg, wu, wd):
    tm = TM_MOE
    n = EXPERT_TILES_PER_STEP
    n_rows = xs.shape[0] // TOK_PITCH
    nt = n_rows // tm
    assert nt % n == 0
    xmap = lambda i, ea, eb, blk, valid: (blk[i * n] // n, 0)
    up_spec = lambda m: pl.BlockSpec((None, D_MODEL, D_EXPERT), m)
    dn_spec = lambda m: pl.BlockSpec((None, D_EXPERT, D_MODEL), m)
    w_specs, w_args = [], []
    for t in range(n):
        amap = lambda i, ea, eb, blk, valid, t=t: (ea[i * n + t], 0, 0)
        bmap = lambda i, ea, eb, blk, valid, t=t: (eb[i * n + t], 0, 0)
        w_specs += [up_spec(amap), up_spec(amap), dn_spec(amap), up_spec(bmap), up_spec(bmap), dn_spec(bmap)]
        w_args += [wg, wu, wd, wg, wu, wd]
    grid_spec = pltpu.PrefetchScalarGridSpec(
        num_scalar_prefetch=4,
        grid=(nt // n,),
        in_specs=[pl.BlockSpec((n * tm * TOK_PITCH, LANES), xmap)] + w_specs,
        out_specs=pl.BlockSpec((n * tm * TOK_ROWS, LANES), lambda i, ea, eb, blk, valid: (i, 0)),
    )
    return pl.pallas_call(
        _experts_kernel,
        grid_spec=grid_spec,
        out_shape=jax.ShapeDtypeStruct((n_rows * TOK_ROWS, LANES), F32),
        compiler_params=pltpu.CompilerParams(dimension_semantics=("arbitrary",),
                                             vmem_limit_bytes=VMEM_LIMIT),
        name="experts",
    )(ea, eb, blk, valid, xs, *w_args)


def _combine_kernel(pos_cur_ref, pos_nxt_ref, h1_ref, p_ref, wpp_ref, wpg_ref, gple_ref, gfin_ref,
                    ys_hbm, o_ref, buf_ref, sem):
    TM = h1_ref.shape[0]
    i = pl.program_id(0)
    n = pl.num_programs(0)
    slot = i % 2

    def copy(pos_ref, s, r):
        src = pl.multiple_of(pos_ref[0, r] * TOK_ROWS, TOK_ROWS)
        dst = pl.multiple_of(r * TOK_PITCH, PITCH_ALIGN)
        return pltpu.make_async_copy(ys_hbm.at[pl.ds(src, TOK_ROWS)], buf_ref.at[s, pl.ds(dst, TOK_ROWS)],
                                     sem.at[s])

    def gather(pos_ref, s):
        def start(rb, c):
            for u in range(DMA_ISSUE_UNROLL):
                copy(pos_ref, s, DMA_ISSUE_UNROLL * rb + u).start(priority=u % 2)
            return c
        lax.fori_loop(0, TM // DMA_ISSUE_UNROLL, start, 0)

    @pl.when(i == 0)
    def _():
        gather(pos_cur_ref, 0)

    @pl.when(i + 1 < n)
    def _():
        gather(pos_nxt_ref, 1 - slot)

    def wait(r, c):
        copy(pos_cur_ref, slot, r).wait()
        return c

    lax.fori_loop(0, TM, wait, 0, unroll=DMA_WAIT_UNROLL)

    h2 = h1_ref[...] + _from_token_tiles(buf_ref.at[slot], TM, TOK_PITCH)
    n3 = _rms(h2, gple_ref[...]).astype(BF16)
    gate = _sigmoid(_dot(n3, wpg_ref[...]))
    pp = _dot(p_ref[...].astype(BF16), wpp_ref[...])
    o_ref[...] = _rms(h2 + pp * gate, gfin_ref[...])


def _combine(pos, h1, p2, wpp, wpg, g_ple, g_final, ys):
    T = h1.shape[0]
    tm = TM_COMB
    nt = T // tm
    pos3 = pos.reshape(nt, 1, tm)
    row = lambda i: (i, 0)
    const = lambda i: (0, 0)
    return pl.pallas_call(
        _combine_kernel,
        grid=(nt,),
        in_specs=[pl.BlockSpec((None, 1, tm), lambda i: (i, 0, 0), memory_space=pltpu.SMEM),
                  pl.BlockSpec((None, 1, tm), lambda i: (jnp.minimum(i + 1, nt - 1), 0, 0),
                               memory_space=pltpu.SMEM),
                  pl.BlockSpec((tm, D_MODEL), row),
                  pl.BlockSpec((tm, D_PLE), row),
                  pl.BlockSpec(wpp.shape, const),
                  pl.BlockSpec(wpg.shape, const),
                  pl.BlockSpec((1, D_MODEL), const),
                  pl.BlockSpec((1, D_MODEL), const),
                  pl.BlockSpec(memory_space=pl.ANY)],
        out_specs=pl.BlockSpec((tm, D_MODEL), row),
        out_shape=jax.ShapeDtypeStruct((T, D_MODEL), F32),
        scratch_shapes=[pltpu.VMEM((2, tm * TOK_PITCH, LANES), F32),
                        pltpu.SemaphoreType.DMA((2,))],
        compiler_params=pltpu.CompilerParams(dimension_semantics=("arbitrary",),
                                             vmem_limit_bytes=VMEM_LIMIT),
        name="combine_ple",
    )(pos3, pos3, h1, p2, wpp, wpg, g_ple, g_final, ys)


def _tile_tables(counts, n_tiles):
    nt = (counts + TM_MOE - 1) // TM_MOE
    cum = jnp.cumsum(nt)
    off = (cum - nt) * TM_MOE
    total = cum[-1]
    tile = jnp.arange(n_tiles, dtype=I32)
    valid = (tile < total).astype(I32)
    blk = jnp.maximum(jnp.minimum(tile, total - 1), 0)
    tb = jnp.minimum(jnp.sum((cum[None, :] <= blk[:, None]).astype(I32), axis=1), N_BUCKETS - 1)
    pair_a = jnp.array([0, 0, 0, 1, 1, 2], I32)
    pair_b = jnp.array([1, 2, 3, 2, 3, 3], I32)
    grp = tb // N_PAIRS
    ea = grp * EXPERTS_PER_GROUP + pair_a[tb % N_PAIRS]
    eb = grp * EXPERTS_PER_GROUP + pair_b[tb % N_PAIRS]
    last = jnp.where(nt > 0, cum - 1, -1)
    tail = total + jnp.arange(N_BUCKETS, dtype=I32)
    ztiles = jnp.concatenate([last, jnp.where(tail < n_tiles, tail, -1)]).astype(I32)
    return off, ea, eb, blk, valid, ztiles


def kernel(x, p, w_in, conv_w, conv_b, dt_bias, a_log, d_skip, g_ssd, fox_fbias, w_out, g_mix, g_ffn,
           w_route_group, b_route_group, w_route_expert, b_route_expert, w_exp_gate, w_exp_up, w_exp_down,
           g_ple, w_ple_proj, w_ple_gate, g_final):
    B, S, _ = x.shape
    T = B * S
    assert S % (L_SSD * SSD_CHUNKS_PER_STEP) == 0 and S % TQ_FOX == 0 and S % TM_PROJ == 0
    assert T % TD_DISP == 0 and T % TM_COMB == 0
    x2 = x.reshape(T, D_MODEL)
    p2 = p[0].reshape(T, D_PLE)

    wi = w_in[0]
    o_xbc = D_SSD
    o_dt = o_xbc + D_XBC
    o_q = o_dt + SSD_HEADS
    o_k = o_q + D_FOX
    o_v = o_k + D_FOX
    o_f = o_v + D_FOX
    wz = wi[:, 0:o_xbc].astype(BF16)
    wxbc = wi[:, o_xbc:o_dt].astype(BF16)
    wq = wi[:, o_q:o_k].astype(BF16)
    wk = wi[:, o_k:o_v].astype(BF16)
    wvt = wi[:, o_v:o_f].T.astype(BF16)
    w_dt = wi[:, o_dt:o_q]
    w_f = wi[:, o_f:o_f + FOX_HEADS]
    wsm = jnp.pad(w_dt, ((0, 0), (0, LANES - SSD_HEADS))).astype(BF16)
    wsmt = jnp.concatenate([w_dt, w_f], axis=1).T.astype(BF16)
    lane_pad = lambda v: jnp.pad(v.reshape(1, -1), ((0, 0), (0, LANES - v.shape[-1])))
    colb = lane_pad(dt_bias[0])
    fbias = jnp.broadcast_to(fox_fbias[0][:, None], (FOX_HEADS, LANES))
    cola = lane_pad(a_log[0])
    rowb = jnp.broadcast_to(dt_bias[0][:, None], (SSD_HEADS, LANES))
    rowa = jnp.broadcast_to(a_log[0][:, None], (SSD_HEADS, LANES))
    dexp = jnp.repeat(d_skip[0], SSD_HEAD_DIM).reshape(1, D_SSD)
    expand = (jnp.arange(LANES)[:, None] == (jnp.arange(D_SSD) // SSD_HEAD_DIM)[None, :]).astype(BF16)
    col = jnp.arange(D_QK)
    head = col // LANES
    j_bias = col % LANES - jnp.where(head % 2 == 0, FOX_HEAD_DIM, 0)
    bias_col = (j_bias >= 0) & (j_bias < F_PARTS)
    qones = bias_col.astype(F32).reshape(1, D_QK)
    src_row = j_bias * FOX_HEADS + head
    place = (bias_col[None, :] & (jnp.arange((F_PARTS + 1) * FOX_HEADS)[:, None] == src_row[None, :])).astype(BF16)
    wo = w_out[0].astype(BF16)
    wr = jnp.pad(jnp.concatenate([w_route_group[0], w_route_expert[0]], axis=1).T,
                 ((0, ROUTE_ROWS - N_GROUPS - N_EXPERTS), (0, 0)))
    wr_hi = wr.astype(BF16)
    wr = jnp.concatenate([wr_hi, (wr - wr_hi.astype(F32)).astype(BF16)], axis=0)
    br = jnp.broadcast_to(jnp.pad(jnp.concatenate([b_route_group[0], b_route_expert[0]]),
                                  (0, ROUTE_ROWS - N_GROUPS - N_EXPERTS))[:, None], (ROUTE_ROWS, LANES))
    wg = w_exp_gate[0].reshape(N_EXPERTS, D_MODEL, D_EXPERT).astype(BF16)
    wu = w_exp_up[0].reshape(N_EXPERTS, D_MODEL, D_EXPERT).astype(BF16)
    wd = w_exp_down[0].reshape(N_EXPERTS, D_EXPERT, D_MODEL).astype(BF16)

    z, xbc, q, k, vt, sm, smt = _in_proj(x2, g_mix[0].reshape(1, -1), wz, wxbc, wq, wk, wvt, wsm, wsmt,
                                         fbias, qones, place, S)
    yssd = _ssd(xbc, z, sm, smt, conv_w[0].reshape(SSD_CONV, D_XBC), conv_b[0].reshape(1, -1),
                colb, cola, rowb, rowa, dexp, g_ssd[0].reshape(1, -1), expand, B, S)
    yfoxt = _fox(q, k, vt, B, S)
    h1, xrow, meta, cnt = _out_proj(x2, yssd, yfoxt, wo[0:D_SSD], wo[D_SSD:], g_ffn[0].reshape(1, -1), wr, br)

    n_rows = T + N_BUCKETS * TM_MOE
    bucket = meta[0].astype(I32)
    rank = meta[1].astype(I32)
    off, ea, eb, blk, valid, ztiles = _tile_tables(cnt[0:N_BUCKETS, 0].astype(I32), n_rows // TM_MOE)
    pos = rank
    for b in range(N_BUCKETS):
        pos = pos + jnp.where(bucket == b, off[b], 0)

    xs = _dispatch(ztiles, pos, xrow, n_rows)
    ys = _experts(ea, eb, blk, valid, xs, wg, wu, wd)
    out = _combine(pos, h1, p2, w_ple_proj[0].astype(BF16), w_ple_gate[0].astype(BF16),
                   g_ple[0].reshape(1, -1), g_final.reshape(1, -1), ys)
    return out.reshape(B, S, D_MODEL)
```

```python
import functools

import jax
import jax.numpy as jnp
from jax import lax
from jax.experimental import pallas as pl
from jax.experimental.pallas import tpu as pltpu

F32 = jnp.float32
BF16 = jnp.bfloat16
I32 = jnp.int32

D_MODEL = 1024
SSD_HEADS = 8
SSD_HEAD_DIM = 64
SSD_GROUPS = 2
SSD_STATE = 128
SSD_CONV = 4
D_SSD = SSD_HEADS * SSD_HEAD_DIM
D_XBC = D_SSD + 2 * SSD_GROUPS * SSD_STATE
FOX_HEADS = 8
FOX_HEAD_DIM = 64
D_FOX = FOX_HEADS * FOX_HEAD_DIM
N_GROUPS = 4
EXPERTS_PER_GROUP = 4
N_EXPERTS = N_GROUPS * EXPERTS_PER_GROUP
N_PAIRS = 6
N_BUCKETS = N_GROUPS * N_PAIRS
ROUTE_ROWS = 32
D_EXPERT = 512
D_PLE = 256
EPS = 1e-6
LOG2E = 1.4426950408889634
LANES = 128
SUBLANES = 8
BF16_SUBLANES = 16
TOK_ROWS = D_MODEL // LANES
TOK_PITCH = 12
PITCH_ALIGN = 4
D_QK = FOX_HEADS * LANES
F_PARTS = 3
VT_ROWS = FOX_HEAD_DIM + BF16_SUBLANES
VMEM_LIMIT = 52 * 1024 * 1024

TM_PROJ = 512
INPROJ_TILES_PER_STEP = 2
L_SSD = 256
SSD_CHUNKS_PER_STEP = 4
TQ_FOX = 512
TK_FOX = 256
TM_MOE = 256
EXPERT_TILES_PER_STEP = 2
TD_DISP = 2048
TM_COMB = 512
DMA_WAIT_UNROLL = 16
DMA_ISSUE_UNROLL = 8

_NT = (((1,), (1,)), ((), ()))
_TN = (((0,), (0,)), ((), ()))


def _rms(x, g):
    ms = jnp.mean(x * x, axis=-1, keepdims=True)
    return x * lax.rsqrt(ms + EPS) * g


def _sigmoid(x):
    return 1.0 / (1.0 + jnp.exp(-x))


def _softplus(x):
    return jnp.maximum(x, 0.0) + jnp.log(1.0 + jnp.exp(-jnp.abs(x)))


def _split_bf16(x, parts):
    out = []
    r = x
    for _ in range(parts):
        h = r.astype(BF16)
        out.append(h)
        r = r - h.astype(F32)
    return out


def _dot(a, b):
    return jnp.dot(a, b, preferred_element_type=F32)


def _dot_exact(a01, x, parts):
    acc = None
    for piece in _split_bf16(x, parts):
        t = _dot(a01, piece)
        acc = t if acc is None else acc + t
    return acc


def _to_token_tiles(ref, x, stride, offset=0):
    m = x.shape[0]
    for s in range(TOK_ROWS):
        ref[pl.ds(offset + s, m, stride=stride), :] = x[:, s * LANES:(s + 1) * LANES]


def _from_token_tiles(ref, m, stride, offset=0):
    return jnp.concatenate([ref[pl.ds(offset + s, m, stride=stride), :] for s in range(TOK_ROWS)], axis=1)


def _inproj_kernel(x_ref, *refs, steps_per_seq):
    param_refs, out_refs, fcar_ref = refs[:11], refs[11:18], refs[18]
    z_ref, xbc_ref, q_ref, k_ref, vt_ref, sm_ref, smt_ref = out_refs

    @pl.when(pl.program_id(0) % steps_per_seq == 0)
    def _():
        fcar_ref[...] = jnp.zeros(fcar_ref.shape, F32)

    for c in range(INPROJ_TILES_PER_STEP):
        rows = pl.ds(c * TM_PROJ, TM_PROJ)
        _inproj_tile(x_ref.at[rows], *param_refs, z_ref.at[rows], xbc_ref.at[rows], q_ref.at[rows], k_ref.at[rows],
                     vt_ref.at[:, rows], sm_ref.at[rows], smt_ref.at[:, rows], fcar_ref)


def _inproj_tile(x_ref, g_ref, wz_ref, wxbc_ref, wq_ref, wk_ref, wvt_ref, wsm_ref, wsmt_ref,
                 fbias_ref, qones_ref, place_ref,
                 z_ref, xbc_ref, q_ref, k_ref, vt_ref, sm_ref, smt_ref, fcar_ref):
    TM = x_ref.shape[0]
    hn = _rms(x_ref[...], g_ref[...]).astype(BF16)
    z_ref[...] = _dot(hn, wz_ref[...]).astype(BF16)
    xbc_ref[...] = _dot(hn, wxbc_ref[...]).astype(BF16)
    vt = lax.dot_general(wvt_ref[...], hn, _NT, preferred_element_type=F32).astype(BF16)
    ones_rows = jnp.ones((VT_ROWS - FOX_HEAD_DIM, TM), BF16)
    for h in range(FOX_HEADS):
        vt_ref[h * VT_ROWS:h * VT_ROWS + FOX_HEAD_DIM, :] = vt[h * FOX_HEAD_DIM:(h + 1) * FOX_HEAD_DIM, :]
        vt_ref[h * VT_ROWS + FOX_HEAD_DIM:(h + 1) * VT_ROWS, :] = ones_rows
    sm_ref[...] = _dot(hn, wsm_ref[...])
    smt = lax.dot_general(wsmt_ref[...], hn, _NT, preferred_element_type=F32)
    smt_ref[...] = smt[0:SSD_HEADS]

    logf = -_softplus(-(smt[SSD_HEADS:] + fbias_ref[:, 0:1]))
    ri = lax.broadcasted_iota(I32, (TM, TM), 0)
    ci = lax.broadcasted_iota(I32, (TM, TM), 1)
    triu = (ri <= ci).astype(BF16)
    fcum = fcar_ref[:, 0:1]
    for piece in _split_bf16(logf, 3):
        fcum = fcum + _dot(piece, triu)
    fcar_ref[...] = jnp.broadcast_to(fcum[:, TM - 1:TM], fcar_ref.shape)
    pieces = _split_bf16(fcum * (-LOG2E), F_PARTS)
    stacked = jnp.concatenate([p.astype(F32) for p in pieces] + [jnp.zeros((SUBLANES, TM), F32)],
                              axis=0).astype(BF16)
    kbias = lax.dot_general(stacked, place_ref[...], _TN, preferred_element_type=F32)
    lane = lax.broadcasted_iota(I32, (TM, LANES), 1)
    qc = _dot(hn, wq_ref[...]) * (FOX_HEAD_DIM ** -0.5 * LOG2E)
    kc = _dot(hn, wk_ref[...])
    for h in range(FOX_HEADS):
        pair = slice((h // 2) * LANES, (h // 2 + 1) * LANES)
        tile = slice(h * LANES, (h + 1) * LANES)
        own = (lane < FOX_HEAD_DIM) if h % 2 == 0 else (lane >= FOX_HEAD_DIM)
        q_ref[:, tile] = jnp.where(own, qc[:, pair], qones_ref[:, tile]).astype(BF16)
        k_ref[:, tile] = jnp.where(own, kc[:, pair], kbias[:, tile]).astype(BF16)


def _in_proj(x2, g_mix, wz, wxbc, wq, wk, wvt, wsm, wsmt, fbias, qones, place, S):
    T = x2.shape[0]
    tm = TM_PROJ * INPROJ_TILES_PER_STEP
    row = lambda i: (i, 0)
    const = lambda i: (0, 0)
    full = lambda a: pl.BlockSpec(a.shape, const)
    return pl.pallas_call(
        functools.partial(_inproj_kernel, steps_per_seq=S // tm),
        grid=(T // tm,),
        in_specs=[pl.BlockSpec((tm, D_MODEL), row), full(g_mix), full(wz), full(wxbc), full(wq), full(wk),
                  full(wvt), full(wsm), full(wsmt), full(fbias), full(qones), full(place)],
        out_specs=[pl.BlockSpec((tm, D_SSD), row),
                   pl.BlockSpec((tm, D_XBC), row),
                   pl.BlockSpec((tm, D_QK), row),
                   pl.BlockSpec((tm, D_QK), row),
                   pl.BlockSpec((FOX_HEADS * VT_ROWS, tm), lambda i: (0, i)),
                   pl.BlockSpec((tm, LANES), row),
                   pl.BlockSpec((SSD_HEADS, tm), lambda i: (0, i))],
        out_shape=[jax.ShapeDtypeStruct((T, D_SSD), BF16),
                   jax.ShapeDtypeStruct((T, D_XBC), BF16),
                   jax.ShapeDtypeStruct((T, D_QK), BF16),
                   jax.ShapeDtypeStruct((T, D_QK), BF16),
                   jax.ShapeDtypeStruct((FOX_HEADS * VT_ROWS, T), BF16),
                   jax.ShapeDtypeStruct((T, LANES), F32),
                   jax.ShapeDtypeStruct((SSD_HEADS, T), F32)],
        scratch_shapes=[pltpu.VMEM((SUBLANES, LANES), F32)],
        compiler_params=pltpu.CompilerParams(dimension_semantics=("arbitrary",),
                                             vmem_limit_bytes=VMEM_LIMIT),
        name="in_proj",
    )(x2, g_mix, wz, wxbc, wq, wk, wvt, wsm, wsmt, fbias, qones, place)


def _ssd_kernel(xbc_ref, z_ref, sm_ref, smt_ref, *rest):
    *param_refs, y_ref, tail_ref, st_ref = rest

    @pl.when(pl.program_id(1) == 0)
    def _():
        tail_ref[...] = jnp.zeros(tail_ref.shape, BF16)
        st_ref[...] = jnp.zeros(st_ref.shape, F32)

    for c in range(SSD_CHUNKS_PER_STEP):
        rows = pl.ds(c * L_SSD, L_SSD)
        _ssd_chunk(xbc_ref.at[rows], z_ref.at[rows], sm_ref.at[rows], smt_ref.at[:, rows], *param_refs,
                   y_ref.at[rows], tail_ref, st_ref)


def _ssd_chunk(xbc_ref, z_ref, sm_ref, smt_ref, convw_ref, convb_ref, colb_ref, cola_ref,
               rowb_ref, rowa_ref, dexp_ref, gssd_ref, expand_ref,
               y_ref, tail_ref, st_ref):
    L = xbc_ref.shape[0]
    R = SSD_HEADS // SSD_GROUPS
    TAIL = tail_ref.shape[0]

    ri = lax.broadcasted_iota(I32, (L, L), 0)
    ci = lax.broadcasted_iota(I32, (L, L), 1)
    causal = ri >= ci
    tril = causal.astype(BF16)
    triu = (ri <= ci).astype(BF16)

    zf = z_ref[...].astype(F32)
    zgate = zf * _sigmoid(zf)

    x_bf = xbc_ref[...]
    tail = tail_ref[...]
    hr = lax.broadcasted_iota(I32, (SUBLANES, TAIL), 0)
    hc = lax.broadcasted_iota(I32, (SUBLANES, TAIL), 1)
    conv = convb_ref[...] + convw_ref[SSD_CONV - 1:SSD_CONV, :] * x_bf.astype(F32)
    head = None
    for d in range(1, SSD_CONV):
        w_d = convw_ref[SSD_CONV - 1 - d:SSD_CONV - d, :]
        conv = conv + w_d * _dot((ri - d == ci).astype(BF16), x_bf)
        t = w_d * _dot((hc == hr + (TAIL - d)).astype(BF16), tail)
        head = t if head is None else head + t
    conv = jnp.concatenate([conv[0:SUBLANES] + head, conv[SUBLANES:]], axis=0)
    tail_ref[...] = xbc_ref[L - TAIL:L, :]
    xc = conv * _sigmoid(conv)
    xs = xc[:, 0:D_SSD]
    xs_bf = xs.astype(BF16)
    bm = xc[:, D_SSD:D_SSD + SSD_GROUPS * SSD_STATE].astype(BF16)
    cm = xc[:, D_SSD + SSD_GROUPS * SSD_STATE:].astype(BF16)

    dtr = _softplus(smt_ref[...] + rowb_ref[:, 0:1])
    adt_r = dtr * (-LOG2E * jnp.exp(rowa_ref[:, 0:1]))
    csr = None
    for part in _split_bf16(adt_r, 3):
        t = _dot(part, triu)
        csr = t if csr is None else csr + t

    dtc = _softplus(sm_ref[...] + colb_ref[...])
    adt_c = dtc * (-LOG2E * jnp.exp(cola_ref[...]))
    cs_c = _dot_exact(tril, adt_c, 3)
    cs_last = cs_c[L - 1:L, :]
    e1 = jnp.exp2(cs_c)
    wst = dtc * jnp.exp2(cs_last - cs_c)
    ex = expand_ref[...]
    p1 = _split_bf16(e1, 2)
    e1x = _dot(p1[0], ex) + _dot(p1[1], ex)
    p2 = _split_bf16(wst, 2)
    wstx = _dot(p2[0], ex) + _dot(p2[1], ex)

    xw = (xs * wstx).astype(BF16)
    dec8 = jnp.broadcast_to(jnp.exp2(csr[:, L - 1:L]), (SSD_HEADS, SSD_STATE))

    ydiag = []
    yoff = []
    for g in range(SSD_GROUPS):
        bg = bm[:, g * SSD_STATE:(g + 1) * SSD_STATE]
        cg = cm[:, g * SSD_STATE:(g + 1) * SSD_STATE]
        gmat = lax.dot_general(cg, bg, _NT, preferred_element_type=F32)
        s_old = st_ref[g]
        yoff.append(lax.dot_general(cg, s_old.astype(BF16), _NT, preferred_element_type=F32))
        for j in range(R):
            h = g * R + j
            seg = cs_c[:, h:h + 1] - csr[h:h + 1, :]
            lm = jnp.exp2(jnp.where(causal, seg, -jnp.inf))
            m = (gmat * lm * dtr[h:h + 1, :]).astype(BF16)
            ydiag.append(_dot(m, xs_bf[:, h * SSD_HEAD_DIM:(h + 1) * SSD_HEAD_DIM]))
        upd = lax.dot_general(xw[:, g * R * SSD_HEAD_DIM:(g + 1) * R * SSD_HEAD_DIM], bg, _TN,
                              preferred_element_type=F32)
        dec = jnp.concatenate(
            [jnp.broadcast_to(dec8[g * R + j:g * R + j + 1, :], (SSD_HEAD_DIM, SSD_STATE)) for j in range(R)],
            axis=0)
        st_ref[g] = dec * s_old + upd

    y = jnp.concatenate(ydiag, axis=1) + e1x * jnp.concatenate(yoff, axis=1) + dexp_ref[...] * xs
    y_ref[...] = _rms(y * zgate, gssd_ref[...]).astype(BF16)


def _ssd(xbc, z, sm, smt, convw, convb, colb, cola, rowb, rowa, dexp, gssd, expand, B, S):
    L = L_SSD * SSD_CHUNKS_PER_STEP
    nc = S // L
    T = B * S
    row = lambda b, c: (b * nc + c, 0)
    const = lambda b, c: (0, 0)
    full = lambda a: pl.BlockSpec(a.shape, const)
    return pl.pallas_call(
        _ssd_kernel,
        grid=(B, nc),
        in_specs=[pl.BlockSpec((L, D_XBC), row),
                  pl.BlockSpec((L, D_SSD), row),
                  pl.BlockSpec((L, LANES), row),
                  pl.BlockSpec((SSD_HEADS, L), lambda b, c: (0, b * nc + c)),
                  full(convw), full(convb), full(colb), full(cola), full(rowb), full(rowa),
                  full(dexp), full(gssd), full(expand)],
        out_specs=pl.BlockSpec((L, D_SSD), row),
        out_shape=jax.ShapeDtypeStruct((T, D_SSD), BF16),
        scratch_shapes=[pltpu.VMEM((BF16_SUBLANES, D_XBC), BF16),
                        pltpu.VMEM((SSD_GROUPS, (SSD_HEADS // SSD_GROUPS) * SSD_HEAD_DIM, SSD_STATE), F32)],
        compiler_params=pltpu.CompilerParams(dimension_semantics=("arbitrary", "arbitrary"),
                                             vmem_limit_bytes=VMEM_LIMIT),
        name="ssd",
    )(xbc, z, sm, smt, convw, convb, colb, cola, rowb, rowa, dexp, gssd, expand)


def _fox_kernel(q_ref, k_ref, vt_ref, o_ref, sa_ref, sb_ref):
    TQ = q_ref.shape[0]
    TK = TK_FOX
    assert TQ == 2 * TK
    n_pairs = pl.program_id(1)
    n_full = 2 * n_pairs
    key_idx = lax.broadcasted_iota(I32, (TK, TQ), 0)
    qry_idx = lax.broadcasted_iota(I32, (TK, TQ), 1)
    heads = range(FOX_HEADS)
    qs = [q_ref[:, h * LANES:(h + 1) * LANES] for h in heads]

    def scores(j, s_ref, h):
        ks = pl.multiple_of(j * TK, TK)
        s = lax.dot_general(k_ref[pl.ds(ks, TK), h * LANES:(h + 1) * LANES], qs[h], _NT,
                            preferred_element_type=F32)
        s_ref[h] = s
        return jnp.max(s, axis=0, keepdims=True)

    def softmax_pv(j, s_ref, h, state, diag):
        ks = pl.multiple_of(j * TK, TK)
        m, acc, smax = state
        s = s_ref[h]
        if diag is not None:
            s = jnp.where(key_idx + diag * TK <= qry_idx, s, -jnp.inf)
            smax = jnp.max(s, axis=0, keepdims=True)
        mn = jnp.maximum(m, smax)
        p = jnp.exp2(s - mn).astype(BF16)
        acc = jnp.exp2(m - mn) * acc + _dot(vt_ref[h * VT_ROWS:(h + 1) * VT_ROWS, pl.ds(ks, TK)], p)
        return mn, acc

    def step(j, cur_ref, nxt_ref, carry, diag):
        new = []
        for h in heads:
            smax_nxt = scores(j + 1, nxt_ref, h) if nxt_ref is not None else carry[h][2]
            new.append(softmax_pv(j, cur_ref, h, carry[h], diag) + (smax_nxt,))
        return tuple(new)

    def pair(t, carry):
        carry = step(2 * t, sa_ref, sb_ref, carry, None)
        return step(2 * t + 1, sb_ref, sa_ref, carry, None)

    carry = tuple((jnp.full((1, TQ), -1e30, F32), jnp.zeros((VT_ROWS, TQ), F32), scores(0, sa_ref, h))
                  for h in heads)
    carry = lax.fori_loop(0, n_pairs, pair, carry)
    carry = step(n_full, sa_ref, sb_ref, carry, 0)
    carry = step(n_full + 1, sb_ref, None, carry, 1)
    o_ref[...] = jnp.concatenate(
        [acc[0:FOX_HEAD_DIM] / acc[FOX_HEAD_DIM:FOX_HEAD_DIM + 1] for _, acc, _ in carry], axis=0).astype(BF16)


def _fox(q, k, vt, B, S):
    TQ = TQ_FOX
    nq = S // TQ
    T = B * S
    return pl.pallas_call(
        _fox_kernel,
        grid=(B, nq),
        in_specs=[pl.BlockSpec((TQ, D_QK), lambda b, i: (b * nq + i, 0)),
                  pl.BlockSpec((S, D_QK), lambda b, i: (b, 0)),
                  pl.BlockSpec((FOX_HEADS * VT_ROWS, S), lambda b, i: (0, b))],
        out_specs=pl.BlockSpec((D_FOX, TQ), lambda b, i: (0, b * nq + i)),
        out_shape=jax.ShapeDtypeStruct((D_FOX, T), BF16),
        scratch_shapes=[pltpu.VMEM((FOX_HEADS, TK_FOX, TQ), F32), pltpu.VMEM((FOX_HEADS, TK_FOX, TQ), F32)],
        compiler_params=pltpu.CompilerParams(dimension_semantics=("arbitrary", "arbitrary"),
                                             vmem_limit_bytes=VMEM_LIMIT),
        name="fox",
    )(q, k, vt)


def _outproj_kernel(x_ref, ys_ref, yft_ref, wo1_ref, wo2_ref, g_ref, wr_ref, br_ref,
                    h1_ref, xrow_ref, meta_ref, cnt_ref, carry_ref):
    TM = x_ref.shape[0]
    i = pl.program_id(0)

    @pl.when(i == 0)
    def _():
        carry_ref[...] = jnp.zeros(carry_ref.shape, F32)

    h1 = (x_ref[...] + _dot(ys_ref[...], wo1_ref[...])
          + lax.dot_general(yft_ref[...], wo2_ref[...], _TN, preferred_element_type=F32))
    h1_ref[...] = h1
    hn = _rms(h1, g_ref[...])
    h_hi, h_lo = _split_bf16(hn, 2)
    lg = lax.dot_general(wr_ref[...], h_hi, _NT, preferred_element_type=F32)
    logits = (lg[0:ROUTE_ROWS] + lg[ROUTE_ROWS:]
              + lax.dot_general(wr_ref[0:ROUTE_ROWS, :], h_lo, _NT, preferred_element_type=F32)
              + br_ref[:, 0:1])
    row = lax.broadcasted_iota(I32, (ROUTE_ROWS, TM), 0).astype(F32)
    ninf = -jnp.inf
    far = float(ROUTE_ROWS)
    gl = jnp.where(row < N_GROUPS, logits, ninf)
    gmax = jnp.max(gl, axis=0, keepdims=True)
    p_grp = 1.0 / jnp.sum(jnp.exp(gl - gmax), axis=0, keepdims=True)
    g_idx = jnp.min(jnp.where(gl == gmax, row, far), axis=0, keepdims=True)
    e0 = N_GROUPS + EXPERTS_PER_GROUP * g_idx
    el = jnp.where((row >= e0) & (row < e0 + EXPERTS_PER_GROUP), logits, ninf)
    v1 = jnp.max(el, axis=0, keepdims=True)
    i1 = jnp.min(jnp.where(el == v1, row, far), axis=0, keepdims=True)
    el2 = jnp.where(row == i1, ninf, el)
    v2 = jnp.max(el2, axis=0, keepdims=True)
    i2 = jnp.min(jnp.where(el2 == v2, row, far), axis=0, keepdims=True)
    e2 = jnp.exp(v2 - v1)
    w1 = p_grp / (1.0 + e2)
    w2 = p_grp * e2 / (1.0 + e2)
    l1 = i1 - e0
    l2 = i2 - e0
    first = l1 < l2
    a = jnp.where(first, l1, l2)
    b = jnp.where(first, l2, l1)
    wa = jnp.where(first, w1, w2)
    wb = jnp.where(first, w2, w1)
    pair = a * (5.0 - a) * 0.5 + b - 1.0
    bucket = g_idx * N_PAIRS + pair

    onehot = (row == bucket).astype(BF16)
    ri = lax.broadcasted_iota(I32, (TM, TM), 0)
    ci = lax.broadcasted_iota(I32, (TM, TM), 1)
    incl = _dot(onehot, (ri <= ci).astype(BF16))
    oh = onehot.astype(F32)
    rank = jnp.sum((incl - oh + carry_ref[:, 0:1]) * oh, axis=0, keepdims=True)
    carry_ref[...] = carry_ref[...] + jnp.broadcast_to(incl[:, TM - 1:TM], carry_ref.shape)
    cnt_ref[...] = carry_ref[...]

    r8 = lax.broadcasted_iota(I32, (SUBLANES, TM), 0)
    meta_ref[...] = jnp.where(r8 == 0, bucket, jnp.where(r8 == 1, rank, 0.0))
    wrows = jnp.where(r8 == 0, wa, jnp.where(r8 == 1, wb, 0.0))
    eye = (lax.broadcasted_iota(I32, (SUBLANES, LANES), 0)
           == lax.broadcasted_iota(I32, (SUBLANES, LANES), 1)).astype(BF16)
    wcols = None
    for piece in _split_bf16(wrows, 3):
        t = lax.dot_general(piece, eye, _TN, preferred_element_type=F32)
        wcols = t if wcols is None else wcols + t
    _to_token_tiles(xrow_ref, hn, TOK_PITCH)
    xrow_ref[pl.ds(TOK_ROWS, TM, stride=TOK_PITCH), :] = wcols
    for s in range(TOK_ROWS + 1, TOK_PITCH):
        xrow_ref[pl.ds(s, TM, stride=TOK_PITCH), :] = jnp.zeros((TM, LANES), F32)


def _out_proj(x2, yssd, yfoxt, wo1, wo2, g_ffn, wr, br):
    T = x2.shape[0]
    tm = TM_PROJ
    row = lambda i: (i, 0)
    const = lambda i: (0, 0)
    return pl.pallas_call(
        _outproj_kernel,
        grid=(T // tm,),
        in_specs=[pl.BlockSpec((tm, D_MODEL), row),
                  pl.BlockSpec((tm, D_SSD), row),
                  pl.BlockSpec((D_FOX, tm), lambda i: (0, i)),
                  pl.BlockSpec(wo1.shape, const),
                  pl.BlockSpec(wo2.shape, const),
                  pl.BlockSpec((1, D_MODEL), const),
                  pl.BlockSpec(wr.shape, const),
                  pl.BlockSpec(br.shape, const)],
        out_specs=[pl.BlockSpec((tm, D_MODEL), row),
                   pl.BlockSpec((tm * TOK_PITCH, LANES), row),
                   pl.BlockSpec((SUBLANES, tm), lambda i: (0, i)),
                   pl.BlockSpec((ROUTE_ROWS, LANES), const)],
        out_shape=[jax.ShapeDtypeStruct((T, D_MODEL), F32),
                   jax.ShapeDtypeStruct((T * TOK_PITCH, LANES), F32),
                   jax.ShapeDtypeStruct((SUBLANES, T), F32),
                   jax.ShapeDtypeStruct((ROUTE_ROWS, LANES), F32)],
        scratch_shapes=[pltpu.VMEM((ROUTE_ROWS, LANES), F32)],
        compiler_params=pltpu.CompilerParams(dimension_semantics=("arbitrary",),
                                             vmem_limit_bytes=VMEM_LIMIT),
        name="out_proj_router",
    )(x2, yssd, yfoxt, wo1, wo2, g_ffn, wr, br)


def _dispatch_kernel(ztile_ref, pos_ref, src_ref, dst_hbm, zero_ref, sem, zsem):
    TD = pos_ref.shape[-1]
    tile_rows = TM_MOE * TOK_PITCH

    @pl.when(pl.program_id(0) == 0)
    def _():
        zero_ref[...] = jnp.zeros(zero_ref.shape, F32)

        def zcopy(e):
            dst = pl.multiple_of(ztile_ref[e] * tile_rows, tile_rows)
            return pltpu.make_async_copy(zero_ref, dst_hbm.at[pl.ds(dst, tile_rows)], zsem)

        def zstart(e, c):
            @pl.when(ztile_ref[e] >= 0)
            def _():
                zcopy(e).start()
            return c

        def zwait(e, c):
            @pl.when(ztile_ref[e] >= 0)
            def _():
                zcopy(e).wait()
            return c

        lax.fori_loop(0, ztile_ref.shape[0], zstart, 0)
        lax.fori_loop(0, ztile_ref.shape[0], zwait, 0)

    def copy(r):
        dst = pl.multiple_of(pos_ref[0, r] * TOK_PITCH, PITCH_ALIGN)
        src = pl.multiple_of(r * TOK_PITCH, PITCH_ALIGN)
        return pltpu.make_async_copy(src_ref.at[pl.ds(src, TOK_PITCH)], dst_hbm.at[pl.ds(dst, TOK_PITCH)], sem)

    def start(rb, c):
        for u in range(DMA_ISSUE_UNROLL):
            copy(DMA_ISSUE_UNROLL * rb + u).start(priority=u % 2)
        return c

    lax.fori_loop(0, TD // DMA_ISSUE_UNROLL, start, 0)

    def wait(r, c):
        copy(r).wait()
        return c

    lax.fori_loop(0, TD, wait, 0, unroll=DMA_WAIT_UNROLL)


def _dispatch(ztiles, pos, xrow, n_rows):
    T = pos.shape[0]
    td = TD_DISP
    pos3 = pos.reshape(T // td, 1, td)
    grid_spec = pltpu.PrefetchScalarGridSpec(
        num_scalar_prefetch=1,
        grid=(T // td,),
        in_specs=[pl.BlockSpec((None, 1, td), lambda i, zt: (i, 0, 0), memory_space=pltpu.SMEM),
                  pl.BlockSpec((td * TOK_PITCH, LANES), lambda i, zt: (i, 0))],
        out_specs=pl.BlockSpec(memory_space=pl.ANY),
        scratch_shapes=[pltpu.VMEM((TM_MOE * TOK_PITCH, LANES), F32),
                        pltpu.SemaphoreType.DMA(()), pltpu.SemaphoreType.DMA(())],
    )
    return pl.pallas_call(
        _dispatch_kernel,
        grid_spec=grid_spec,
        out_shape=jax.ShapeDtypeStruct((n_rows * TOK_PITCH, LANES), F32),
        compiler_params=pltpu.CompilerParams(dimension_semantics=("arbitrary",),
                                             vmem_limit_bytes=VMEM_LIMIT),
        name="dispatch",
    )(ztiles, pos3, xrow)


def _experts_kernel(ea_ref, eb_ref, blk_ref, valid_ref, xs_ref, *refs):
    del ea_ref, eb_ref, blk_ref
    w_refs, ys_ref = refs[:-1], refs[-1]
    n = EXPERT_TILES_PER_STEP
    TM = TM_MOE
    first = pl.program_id(0) * n

    def xs_tile(t):
        return xs_ref.at[pl.ds(t * TM * TOK_PITCH, TM * TOK_PITCH)]

    def ys_tile(t):
        return ys_ref.at[pl.ds(t * TM * TOK_ROWS, TM * TOK_ROWS)]

    def compute(t):
        wga_ref, wua_ref, wda_ref, wgb_ref, wub_ref, wdb_ref = w_refs[6 * t:6 * t + 6]
        x = _from_token_tiles(xs_tile(t), TM, TOK_PITCH).astype(BF16)
        w = xs_tile(t)[pl.ds(TOK_ROWS, TM, stride=TOK_PITCH), :]

        def expert(wg_ref, wu_ref, wd_ref, wt):
            g = _dot(x, wg_ref[...])
            u = _dot(x, wu_ref[...])
            return _dot((g * _sigmoid(g) * u * wt).astype(BF16), wd_ref[...])

        y = expert(wga_ref, wua_ref, wda_ref, w[:, 0:1]) + expert(wgb_ref, wub_ref, wdb_ref, w[:, 1:2])
        _to_token_tiles(ys_tile(t), y, TOK_ROWS)

    def clear(t):
        ys_tile(t)[...] = jnp.zeros((TM * TOK_ROWS, LANES), F32)

    in_use = valid_ref[first]
    for t in range(1, n):
        in_use = in_use + valid_ref[first + t]
    for k in range(n + 1):
        @pl.when(in_use == k)
        def _(k=k):
            for t in range(n):
                compute(t) if t < k else clear(t)


def _experts(ea, eb, blk, valid, xs, wg, wu, wd):
    tm = TM_MOE
    n = EXPERT_TILES_PER_STEP
    n_rows = xs.shape[0] // TOK_PITCH
    nt = n_rows // tm
    assert nt % n == 0
    xmap = lambda i, ea, eb, blk, valid: (blk[i * n] // n, 0)
    up_spec = lambda m: pl.BlockSpec((None, D_MODEL, D_EXPERT), m)
    dn_spec = lambda m: pl.BlockSpec((None, D_EXPERT, D_MODEL), m)
    w_specs, w_args = [], []
    for t in range(n):
        amap = lambda i, ea, eb, blk, valid, t=t: (ea[i * n + t], 0, 0)
        bmap = lambda i, ea, eb, blk, valid, t=t: (eb[i * n + t], 0, 0)
        w_specs += [up_spec(amap), up_spec(amap), dn_spec(amap), up_spec(bmap), up_spec(bmap), dn_spec(bmap)]
        w_args += [wg, wu, wd, wg, wu, wd]
    grid_spec = pltpu.PrefetchScalarGridSpec(
        num_scalar_prefetch=4,
        grid=(nt // n,),
        in_specs=[pl.BlockSpec((n * tm * TOK_PITCH, LANES), xmap)] + w_specs,
        out_specs=pl.BlockSpec((n * tm * TOK_ROWS, LANES), lambda i, ea, eb, blk, valid: (i, 0)),
    )
    return pl.pallas_call(
        _experts_kernel,
        grid_spec=grid_spec,
        out_shape=jax.ShapeDtypeStruct((n_rows * TOK_ROWS, LANES), F32),
        compiler_params=pltpu.CompilerParams(dimension_semantics=("arbitrary",),
                                             vmem_limit_bytes=VMEM_LIMIT),
        name="experts",
    )(ea, eb, blk, valid, xs, *w_args)


def _combine_kernel(pos_cur_ref, pos_nxt_ref, h1_ref, p_ref, wpp_ref, wpg_ref, gple_ref, gfin_ref,
                    ys_hbm, o_ref, buf_ref, sem):
    TM = h1_ref.shape[0]
    i = pl.program_id(0)
    n = pl.num_programs(0)
    slot = i % 2

    def copy(pos_ref, s, r):
        src = pl.multiple_of(pos_ref[0, r] * TOK_ROWS, TOK_ROWS)
        dst = pl.multiple_of(r * TOK_PITCH, PITCH_ALIGN)
        return pltpu.make_async_copy(ys_hbm.at[pl.ds(src, TOK_ROWS)], buf_ref.at[s, pl.ds(dst, TOK_ROWS)],
                                     sem.at[s])

    def gather(pos_ref, s):
        def start(rb, c):
            for u in range(DMA_ISSUE_UNROLL):
                copy(pos_ref, s, DMA_ISSUE_UNROLL * rb + u).start(priority=u % 2)
            return c
        lax.fori_loop(0, TM // DMA_ISSUE_UNROLL, start, 0)

    @pl.when(i == 0)
    def _():
        gather(pos_cur_ref, 0)

    @pl.when(i + 1 < n)
    def _():
        gather(pos_nxt_ref, 1 - slot)

    def wait(r, c):
        copy(pos_cur_ref, slot, r).wait()
        return c

    lax.fori_loop(0, TM, wait, 0, unroll=DMA_WAIT_UNROLL)

    h2 = h1_ref[...] + _from_token_tiles(buf_ref.at[slot], TM, TOK_PITCH)
    n3 = _rms(h2, gple_ref[...]).astype(BF16)
    gate = _sigmoid(_dot(n3, wpg_ref[...]))
    pp = _dot(p_ref[...].astype(BF16), wpp_ref[...])
    o_ref[...] = _rms(h2 + pp * gate, gfin_ref[...])


def _combine(pos, h1, p2, wpp, wpg, g_ple, g_final, ys):
    T = h1.shape[0]
    tm = TM_COMB
    nt = T // tm
    pos3 = pos.reshape(nt, 1, tm)
    row = lambda i: (i, 0)
    const = lambda i: (0, 0)
    return pl.pallas_call(
        _combine_kernel,
        grid=(nt,),
        in_specs=[pl.BlockSpec((None, 1, tm), lambda i: (i, 0, 0), memory_space=pltpu.SMEM),
                  pl.BlockSpec((None, 1, tm), lambda i: (jnp.minimum(i + 1, nt - 1), 0, 0),
                               memory_space=pltpu.SMEM),
                  pl.BlockSpec((tm, D_MODEL), row),
                  pl.BlockSpec((tm, D_PLE), row),
                  pl.BlockSpec(wpp.shape, const),
                  pl.BlockSpec(wpg.shape, const),
                  pl.BlockSpec((1, D_MODEL), const),
                  pl.BlockSpec((1, D_MODEL), const),
                  pl.BlockSpec(memory_space=pl.ANY)],
        out_specs=pl.BlockSpec((tm, D_MODEL), row),
        out_shape=jax.ShapeDtypeStruct((T, D_MODEL), F32),
        scratch_shapes=[pltpu.VMEM((2, tm * TOK_PITCH, LANES), F32),
                        pltpu.SemaphoreType.DMA((2,))],
        compiler_params=pltpu.CompilerParams(dimension_semantics=("arbitrary",),
                                             vmem_limit_bytes=VMEM_LIMIT),
        name="combine_ple",
    )(pos3, pos3, h1, p2, wpp, wpg, g_ple, g_final, ys)


def _tile_tables(counts, n_tiles):
    nt = (counts + TM_MOE - 1) // TM_MOE
    cum = jnp.cumsum(nt)
    off = (cum - nt) * TM_MOE
    total = cum[-1]
    tile = jnp.arange(n_tiles, dtype=I32)
    valid = (tile < total).astype(I32)
    blk = jnp.maximum(jnp.minimum(tile, total - 1), 0)
    tb = jnp.minimum(jnp.sum((cum[None, :] <= blk[:, None]).astype(I32), axis=1), N_BUCKETS - 1)
    pair_a = jnp.array([0, 0, 0, 1, 1, 2], I32)
    pair_b = jnp.array([1, 2, 3, 2, 3, 3], I32)
    grp = tb // N_PAIRS
    ea = grp * EXPERTS_PER_GROUP + pair_a[tb % N_PAIRS]
    eb = grp * EXPERTS_PER_GROUP + pair_b[tb % N_PAIRS]
    last = jnp.where(nt > 0, cum - 1, -1)
    tail = total + jnp.arange(N_BUCKETS, dtype=I32)
    ztiles = jnp.concatenate([last, jnp.where(tail < n_tiles, tail, -1)]).astype(I32)
    return off, ea, eb, blk, valid, ztiles


def kernel(x, p, w_in, conv_w, conv_b, dt_bias, a_log, d_skip, g_ssd, fox_fbias, w_out, g_mix, g_ffn,
           w_route_group, b_route_group, w_route_expert, b_route_expert, w_exp_gate, w_exp_up, w_exp_down,
           g_ple, w_ple_proj, w_ple_gate, g_final):
    B, S, _ = x.shape
    T = B * S
    assert S % (L_SSD * SSD_CHUNKS_PER_STEP) == 0 and S % TQ_FOX == 0
    assert S % (TM_PROJ * INPROJ_TILES_PER_STEP) == 0 and T % TM_PROJ == 0
    assert T % TD_DISP == 0 and T % TM_COMB == 0
    x2 = x.reshape(T, D_MODEL)
    p2 = p[0].reshape(T, D_PLE)

    wi = w_in[0]
    o_xbc = D_SSD
    o_dt = o_xbc + D_XBC
    o_q = o_dt + SSD_HEADS
    o_k = o_q + D_FOX
    o_v = o_k + D_FOX
    o_f = o_v + D_FOX
    wz = wi[:, 0:o_xbc].astype(BF16)
    wxbc = wi[:, o_xbc:o_dt].astype(BF16)
    wq = wi[:, o_q:o_k].astype(BF16)
    wk = wi[:, o_k:o_v].astype(BF16)
    wvt = wi[:, o_v:o_f].T.astype(BF16)
    w_dt = wi[:, o_dt:o_q]
    w_f = wi[:, o_f:o_f + FOX_HEADS]
    wsm = jnp.pad(w_dt, ((0, 0), (0, LANES - SSD_HEADS))).astype(BF16)
    wsmt = jnp.concatenate([w_dt, w_f], axis=1).T.astype(BF16)
    lane_pad = lambda v: jnp.pad(v.reshape(1, -1), ((0, 0), (0, LANES - v.shape[-1])))
    colb = lane_pad(dt_bias[0])
    fbias = jnp.broadcast_to(fox_fbias[0][:, None], (FOX_HEADS, LANES))
    cola = lane_pad(a_log[0])
    rowb = jnp.broadcast_to(dt_bias[0][:, None], (SSD_HEADS, LANES))
    rowa = jnp.broadcast_to(a_log[0][:, None], (SSD_HEADS, LANES))
    dexp = jnp.repeat(d_skip[0], SSD_HEAD_DIM).reshape(1, D_SSD)
    expand = (jnp.arange(LANES)[:, None] == (jnp.arange(D_SSD) // SSD_HEAD_DIM)[None, :]).astype(BF16)
    col = jnp.arange(D_QK)
    head = col // LANES
    j_bias = col % LANES - jnp.where(head % 2 == 0, FOX_HEAD_DIM, 0)
    bias_col = (j_bias >= 0) & (j_bias < F_PARTS)
    qones = bias_col.astype(F32).reshape(1, D_QK)
    src_row = j_bias * FOX_HEADS + head
    place = (bias_col[None, :] & (jnp.arange((F_PARTS + 1) * FOX_HEADS)[:, None] == src_row[None, :])).astype(BF16)
    wo = w_out[0].astype(BF16)
    wr = jnp.pad(jnp.concatenate([w_route_group[0], w_route_expert[0]], axis=1).T,
                 ((0, ROUTE_ROWS - N_GROUPS - N_EXPERTS), (0, 0)))
    wr_hi = wr.astype(BF16)
    wr = jnp.concatenate([wr_hi, (wr - wr_hi.astype(F32)).astype(BF16)], axis=0)
    br = jnp.broadcast_to(jnp.pad(jnp.concatenate([b_route_group[0], b_route_expert[0]]),
                                  (0, ROUTE_ROWS - N_GROUPS - N_EXPERTS))[:, None], (ROUTE_ROWS, LANES))
    wg = w_exp_gate[0].reshape(N_EXPERTS, D_MODEL, D_EXPERT).astype(BF16)
    wu = w_exp_up[0].reshape(N_EXPERTS, D_MODEL, D_EXPERT).astype(BF16)
    wd = w_exp_down[0].reshape(N_EXPERTS, D_EXPERT, D_MODEL).astype(BF16)

    z, xbc, q, k, vt, sm, smt = _in_proj(x2, g_mix[0].reshape(1, -1), wz, wxbc, wq, wk, wvt, wsm, wsmt,
                                         fbias, qones, place, S)
    yssd = _ssd(xbc, z, sm, smt, conv_w[0].reshape(SSD_CONV, D_XBC), conv_b[0].reshape(1, -1),
                colb, cola, rowb, rowa, dexp, g_ssd[0].reshape(1, -1), expand, B, S)
    yfoxt = _fox(q, k, vt, B, S)
    h1, xrow, meta, cnt = _out_proj(x2, yssd, yfoxt, wo[0:D_SSD], wo[D_SSD:], g_ffn[0].reshape(1, -1), wr, br)

    n_rows = T + N_BUCKETS * TM_MOE
    bucket = meta[0].astype(I32)
    rank = meta[1].astype(I32)
    off, ea, eb, blk, valid, ztiles = _tile_tables(cnt[0:N_BUCKETS, 0].astype(I32), n_rows // TM_MOE)
    pos = rank
    for b in range(N_BUCKETS):
        pos = pos + jnp.where(bucket == b, off[b], 0)

    xs = _dispatch(ztiles, pos, xrow, n_rows)
    ys = _experts(ea, eb, blk, valid, xs, wg, wu, wd)
    out = _combine(pos, h1, p2, w_ple_proj[0].astype(BF16), w_ple_gate[0].astype(BF16),
                   g_ple[0].reshape(1, -1), g_final.reshape(1, -1), ys)
    return out.reshape(B, S, D_MODEL)
```

```python
import functools

import jax
import jax.numpy as jnp
from jax import lax
from jax.experimental import pallas as pl
from jax.experimental.pallas import tpu as pltpu

F32 = jnp.float32
BF16 = jnp.bfloat16
I32 = jnp.int32

D_MODEL = 1024
SSD_HEADS = 8
SSD_HEAD_DIM = 64
SSD_GROUPS = 2
SSD_STATE = 128
SSD_CONV = 4
D_SSD = SSD_HEADS * SSD_HEAD_DIM
D_XBC = D_SSD + 2 * SSD_GROUPS * SSD_STATE
FOX_HEADS = 8
FOX_HEAD_DIM = 64
D_FOX = FOX_HEADS * FOX_HEAD_DIM
N_GROUPS = 4
EXPERTS_PER_GROUP = 4
N_EXPERTS = N_GROUPS * EXPERTS_PER_GROUP
N_PAIRS = 6
N_BUCKETS = N_GROUPS * N_PAIRS
ROUTE_ROWS = 32
D_EXPERT = 512
D_PLE = 256
EPS = 1e-6
LOG2E = 1.4426950408889634
LANES = 128
SUBLANES = 8
BF16_SUBLANES = 16
TOK_ROWS = D_MODEL // LANES
TOK_PITCH = 12
PITCH_ALIGN = 4
D_QK = FOX_HEADS * LANES
F_PARTS = 3
VT_ROWS = FOX_HEAD_DIM + BF16_SUBLANES
VMEM_LIMIT = 52 * 1024 * 1024

TM_PROJ = 512
INPROJ_TILES_PER_STEP = 2
L_SSD = 256
SSD_CHUNKS_PER_STEP = 4
TQ_FOX = 512
TK_FOX = 256
TM_MOE = 256
EXPERT_TILES_PER_STEP = 2
TD_DISP = 2048
TM_COMB = 512
DMA_WAIT_UNROLL = 16
DMA_ISSUE_UNROLL = 16

_NT = (((1,), (1,)), ((), ()))
_TN = (((0,), (0,)), ((), ()))


def _rms(x, g):
    ms = jnp.mean(x * x, axis=-1, keepdims=True)
    return x * lax.rsqrt(ms + EPS) * g


def _sigmoid(x):
    return 1.0 / (1.0 + jnp.exp(-x))


def _softplus(x):
    return jnp.maximum(x, 0.0) + jnp.log(1.0 + jnp.exp(-jnp.abs(x)))


def _split_bf16(x, parts):
    out = []
    r = x
    for _ in range(parts):
        h = r.astype(BF16)
        out.append(h)
        r = r - h.astype(F32)
    return out


def _dot(a, b):
    return jnp.dot(a, b, preferred_element_type=F32)


def _dot_exact(a01, x, parts):
    acc = None
    for piece in _split_bf16(x, parts):
        t = _dot(a01, piece)
        acc = t if acc is None else acc + t
    return acc


def _to_token_tiles(ref, x, stride, offset=0):
    m = x.shape[0]
    for s in range(TOK_ROWS):
        ref[pl.ds(offset + s, m, stride=stride), :] = x[:, s * LANES:(s + 1) * LANES]


def _from_token_tiles(ref, m, stride, offset=0):
    return jnp.concatenate([ref[pl.ds(offset + s, m, stride=stride), :] for s in range(TOK_ROWS)], axis=1)


def _inproj_kernel(x_ref, *refs, steps_per_seq):
    param_refs, out_refs, fcar_ref = refs[:11], refs[11:18], refs[18]
    z_ref, xbc_ref, q_ref, k_ref, vt_ref, sm_ref, smt_ref = out_refs

    @pl.when(pl.program_id(0) % steps_per_seq == 0)
    def _():
        fcar_ref[...] = jnp.zeros(fcar_ref.shape, F32)

    for c in range(INPROJ_TILES_PER_STEP):
        rows = pl.ds(c * TM_PROJ, TM_PROJ)
        _inproj_tile(x_ref.at[rows], *param_refs, z_ref.at[rows], xbc_ref.at[rows], q_ref.at[rows], k_ref.at[rows],
                     vt_ref.at[:, rows], sm_ref.at[rows], smt_ref.at[:, rows], fcar_ref)


def _inproj_tile(x_ref, g_ref, wz_ref, wxbc_ref, wq_ref, wk_ref, wvt_ref, wsm_ref, wsmt_ref,
                 fbias_ref, qones_ref, place_ref,
                 z_ref, xbc_ref, q_ref, k_ref, vt_ref, sm_ref, smt_ref, fcar_ref):
    TM = x_ref.shape[0]
    hn = _rms(x_ref[...], g_ref[...]).astype(BF16)
    z_ref[...] = _dot(hn, wz_ref[...]).astype(BF16)
    xbc_ref[...] = _dot(hn, wxbc_ref[...]).astype(BF16)
    vt = lax.dot_general(wvt_ref[...], hn, _NT, preferred_element_type=F32).astype(BF16)
    ones_rows = jnp.ones((VT_ROWS - FOX_HEAD_DIM, TM), BF16)
    for h in range(FOX_HEADS):
        vt_ref[h * VT_ROWS:h * VT_ROWS + FOX_HEAD_DIM, :] = vt[h * FOX_HEAD_DIM:(h + 1) * FOX_HEAD_DIM, :]
        vt_ref[h * VT_ROWS + FOX_HEAD_DIM:(h + 1) * VT_ROWS, :] = ones_rows
    sm_ref[...] = _dot(hn, wsm_ref[...])
    smt = lax.dot_general(wsmt_ref[...], hn, _NT, preferred_element_type=F32)
    smt_ref[...] = smt[0:SSD_HEADS]

    logf = -_softplus(-(smt[SSD_HEADS:] + fbias_ref[:, 0:1]))
    ri = lax.broadcasted_iota(I32, (TM, TM), 0)
    ci = lax.broadcasted_iota(I32, (TM, TM), 1)
    triu = (ri <= ci).astype(BF16)
    fcum = fcar_ref[:, 0:1]
    for piece in _split_bf16(logf, 3):
        fcum = fcum + _dot(piece, triu)
    fcar_ref[...] = jnp.broadcast_to(fcum[:, TM - 1:TM], fcar_ref.shape)
    pieces = _split_bf16(fcum * (-LOG2E), F_PARTS)
    stacked = jnp.concatenate([p.astype(F32) for p in pieces] + [jnp.zeros((SUBLANES, TM), F32)],
                              axis=0).astype(BF16)
    kbias = lax.dot_general(stacked, place_ref[...], _TN, preferred_element_type=F32)
    lane = lax.broadcasted_iota(I32, (TM, LANES), 1)
    qc = _dot(hn, wq_ref[...]) * (FOX_HEAD_DIM ** -0.5 * LOG2E)
    kc = _dot(hn, wk_ref[...])
    for h in range(FOX_HEADS):
        pair = slice((h // 2) * LANES, (h // 2 + 1) * LANES)
        tile = slice(h * LANES, (h + 1) * LANES)
        own = (lane < FOX_HEAD_DIM) if h % 2 == 0 else (lane >= FOX_HEAD_DIM)
        q_ref[:, tile] = jnp.where(own, qc[:, pair], qones_ref[:, tile]).astype(BF16)
        k_ref[:, tile] = jnp.where(own, kc[:, pair], kbias[:, tile]).astype(BF16)


def _in_proj(x2, g_mix, wz, wxbc, wq, wk, wvt, wsm, wsmt, fbias, qones, place, S):
    T = x2.shape[0]
    tm = TM_PROJ * INPROJ_TILES_PER_STEP
    row = lambda i: (i, 0)
    const = lambda i: (0, 0)
    full = lambda a: pl.BlockSpec(a.shape, const)
    return pl.pallas_call(
        functools.partial(_inproj_kernel, steps_per_seq=S // tm),
        grid=(T // tm,),
        in_specs=[pl.BlockSpec((tm, D_MODEL), row), full(g_mix), full(wz), full(wxbc), full(wq), full(wk),
                  full(wvt), full(wsm), full(wsmt), full(fbias), full(qones), full(place)],
        out_specs=[pl.BlockSpec((tm, D_SSD), row),
                   pl.BlockSpec((tm, D_XBC), row),
                   pl.BlockSpec((tm, D_QK), row),
                   pl.BlockSpec((tm, D_QK), row),
                   pl.BlockSpec((FOX_HEADS * VT_ROWS, tm), lambda i: (0, i)),
                   pl.BlockSpec((tm, LANES), row),
                   pl.BlockSpec((SSD_HEADS, tm), lambda i: (0, i))],
        out_shape=[jax.ShapeDtypeStruct((T, D_SSD), BF16),
                   jax.ShapeDtypeStruct((T, D_XBC), BF16),
                   jax.ShapeDtypeStruct((T, D_QK), BF16),
                   jax.ShapeDtypeStruct((T, D_QK), BF16),
                   jax.ShapeDtypeStruct((FOX_HEADS * VT_ROWS, T), BF16),
                   jax.ShapeDtypeStruct((T, LANES), F32),
                   jax.ShapeDtypeStruct((SSD_HEADS, T), F32)],
        scratch_shapes=[pltpu.VMEM((SUBLANES, LANES), F32)],
        compiler_params=pltpu.CompilerParams(dimension_semantics=("arbitrary",),
                                             vmem_limit_bytes=VMEM_LIMIT),
        name="in_proj",
    )(x2, g_mix, wz, wxbc, wq, wk, wvt, wsm, wsmt, fbias, qones, place)


def _ssd_kernel(xbc_ref, z_ref, sm_ref, smt_ref, *rest):
    *param_refs, y_ref, tail_ref, st_ref = rest

    @pl.when(pl.program_id(1) == 0)
    def _():
        tail_ref[...] = jnp.zeros(tail_ref.shape, BF16)
        st_ref[...] = jnp.zeros(st_ref.shape, F32)

    for c in range(SSD_CHUNKS_PER_STEP):
        rows = pl.ds(c * L_SSD, L_SSD)
        _ssd_chunk(xbc_ref.at[rows], z_ref.at[rows], sm_ref.at[rows], smt_ref.at[:, rows], *param_refs,
                   y_ref.at[rows], tail_ref, st_ref)


def _ssd_chunk(xbc_ref, z_ref, sm_ref, smt_ref, convw_ref, convb_ref, colb_ref, cola_ref,
               rowb_ref, rowa_ref, dexp_ref, gssd_ref, expand_ref,
               y_ref, tail_ref, st_ref):
    L = xbc_ref.shape[0]
    R = SSD_HEADS // SSD_GROUPS
    TAIL = tail_ref.shape[0]

    ri = lax.broadcasted_iota(I32, (L, L), 0)
    ci = lax.broadcasted_iota(I32, (L, L), 1)
    causal = ri >= ci
    tril = causal.astype(BF16)
    triu = (ri <= ci).astype(BF16)

    zf = z_ref[...].astype(F32)
    zgate = zf * _sigmoid(zf)

    x_bf = xbc_ref[...]
    tail = tail_ref[...]
    hr = lax.broadcasted_iota(I32, (SUBLANES, TAIL), 0)
    hc = lax.broadcasted_iota(I32, (SUBLANES, TAIL), 1)
    conv = convb_ref[...] + convw_ref[SSD_CONV - 1:SSD_CONV, :] * x_bf.astype(F32)
    head = None
    for d in range(1, SSD_CONV):
        w_d = convw_ref[SSD_CONV - 1 - d:SSD_CONV - d, :]
        conv = conv + w_d * _dot((ri - d == ci).astype(BF16), x_bf)
        t = w_d * _dot((hc == hr + (TAIL - d)).astype(BF16), tail)
        head = t if head is None else head + t
    conv = jnp.concatenate([conv[0:SUBLANES] + head, conv[SUBLANES:]], axis=0)
    tail_ref[...] = xbc_ref[L - TAIL:L, :]
    xc = conv * _sigmoid(conv)
    xs = xc[:, 0:D_SSD]
    xs_bf = xs.astype(BF16)
    bm = xc[:, D_SSD:D_SSD + SSD_GROUPS * SSD_STATE].astype(BF16)
    cm = xc[:, D_SSD + SSD_GROUPS * SSD_STATE:].astype(BF16)

    dtr = _softplus(smt_ref[...] + rowb_ref[:, 0:1])
    adt_r = dtr * (-LOG2E * jnp.exp(rowa_ref[:, 0:1]))
    csr = None
    for part in _split_bf16(adt_r, 3):
        t = _dot(part, triu)
        csr = t if csr is None else csr + t

    dtc = _softplus(sm_ref[...] + colb_ref[...])
    adt_c = dtc * (-LOG2E * jnp.exp(cola_ref[...]))
    cs_c = _dot_exact(tril, adt_c, 3)
    cs_last = cs_c[L - 1:L, :]
    e1 = jnp.exp2(cs_c)
    wst = dtc * jnp.exp2(cs_last - cs_c)
    ex = expand_ref[...]
    p1 = _split_bf16(e1, 2)
    e1x = _dot(p1[0], ex) + _dot(p1[1], ex)
    p2 = _split_bf16(wst, 2)
    wstx = _dot(p2[0], ex) + _dot(p2[1], ex)

    xw = (xs * wstx).astype(BF16)
    dec8 = jnp.broadcast_to(jnp.exp2(csr[:, L - 1:L]), (SSD_HEADS, SSD_STATE))

    ydiag = []
    yoff = []
    for g in range(SSD_GROUPS):
        bg = bm[:, g * SSD_STATE:(g + 1) * SSD_STATE]
        cg = cm[:, g * SSD_STATE:(g + 1) * SSD_STATE]
        gmat = lax.dot_general(cg, bg, _NT, preferred_element_type=F32)
        s_old = st_ref[g]
        yoff.append(lax.dot_general(cg, s_old.astype(BF16), _NT, preferred_element_type=F32))
        for j in range(R):
            h = g * R + j
            seg = cs_c[:, h:h + 1] - csr[h:h + 1, :]
            lm = jnp.exp2(jnp.where(causal, seg, -jnp.inf))
            m = (gmat * lm * dtr[h:h + 1, :]).astype(BF16)
            ydiag.append(_dot(m, xs_bf[:, h * SSD_HEAD_DIM:(h + 1) * SSD_HEAD_DIM]))
        upd = lax.dot_general(xw[:, g * R * SSD_HEAD_DIM:(g + 1) * R * SSD_HEAD_DIM], bg, _TN,
                              preferred_element_type=F32)
        dec = jnp.concatenate(
            [jnp.broadcast_to(dec8[g * R + j:g * R + j + 1, :], (SSD_HEAD_DIM, SSD_STATE)) for j in range(R)],
            axis=0)
        st_ref[g] = dec * s_old + upd

    y = jnp.concatenate(ydiag, axis=1) + e1x * jnp.concatenate(yoff, axis=1) + dexp_ref[...] * xs
    y_ref[...] = _rms(y * zgate, gssd_ref[...]).astype(BF16)


def _ssd(xbc, z, sm, smt, convw, convb, colb, cola, rowb, rowa, dexp, gssd, expand, B, S):
    L = L_SSD * SSD_CHUNKS_PER_STEP
    nc = S // L
    T = B * S
    row = lambda b, c: (b * nc + c, 0)
    const = lambda b, c: (0, 0)
    full = lambda a: pl.BlockSpec(a.shape, const)
    return pl.pallas_call(
        _ssd_kernel,
        grid=(B, nc),
        in_specs=[pl.BlockSpec((L, D_XBC), row),
                  pl.BlockSpec((L, D_SSD), row),
                  pl.BlockSpec((L, LANES), row),
                  pl.BlockSpec((SSD_HEADS, L), lambda b, c: (0, b * nc + c)),
                  full(convw), full(convb), full(colb), full(cola), full(rowb), full(rowa),
                  full(dexp), full(gssd), full(expand)],
        out_specs=pl.BlockSpec((L, D_SSD), row),
        out_shape=jax.ShapeDtypeStruct((T, D_SSD), BF16),
        scratch_shapes=[pltpu.VMEM((BF16_SUBLANES, D_XBC), BF16),
                        pltpu.VMEM((SSD_GROUPS, (SSD_HEADS // SSD_GROUPS) * SSD_HEAD_DIM, SSD_STATE), F32)],
        compiler_params=pltpu.CompilerParams(dimension_semantics=("arbitrary", "arbitrary"),
                                             vmem_limit_bytes=VMEM_LIMIT),
        name="ssd",
    )(xbc, z, sm, smt, convw, convb, colb, cola, rowb, rowa, dexp, gssd, expand)


def _fox_kernel(q_ref, k_ref, vt_ref, o_ref, sa_ref, sb_ref):
    TQ = q_ref.shape[0]
    TK = TK_FOX
    assert TQ == 2 * TK
    n_pairs = pl.program_id(1)
    n_full = 2 * n_pairs
    key_idx = lax.broadcasted_iota(I32, (TK, TQ), 0)
    qry_idx = lax.broadcasted_iota(I32, (TK, TQ), 1)
    heads = range(FOX_HEADS)
    qs = [q_ref[:, h * LANES:(h + 1) * LANES] for h in heads]

    def scores(j, s_ref, h):
        ks = pl.multiple_of(j * TK, TK)
        s = lax.dot_general(k_ref[pl.ds(ks, TK), h * LANES:(h + 1) * LANES], qs[h], _NT,
                            preferred_element_type=F32)
        s_ref[h] = s
        return jnp.max(s, axis=0, keepdims=True)

    def softmax_pv(j, s_ref, h, state, diag):
        ks = pl.multiple_of(j * TK, TK)
        m, acc, smax = state
        s = s_ref[h]
        if diag is not None:
            s = jnp.where(key_idx + diag * TK <= qry_idx, s, -jnp.inf)
            smax = jnp.max(s, axis=0, keepdims=True)
        mn = jnp.maximum(m, smax)
        p = jnp.exp2(s - mn).astype(BF16)
        acc = jnp.exp2(m - mn) * acc + _dot(vt_ref[h * VT_ROWS:(h + 1) * VT_ROWS, pl.ds(ks, TK)], p)
        return mn, acc

    def step(j, cur_ref, nxt_ref, carry, diag):
        new = []
        for h in heads:
            smax_nxt = scores(j + 1, nxt_ref, h) if nxt_ref is not None else carry[h][2]
            new.append(softmax_pv(j, cur_ref, h, carry[h], diag) + (smax_nxt,))
        return tuple(new)

    def pair(t, carry):
        carry = step(2 * t, sa_ref, sb_ref, carry, None)
        return step(2 * t + 1, sb_ref, sa_ref, carry, None)

    carry = tuple((jnp.full((1, TQ), -1e30, F32), jnp.zeros((VT_ROWS, TQ), F32), scores(0, sa_ref, h))
                  for h in heads)
    carry = lax.fori_loop(0, n_pairs, pair, carry)
    carry = step(n_full, sa_ref, sb_ref, carry, 0)
    carry = step(n_full + 1, sb_ref, None, carry, 1)
    o_ref[...] = jnp.concatenate(
        [acc[0:FOX_HEAD_DIM] / acc[FOX_HEAD_DIM:FOX_HEAD_DIM + 1] for _, acc, _ in carry], axis=0).astype(BF16)


def _fox(q, k, vt, B, S):
    TQ = TQ_FOX
    nq = S // TQ
    T = B * S
    return pl.pallas_call(
        _fox_kernel,
        grid=(B, nq),
        in_specs=[pl.BlockSpec((TQ, D_QK), lambda b, i: (b * nq + i, 0)),
                  pl.BlockSpec((S, D_QK), lambda b, i: (b, 0)),
                  pl.BlockSpec((FOX_HEADS * VT_ROWS, S), lambda b, i: (0, b))],
        out_specs=pl.BlockSpec((D_FOX, TQ), lambda b, i: (0, b * nq + i)),
        out_shape=jax.ShapeDtypeStruct((D_FOX, T), BF16),
        scratch_shapes=[pltpu.VMEM((FOX_HEADS, TK_FOX, TQ), F32), pltpu.VMEM((FOX_HEADS, TK_FOX, TQ), F32)],
        compiler_params=pltpu.CompilerParams(dimension_semantics=("arbitrary", "arbitrary"),
                                             vmem_limit_bytes=VMEM_LIMIT),
        name="fox",
    )(q, k, vt)


def _outproj_kernel(x_ref, ys_ref, yft_ref, wo1_ref, wo2_ref, g_ref, wr_ref, br_ref,
                    h1_ref, xrow_ref, meta_ref, cnt_ref, carry_ref):
    TM = x_ref.shape[0]
    i = pl.program_id(0)

    @pl.when(i == 0)
    def _():
        carry_ref[...] = jnp.zeros(carry_ref.shape, F32)

    h1 = (x_ref[...] + _dot(ys_ref[...], wo1_ref[...])
          + lax.dot_general(yft_ref[...], wo2_ref[...], _TN, preferred_element_type=F32))
    h1_ref[...] = h1
    hn = _rms(h1, g_ref[...])
    h_hi, h_lo = _split_bf16(hn, 2)
    lg = lax.dot_general(wr_ref[...], h_hi, _NT, preferred_element_type=F32)
    logits = (lg[0:ROUTE_ROWS] + lg[ROUTE_ROWS:]
              + lax.dot_general(wr_ref[0:ROUTE_ROWS, :], h_lo, _NT, preferred_element_type=F32)
              + br_ref[:, 0:1])
    row = lax.broadcasted_iota(I32, (ROUTE_ROWS, TM), 0).astype(F32)
    ninf = -jnp.inf
    far = float(ROUTE_ROWS)
    gl = jnp.where(row < N_GROUPS, logits, ninf)
    gmax = jnp.max(gl, axis=0, keepdims=True)
    p_grp = 1.0 / jnp.sum(jnp.exp(gl - gmax), axis=0, keepdims=True)
    g_idx = jnp.min(jnp.where(gl == gmax, row, far), axis=0, keepdims=True)
    e0 = N_GROUPS + EXPERTS_PER_GROUP * g_idx
    el = jnp.where((row >= e0) & (row < e0 + EXPERTS_PER_GROUP), logits, ninf)
    v1 = jnp.max(el, axis=0, keepdims=True)
    i1 = jnp.min(jnp.where(el == v1, row, far), axis=0, keepdims=True)
    el2 = jnp.where(row == i1, ninf, el)
    v2 = jnp.max(el2, axis=0, keepdims=True)
    i2 = jnp.min(jnp.where(el2 == v2, row, far), axis=0, keepdims=True)
    e2 = jnp.exp(v2 - v1)
    w1 = p_grp / (1.0 + e2)
    w2 = p_grp * e2 / (1.0 + e2)
    l1 = i1 - e0
    l2 = i2 - e0
    first = l1 < l2
    a = jnp.where(first, l1, l2)
    b = jnp.where(first, l2, l1)
    wa = jnp.where(first, w1, w2)
    wb = jnp.where(first, w2, w1)
    pair = a * (5.0 - a) * 0.5 + b - 1.0
    bucket = g_idx * N_PAIRS + pair

    onehot = (row == bucket).astype(BF16)
    ri = lax.broadcasted_iota(I32, (TM, TM), 0)
    ci = lax.broadcasted_iota(I32, (TM, TM), 1)
    incl = _dot(onehot, (ri <= ci).astype(BF16))
    oh = onehot.astype(F32)
    rank = jnp.sum((incl - oh + carry_ref[:, 0:1]) * oh, axis=0, keepdims=True)
    carry_ref[...] = carry_ref[...] + jnp.broadcast_to(incl[:, TM - 1:TM], carry_ref.shape)
    cnt_ref[...] = carry_ref[...]

    r8 = lax.broadcasted_iota(I32, (SUBLANES, TM), 0)
    meta_ref[...] = jnp.where(r8 == 0, bucket, jnp.where(r8 == 1, rank, 0.0))
    wrows = jnp.where(r8 == 0, wa, jnp.where(r8 == 1, wb, 0.0))
    eye = (lax.broadcasted_iota(I32, (SUBLANES, LANES), 0)
           == lax.broadcasted_iota(I32, (SUBLANES, LANES), 1)).astype(BF16)
    wcols = None
    for piece in _split_bf16(wrows, 3):
        t = lax.dot_general(piece, eye, _TN, preferred_element_type=F32)
        wcols = t if wcols is None else wcols + t
    _to_token_tiles(xrow_ref, hn, TOK_PITCH)
    xrow_ref[pl.ds(TOK_ROWS, TM, stride=TOK_PITCH), :] = wcols
    for s in range(TOK_ROWS + 1, TOK_PITCH):
        xrow_ref[pl.ds(s, TM, stride=TOK_PITCH), :] = jnp.zeros((TM, LANES), F32)


def _out_proj(x2, yssd, yfoxt, wo1, wo2, g_ffn, wr, br):
    T = x2.shape[0]
    tm = TM_PROJ
    row = lambda i: (i, 0)
    const = lambda i: (0, 0)
    return pl.pallas_call(
        _outproj_kernel,
        grid=(T // tm,),
        in_specs=[pl.BlockSpec((tm, D_MODEL), row),
                  pl.BlockSpec((tm, D_SSD), row),
                  pl.BlockSpec((D_FOX, tm), lambda i: (0, i)),
                  pl.BlockSpec(wo1.shape, const),
                  pl.BlockSpec(wo2.shape, const),
                  pl.BlockSpec((1, D_MODEL), const),
                  pl.BlockSpec(wr.shape, const),
                  pl.BlockSpec(br.shape, const)],
        out_specs=[pl.BlockSpec((tm, D_MODEL), row),
                   pl.BlockSpec((tm * TOK_PITCH, LANES), row),
                   pl.BlockSpec((SUBLANES, tm), lambda i: (0, i)),
                   pl.BlockSpec((ROUTE_ROWS, LANES), const)],
        out_shape=[jax.ShapeDtypeStruct((T, D_MODEL), F32),
                   jax.ShapeDtypeStruct((T * TOK_PITCH, LANES), F32),
                   jax.ShapeDtypeStruct((SUBLANES, T), F32),
                   jax.ShapeDtypeStruct((ROUTE_ROWS, LANES), F32)],
        scratch_shapes=[pltpu.VMEM((ROUTE_ROWS, LANES), F32)],
        compiler_params=pltpu.CompilerParams(dimension_semantics=("arbitrary",),
                                             vmem_limit_bytes=VMEM_LIMIT),
        name="out_proj_router",
    )(x2, yssd, yfoxt, wo1, wo2, g_ffn, wr, br)


def _dispatch_kernel(ztile_ref, pos_ref, src_ref, dst_hbm, zero_ref, sem, zsem):
    TD = pos_ref.shape[-1]
    tile_rows = TM_MOE * TOK_PITCH

    @pl.when(pl.program_id(0) == 0)
    def _():
        zero_ref[...] = jnp.zeros(zero_ref.shape, F32)

        def zcopy(e):
            dst = pl.multiple_of(ztile_ref[e] * tile_rows, tile_rows)
            return pltpu.make_async_copy(zero_ref, dst_hbm.at[pl.ds(dst, tile_rows)], zsem)

        def zstart(e, c):
            @pl.when(ztile_ref[e] >= 0)
            def _():
                zcopy(e).start()
            return c

        def zwait(e, c):
            @pl.when(ztile_ref[e] >= 0)
            def _():
                zcopy(e).wait()
            return c

        lax.fori_loop(0, ztile_ref.shape[0], zstart, 0)
        lax.fori_loop(0, ztile_ref.shape[0], zwait, 0)

    def copy(r):
        dst = pl.multiple_of(pos_ref[0, r] * TOK_PITCH, PITCH_ALIGN)
        src = pl.multiple_of(r * TOK_PITCH, PITCH_ALIGN)
        return pltpu.make_async_copy(src_ref.at[pl.ds(src, TOK_PITCH)], dst_hbm.at[pl.ds(dst, TOK_PITCH)], sem)

    def start(rb, c):
        for u in range(DMA_ISSUE_UNROLL):
            copy(DMA_ISSUE_UNROLL * rb + u).start(priority=u % 2)
        return c

    lax.fori_loop(0, TD // DMA_ISSUE_UNROLL, start, 0)

    def wait(r, c):
        copy(r).wait()
        return c

    lax.fori_loop(0, TD, wait, 0, unroll=DMA_WAIT_UNROLL)


def _dispatch(ztiles, pos, xrow, n_rows):
    T = pos.shape[0]
    td = TD_DISP
    pos3 = pos.reshape(T // td, 1, td)
    grid_spec = pltpu.PrefetchScalarGridSpec(
        num_scalar_prefetch=1,
        grid=(T // td,),
        in_specs=[pl.BlockSpec((None, 1, td), lambda i, zt: (i, 0, 0), memory_space=pltpu.SMEM),
                  pl.BlockSpec((td * TOK_PITCH, LANES), lambda i, zt: (i, 0))],
        out_specs=pl.BlockSpec(memory_space=pl.ANY),
        scratch_shapes=[pltpu.VMEM((TM_MOE * TOK_PITCH, LANES), F32),
                        pltpu.SemaphoreType.DMA(()), pltpu.SemaphoreType.DMA(())],
    )
    return pl.pallas_call(
        _dispatch_kernel,
        grid_spec=grid_spec,
        out_shape=jax.ShapeDtypeStruct((n_rows * TOK_PITCH, LANES), F32),
        compiler_params=pltpu.CompilerParams(dimension_semantics=("arbitrary",),
                                             vmem_limit_bytes=VMEM_LIMIT),
        name="dispatch",
    )(ztiles, pos3, xrow)


def _experts_kernel(ea_ref, eb_ref, blk_ref, valid_ref, xs_ref, *refs):
    del ea_ref, eb_ref, blk_ref
    w_refs, ys_ref = refs[:-1], refs[-1]
    n = EXPERT_TILES_PER_STEP
    TM = TM_MOE
    first = pl.program_id(0) * n

    def xs_tile(t):
        return xs_ref.at[pl.ds(t * TM * TOK_PITCH, TM * TOK_PITCH)]

    def ys_tile(t):
        return ys_ref.at[pl.ds(t * TM * TOK_ROWS, TM * TOK_ROWS)]

    def compute(t):
        wga_ref, wua_ref, wda_ref, wgb_ref, wub_ref, wdb_ref = w_refs[6 * t:6 * t + 6]
        x = _from_token_tiles(xs_tile(t), TM, TOK_PITCH).astype(BF16)
        w = xs_tile(t)[pl.ds(TOK_ROWS, TM, stride=TOK_PITCH), :]

        def expert(wg_ref, wu_ref, wd_ref, wt):
            g = _dot(x, wg_ref[...])
            u = _dot(x, wu_ref[...])
            return _dot((g * _sigmoid(g) * u * wt).astype(BF16), wd_ref[...])

        y = expert(wga_ref, wua_ref, wda_ref, w[:, 0:1]) + expert(wgb_ref, wub_ref, wdb_ref, w[:, 1:2])
        _to_token_tiles(ys_tile(t), y, TOK_ROWS)

    def clear(t):
        ys_tile(t)[...] = jnp.zeros((TM * TOK_ROWS, LANES), F32)

    in_use = valid_ref[first]
    for t in range(1, n):
        in_use = in_use + valid_ref[first + t]
    for k in range(n + 1):
        @pl.when(in_use == k)
        def _(k=k):
            for t in range(n):
                compute(t) if t < k else clear(t)


def _experts(ea, eb, blk, valid, xs, wg, wu, wd):
    tm = TM_MOE
    n = EXPERT_TILES_PER_STEP
    n_rows = xs.shape[0] // TOK_PITCH
    nt = n_rows // tm
    assert nt % n == 0
    xmap = lambda i, ea, eb, blk, valid: (blk[i * n] // n, 0)
    up_spec = lambda m: pl.BlockSpec((None, D_MODEL, D_EXPERT), m)
    dn_spec = lambda m: pl.BlockSpec((None, D_EXPERT, D_MODEL), m)
    w_specs, w_args = [], []
    for t in range(n):
        amap = lambda i, ea, eb, blk, valid, t=t: (ea[i * n + t], 0, 0)
        bmap = lambda i, ea, eb, blk, valid, t=t: (eb[i * n + t], 0, 0)
        w_specs += [up_spec(amap), up_spec(amap), dn_spec(amap), up_spec(bmap), up_spec(bmap), dn_spec(bmap)]
        w_args += [wg, wu, wd, wg, wu, wd]
    grid_spec = pltpu.PrefetchScalarGridSpec(
        num_scalar_prefetch=4,
        grid=(nt // n,),
        in_specs=[pl.BlockSpec((n * tm * TOK_PITCH, LANES), xmap)] + w_specs,
        out_specs=pl.BlockSpec((n * tm * TOK_ROWS, LANES), lambda i, ea, eb, blk, valid: (i, 0)),
    )
    return pl.pallas_call(
        _experts_kernel,
        grid_spec=grid_spec,
        out_shape=jax.ShapeDtypeStruct((n_rows * TOK_ROWS, LANES), F32),
        compiler_params=pltpu.CompilerParams(dimension_semantics=("arbitrary",),
                                             vmem_limit_bytes=VMEM_LIMIT),
        name="experts",
    )(ea, eb, blk, valid, xs, *w_args)


def _combine_kernel(pos_cur_ref, pos_nxt_ref, h1_ref, p_ref, wpp_ref, wpg_ref, gple_ref, gfin_ref,
                    ys_hbm, o_ref, buf_ref, sem):
    TM = h1_ref.shape[0]
    i = pl.program_id(0)
    n = pl.num_programs(0)
    slot = i % 2

    def copy(pos_ref, s, r):
        src = pl.multiple_of(pos_ref[0, r] * TOK_ROWS, TOK_ROWS)
        dst = pl.multiple_of(r * TOK_PITCH, PITCH_ALIGN)
        return pltpu.make_async_copy(ys_hbm.at[pl.ds(src, TOK_ROWS)], buf_ref.at[s, pl.ds(dst, TOK_ROWS)],
                                     sem.at[s])

    def gather(pos_ref, s):
        def start(rb, c):
            for u in range(DMA_ISSUE_UNROLL):
                copy(pos_ref, s, DMA_ISSUE_UNROLL * rb + u).start(priority=u % 2)
            return c
        lax.fori_loop(0, TM // DMA_ISSUE_UNROLL, start, 0)

    @pl.when(i == 0)
    def _():
        gather(pos_cur_ref, 0)

    @pl.when(i + 1 < n)
    def _():
        gather(pos_nxt_ref, 1 - slot)

    def wait(r, c):
        copy(pos_cur_ref, slot, r).wait()
        return c

    lax.fori_loop(0, TM, wait, 0, unroll=DMA_WAIT_UNROLL)

    h2 = h1_ref[...] + _from_token_tiles(buf_ref.at[slot], TM, TOK_PITCH)
    n3 = _rms(h2, gple_ref[...]).astype(BF16)
    gate = _sigmoid(_dot(n3, wpg_ref[...]))
    pp = _dot(p_ref[...].astype(BF16), wpp_ref[...])
    o_ref[...] = _rms(h2 + pp * gate, gfin_ref[...])


def _combine(pos, h1, p2, wpp, wpg, g_ple, g_final, ys):
    T = h1.shape[0]
    tm = TM_COMB
    nt = T // tm
    pos3 = pos.reshape(nt, 1, tm)
    row = lambda i: (i, 0)
    const = lambda i: (0, 0)
    return pl.pallas_call(
        _combine_kernel,
        grid=(nt,),
        in_specs=[pl.BlockSpec((None, 1, tm), lambda i: (i, 0, 0), memory_space=pltpu.SMEM),
                  pl.BlockSpec((None, 1, tm), lambda i: (jnp.minimum(i + 1, nt - 1), 0, 0),
                               memory_space=pltpu.SMEM),
                  pl.BlockSpec((tm, D_MODEL), row),
                  pl.BlockSpec((tm, D_PLE), row),
                  pl.BlockSpec(wpp.shape, const),
                  pl.BlockSpec(wpg.shape, const),
                  pl.BlockSpec((1, D_MODEL), const),
                  pl.BlockSpec((1, D_MODEL), const),
                  pl.BlockSpec(memory_space=pl.ANY)],
        out_specs=pl.BlockSpec((tm, D_MODEL), row),
        out_shape=jax.ShapeDtypeStruct((T, D_MODEL), F32),
        scratch_shapes=[pltpu.VMEM((2, tm * TOK_PITCH, LANES), F32),
                        pltpu.SemaphoreType.DMA((2,))],
        compiler_params=pltpu.CompilerParams(dimension_semantics=("arbitrary",),
                                             vmem_limit_bytes=VMEM_LIMIT),
        name="combine_ple",
    )(pos3, pos3, h1, p2, wpp, wpg, g_ple, g_final, ys)


def _tile_tables(counts, n_tiles):
    nt = (counts + TM_MOE - 1) // TM_MOE
    cum = jnp.cumsum(nt)
    off = (cum - nt) * TM_MOE
    total = cum[-1]
    tile = jnp.arange(n_tiles, dtype=I32)
    valid = (tile < total).astype(I32)
    blk = jnp.maximum(jnp.minimum(tile, total - 1), 0)
    tb = jnp.minimum(jnp.sum((cum[None, :] <= blk[:, None]).astype(I32), axis=1), N_BUCKETS - 1)
    pair_a = jnp.array([0, 0, 0, 1, 1, 2], I32)
    pair_b = jnp.array([1, 2, 3, 2, 3, 3], I32)
    grp = tb // N_PAIRS
    ea = grp * EXPERTS_PER_GROUP + pair_a[tb % N_PAIRS]
    eb = grp * EXPERTS_PER_GROUP + pair_b[tb % N_PAIRS]
    last = jnp.where(nt > 0, cum - 1, -1)
    tail = total + jnp.arange(N_BUCKETS, dtype=I32)
    ztiles = jnp.concatenate([last, jnp.where(tail < n_tiles, tail, -1)]).astype(I32)
    return off, ea, eb, blk, valid, ztiles


def kernel(x, p, w_in, conv_w, conv_b, dt_bias, a_log, d_skip, g_ssd, fox_fbias, w_out, g_mix, g_ffn,
           w_route_group, b_route_group, w_route_expert, b_route_expert, w_exp_gate, w_exp_up, w_exp_down,
           g_ple, w_ple_proj, w_ple_gate, g_final):
    B, S, _ = x.shape
    T = B * S
    assert S % (L_SSD * SSD_CHUNKS_PER_STEP) == 0 and S % TQ_FOX == 0
    assert S % (TM_PROJ * INPROJ_TILES_PER_STEP) == 0 and T % TM_PROJ == 0
    assert T % TD_DISP == 0 and T % TM_COMB == 0
    x2 = x.reshape(T, D_MODEL)
    p2 = p[0].reshape(T, D_PLE)

    wi = w_in[0]
    o_xbc = D_SSD
    o_dt = o_xbc + D_XBC
    o_q = o_dt + SSD_HEADS
    o_k = o_q + D_FOX
    o_v = o_k + D_FOX
    o_f = o_v + D_FOX
    wz = wi[:, 0:o_xbc].astype(BF16)
    wxbc = wi[:, o_xbc:o_dt].astype(BF16)
    wq = wi[:, o_q:o_k].astype(BF16)
    wk = wi[:, o_k:o_v].astype(BF16)
    wvt = wi[:, o_v:o_f].T.astype(BF16)
    w_dt = wi[:, o_dt:o_q]
    w_f = wi[:, o_f:o_f + FOX_HEADS]
    wsm = jnp.pad(w_dt, ((0, 0), (0, LANES - SSD_HEADS))).astype(BF16)
    wsmt = jnp.concatenate([w_dt, w_f], axis=1).T.astype(BF16)
    lane_pad = lambda v: jnp.pad(v.reshape(1, -1), ((0, 0), (0, LANES - v.shape[-1])))
    colb = lane_pad(dt_bias[0])
    fbias = jnp.broadcast_to(fox_fbias[0][:, None], (FOX_HEADS, LANES))
    cola = lane_pad(a_log[0])
    rowb = jnp.broadcast_to(dt_bias[0][:, None], (SSD_HEADS, LANES))
    rowa = jnp.broadcast_to(a_log[0][:, None], (SSD_HEADS, LANES))
    dexp = jnp.repeat(d_skip[0], SSD_HEAD_DIM).reshape(1, D_SSD)
    expand = (jnp.arange(LANES)[:, None] == (jnp.arange(D_SSD) // SSD_HEAD_DIM)[None, :]).astype(BF16)
    col = jnp.arange(D_QK)
    head = col // LANES
    j_bias = col % LANES - jnp.where(head % 2 == 0, FOX_HEAD_DIM, 0)
    bias_col = (j_bias >= 0) & (j_bias < F_PARTS)
    qones = bias_col.astype(F32).reshape(1, D_QK)
    src_row = j_bias * FOX_HEADS + head
    place = (bias_col[None, :] & (jnp.arange((F_PARTS + 1) * FOX_HEADS)[:, None] == src_row[None, :])).astype(BF16)
    wo = w_out[0].astype(BF16)
    wr = jnp.pad(jnp.concatenate([w_route_group[0], w_route_expert[0]], axis=1).T,
                 ((0, ROUTE_ROWS - N_GROUPS - N_EXPERTS), (0, 0)))
    wr_hi = wr.astype(BF16)
    wr = jnp.concatenate([wr_hi, (wr - wr_hi.astype(F32)).astype(BF16)], axis=0)
    br = jnp.broadcast_to(jnp.pad(jnp.concatenate([b_route_group[0], b_route_expert[0]]),
                                  (0, ROUTE_ROWS - N_GROUPS - N_EXPERTS))[:, None], (ROUTE_ROWS, LANES))
    wg = w_exp_gate[0].reshape(N_EXPERTS, D_MODEL, D_EXPERT).astype(BF16)
    wu = w_exp_up[0].reshape(N_EXPERTS, D_MODEL, D_EXPERT).astype(BF16)
    wd = w_exp_down[0].reshape(N_EXPERTS, D_EXPERT, D_MODEL).astype(BF16)

    z, xbc, q, k, vt, sm, smt = _in_proj(x2, g_mix[0].reshape(1, -1), wz, wxbc, wq, wk, wvt, wsm, wsmt,
                                         fbias, qones, place, S)
    yssd = _ssd(xbc, z, sm, smt, conv_w[0].reshape(SSD_CONV, D_XBC), conv_b[0].reshape(1, -1),
                colb, cola, rowb, rowa, dexp, g_ssd[0].reshape(1, -1), expand, B, S)
    yfoxt = _fox(q, k, vt, B, S)
    h1, xrow, meta, cnt = _out_proj(x2, yssd, yfoxt, wo[0:D_SSD], wo[D_SSD:], g_ffn[0].reshape(1, -1), wr, br)

    n_rows = T + N_BUCKETS * TM_MOE
    bucket = meta[0].astype(I32)
    rank = meta[1].astype(I32)
    off, ea, eb, blk, valid, ztiles = _tile_tables(cnt[0:N_BUCKETS, 0].astype(I32), n_rows // TM_MOE)
    pos = rank
    for b in range(N_BUCKETS):
        pos = pos + jnp.where(bucket == b, off[b], 0)

    xs = _dispatch(ztiles, pos, xrow, n_rows)
    ys = _experts(ea, eb, blk, valid, xs, wg, wu, wd)
    out = _combine(pos, h1, p2, w_ple_proj[0].astype(BF16), w_ple_gate[0].astype(BF16),
                   g_ple[0].reshape(1, -1), g_final.reshape(1, -1), ys)
    return out.reshape(B, S, D_MODEL)
```
